```python
import jax, jax.numpy as jnp
from jax import lax
import numpy as np

D_MODEL = 2048
BATCH = 4
SEQ = 4096
DEPTH = 1
DEC_BATCH = 32
DEC_SEQ = 8
PAST_LEN = 16384
PAGE_SIZE = 128

A_WINDOWS = (128, 512, 2048)
A_DILATIONS = (1, 4, 16)
A_GROUPS = 3
A_HEADS = 8
A_HEAD_DIM = 128
A_WIDTH = A_HEADS * A_HEAD_DIM
A_QBLOCK = 128
ATT_SCALE = A_HEAD_DIM ** -0.5
SSM_D_INNER = D_MODEL
SSM_HEAD_DIM = 64
SSM_HEADS = SSM_D_INNER // SSM_HEAD_DIM
SSM_GROUPS = 8
SSM_D_STATE = 128
SSM_CONV_W = 4
SSM_CHUNK = 128
SSM_CONV_DIM = SSM_D_INNER + 2 * SSM_GROUPS * SSM_D_STATE
QKV_WIDTH = A_GROUPS * A_WIDTH
IN_SPLITS = (QKV_WIDTH, QKV_WIDTH, QKV_WIDTH, A_WIDTH, SSM_D_INNER, SSM_CONV_DIM, SSM_HEADS, D_MODEL, D_MODEL)
IN_WIDTH = sum(IN_SPLITS)
NORM_EPS = 1e-6

kernel_name = "hybrid_dilated_attn_ssd_step"


def _rmsnorm(x, g):
    xf = x.astype(jnp.float32)
    y = xf * lax.rsqrt(jnp.mean(xf * xf, axis=-1, keepdims=True) + NORM_EPS)
    return (y * g.astype(jnp.float32)).astype(x.dtype)


def _masked_softmax_lse(scores, mask):
    s = jnp.where(mask, scores, -jnp.inf)
    m = jnp.max(s, axis=-1, keepdims=True)
    e = jnp.exp(s - m)
    z = jnp.sum(e, axis=-1, keepdims=True)
    return e / z, (m + jnp.log(z))[..., 0]


def _dilated_attn_prompt(q, k, v, window, dilation):
    bsz, s, h, e = q.shape
    nk = window // dilation
    L = s // dilation
    qb = A_QBLOCK
    nb = -(-L // qb)
    lp = nb * qb

    def to_sub(t):
        return t.reshape(bsz, L, dilation, h, e).transpose(0, 2, 1, 3, 4)

    qs = jnp.pad(to_sub(q), ((0, 0), (0, 0), (0, lp - L), (0, 0), (0, 0))).reshape(bsz, dilation, nb, qb, h, e)
    kpad = ((0, 0), (0, 0), (nk, lp - L), (0, 0), (0, 0))
    ks = jnp.pad(to_sub(k), kpad)
    vs = jnp.pad(to_sub(v), kpad)
    kj = jnp.arange(qb + nk)
    idx = jnp.arange(nb)[:, None] * qb + kj[None, :]
    kb = ks[:, :, idx]
    vb = vs[:, :, idx]
    scores = jnp.einsum('bdnqhe,bdnkhe->bdnhqk', qs, kb, preferred_element_type=jnp.float32) * ATT_SCALE
    dist = jnp.arange(qb)[:, None] + nk - kj[None, :]
    key_sub = idx[:, None, :] - nk
    mask = (dist >= 0) & (dist <= nk) & (key_sub >= 0)
    p, lse = _masked_softmax_lse(scores, mask[None, None, :, None])
    o = jnp.einsum('bdnhqk,bdnkhe->bdnqhe', p.astype(vb.dtype), vb, preferred_element_type=jnp.float32)
    o = o.reshape(bsz, dilation, lp, h, e)[:, :, :L].transpose(0, 2, 1, 3, 4).reshape(bsz, s, h, e)
    lse = lse.transpose(0, 1, 2, 4, 3).reshape(bsz, dilation, lp, h)[:, :, :L]
    lse = lse.transpose(0, 2, 1, 3).reshape(bsz, s, h)
    return o, lse


def _dilated_attn_decode(q, k_new, v_new, kv_buf, window, dilation):
    wb = kv_buf.shape[1]
    t = q.shape[1]
    nk = window // dilation
    k_ctx = jnp.concatenate([kv_buf[:, :, 0], k_new], axis=1)
    v_ctx = jnp.concatenate([kv_buf[:, :, 1], v_new], axis=1)
    idx = wb + jnp.arange(t)[:, None] - jnp.arange(nk + 1)[None, :] * dilation
    valid = idx >= 0
    idxc = jnp.maximum(idx, 0)
    kg = k_ctx[:, idxc]
    vg = v_ctx[:, idxc]
    scores = jnp.einsum('bthe,btkhe->bthk', q, kg, preferred_element_type=jnp.float32) * ATT_SCALE
    p, lse = _masked_softmax_lse(scores, valid[None, :, None, :])
    o = jnp.einsum('bthk,btkhe->bthe', p.astype(vg.dtype), vg, preferred_element_type=jnp.float32)
    new_buf = jnp.concatenate([kv_buf, jnp.stack([k_new, v_new], axis=2)], axis=1)[:, t:]
    return o, lse, new_buf


def _ssd(xs, dt, a, bm, cm, h0):
    f32 = jnp.float32
    bsz, l, nh, p = xs.shape
    g, n = bm.shape[2], bm.shape[3]
    r = nh // g
    chunk = SSM_CHUNK if l % SSM_CHUNK == 0 else l
    nc = l // chunk
    xdt = (xs.astype(f32) * dt[..., None]).reshape(bsz, nc, chunk, g, r, p)
    bc = bm.astype(f32).reshape(bsz, nc, chunk, g, n)
    cc = cm.astype(f32).reshape(bsz, nc, chunk, g, n)
    da = (dt * a).reshape(bsz, nc, chunk, g, r).transpose(0, 3, 4, 1, 2)
    cs = jnp.cumsum(da, axis=-1)
    tri = jnp.tril(jnp.ones((chunk, chunk), dtype=bool))
    decay = jnp.exp(jnp.where(tri, cs[..., :, None] - cs[..., None, :], -jnp.inf))
    cb = jnp.einsum('bclgn,bcsgn->bgcls', cc, bc)
    y_diag = jnp.einsum('bgcls,bgrcls,bcsgrp->bclgrp', cb, decay, xdt)
    decay_states = jnp.exp(cs[..., -1:] - cs)
    states = jnp.einsum('bcsgn,bgrcs,bcsgrp->bcgrpn', bc, decay_states, xdt)
    states = jnp.concatenate([h0.astype(f32).reshape(bsz, 1, g, r, p, n), states], axis=1)
    cz = jnp.cumsum(jnp.pad(cs[..., -1], ((0, 0), (0, 0), (0, 0), (1, 0))), axis=-1)
    tri_c = jnp.tril(jnp.ones((nc + 1, nc + 1), dtype=bool))
    decay_chunk = jnp.exp(jnp.where(tri_c, cz[..., :, None] - cz[..., None, :], -jnp.inf))
    new_states = jnp.einsum('bgrzc,bcgrpn->bzgrpn', decay_chunk, states)
    y_off = jnp.einsum('bclgn,bcgrpn,bgrcl->bclgrp', cc, new_states[:, :-1], jnp.exp(cs))
    y = (y_diag + y_off).reshape(bsz, l, nh, p)
    final = new_states[:, -1].reshape(bsz, nh, p, n).astype(h0.dtype)
    return y, final


def _layer(x, c, kv_bufs, conv_state, ssm_state, decode,
           w_ada, b_ada, g_pre, g_post, w_in, conv_w, conv_b, dt_bias, a_log, d_skip, g_ssm, w_o_a, w_o_b, w_o):
    bsz, l, _ = x.shape
    mod = jax.nn.silu(c) @ w_ada + b_ada
    shift, scale, gate = jnp.split(mod, 3, axis=-1)
    h = _rmsnorm(x, g_pre) * (1 + scale[:, None]) + shift[:, None]
    proj = h @ w_in
    cuts = [int(i) for i in np.cumsum(IN_SPLITS)[:-1]]
    q, k, v, z_a, z_b, xbc, dt_raw, r_a, r_b = jnp.split(proj, cuts, axis=-1)

    q = q.reshape(bsz, l, A_GROUPS, A_HEADS, A_HEAD_DIM)
    k = k.reshape(bsz, l, A_GROUPS, A_HEADS, A_HEAD_DIM)
    v = v.reshape(bsz, l, A_GROUPS, A_HEADS, A_HEAD_DIM)
    outs, lses, new_bufs = [], [], []
    for gi in range(A_GROUPS):
        w, d = A_WINDOWS[gi], A_DILATIONS[gi]
        qg, kg, vg = q[:, :, gi], k[:, :, gi], v[:, :, gi]
        if decode:
            o, lse, nbuf = _dilated_attn_decode(qg, kg, vg, kv_bufs[gi], w, d)
        else:
            o, lse = _dilated_attn_prompt(qg, kg, vg, w, d)
            nbuf = jnp.stack([kg, vg], axis=2)[:, -min(w, l):]
        outs.append(o)
        lses.append(lse)
        new_bufs.append(nbuf)
    wts = jax.nn.softmax(jnp.stack(lses, axis=0), axis=0)
    o_a = jnp.sum(wts[..., None] * jnp.stack(outs, axis=0), axis=0)
    o_a = o_a.reshape(bsz, l, A_WIDTH).astype(x.dtype) * jax.nn.silu(z_a)
    p_a = o_a @ w_o_a

    ext = jnp.concatenate([conv_state, xbc], axis=1)
    conv = conv_b
    for kk in range(SSM_CONV_W):
        conv = conv + conv_w[kk] * ext[:, kk:kk + l]
    new_conv = ext[:, l:]
    xbc_act = jax.nn.silu(conv)
    xs, bm, cm = jnp.split(xbc_act, [SSM_D_INNER, SSM_D_INNER + SSM_GROUPS * SSM_D_STATE], axis=-1)
    xs = xs.reshape(bsz, l, SSM_HEADS, SSM_HEAD_DIM)
    bm = bm.reshape(bsz, l, SSM_GROUPS, SSM_D_STATE)
    cm = cm.reshape(bsz, l, SSM_GROUPS, SSM_D_STATE)
    dt = jax.nn.softplus(dt_raw.astype(jnp.float32) + dt_bias.astype(jnp.float32))
    a = -jnp.exp(a_log.astype(jnp.float32))
    y, new_ssm = _ssd(xs, dt, a, bm, cm, ssm_state)
    y = y + xs.astype(jnp.float32) * d_skip.astype(jnp.float32)[:, None]
    yg = y.reshape(bsz, l, SSM_D_INNER) * jax.nn.silu(z_b.astype(jnp.float32))
    yg = yg.reshape(bsz, l, SSM_GROUPS, SSM_D_INNER // SSM_GROUPS)
    yg = yg * lax.rsqrt(jnp.mean(yg * yg, axis=-1, keepdims=True) + NORM_EPS)
    yn = (yg.reshape(bsz, l, SSM_D_INNER) * g_ssm.astype(jnp.float32)).astype(x.dtype)
    p_b = yn @ w_o_b

    merged = jax.nn.sigmoid(r_a) * p_a + jax.nn.sigmoid(r_b) * p_b
    out = merged @ w_o
    y_out = x + gate[:, None] * _rmsnorm(out, g_post)
    return y_out, new_bufs, new_conv, new_ssm


def setup_inputs(seed: int = 0) -> dict:
    key = jax.random.key(seed)
    ks = jax.random.split(key, 24)
    f32 = jnp.float32

    def nrm(k, shape, s):
        return jax.random.normal(k, shape, f32) * s

    wb = [min(w, PAST_LEN) for w in A_WINDOWS]
    kv_shape = lambda n: (DEPTH, DEC_BATCH, n, 2, A_HEADS, A_HEAD_DIM)
    dt0 = jnp.exp(jax.random.uniform(ks[15], (DEPTH, SSM_HEADS), f32, np.log(1e-3), np.log(1e-1)))
    return {
        "x_prompt": nrm(ks[0], (BATCH, SEQ, D_MODEL), 1.0),
        "x_sample": nrm(ks[1], (DEC_BATCH, DEC_SEQ, D_MODEL), 1.0),
        "cache_a_w128": nrm(ks[2], kv_shape(wb[0]), 1.0),
        "cache_a_w512": nrm(ks[3], kv_shape(wb[1]), 1.0),
        "cache_a_w2048": nrm(ks[4], kv_shape(wb[2]), 1.0),
        "state_conv": nrm(ks[5], (DEPTH, DEC_BATCH, SSM_CONV_W - 1, SSM_CONV_DIM), 1.0),
        "state_ssm": nrm(ks[6], (DEPTH, DEC_BATCH, SSM_HEADS, SSM_HEAD_DIM, SSM_D_STATE), 0.1),
        "c_prompt": nrm(ks[7], (BATCH, D_MODEL), 1.0),
        "c_sample": nrm(ks[8], (DEC_BATCH, D_MODEL), 1.0),
        "w_ada": nrm(ks[9], (DEPTH, D_MODEL, 3 * D_MODEL), 0.2 * D_MODEL ** -0.5),
        "b_ada": nrm(ks[10], (DEPTH, 3 * D_MODEL), 0.01),
        "g_pre": 1.0 + nrm(ks[11], (DEPTH, D_MODEL), 0.05),
        "g_post": 1.0 + nrm(ks[12], (DEPTH, D_MODEL), 0.05),
        "w_in": nrm(ks[13], (DEPTH, D_MODEL, IN_WIDTH), D_MODEL ** -0.5),
        "conv_w": nrm(ks[14], (DEPTH, SSM_CONV_W, SSM_CONV_DIM), SSM_CONV_W ** -0.5),
        "conv_b": nrm(ks[16], (DEPTH, SSM_CONV_DIM), 0.01),
        "dt_bias": dt0 + jnp.log(-jnp.expm1(-dt0)),
        "a_log": jnp.log(jax.random.uniform(ks[17], (DEPTH, SSM_HEADS), f32, 1.0, 16.0)),
        "d_skip": 1.0 + nrm(ks[18], (DEPTH, SSM_HEADS), 0.05),
        "g_ssm": 1.0 + nrm(ks[19], (DEPTH, SSM_D_INNER), 0.05),
        "w_o_a": nrm(ks[20], (DEPTH, A_WIDTH, D_MODEL), A_WIDTH ** -0.5),
        "w_o_b": nrm(ks[21], (DEPTH, SSM_D_INNER, D_MODEL), SSM_D_INNER ** -0.5),
        "w_o": nrm(ks[22], (DEPTH, D_MODEL, D_MODEL), D_MODEL ** -0.5),
    }


def reference(x_prompt, x_sample, cache_a_w128, cache_a_w512, cache_a_w2048, state_conv, state_ssm,
              c_prompt, c_sample, w_ada, b_ada, g_pre, g_post, w_in, conv_w, conv_b, dt_bias, a_log,
              d_skip, g_ssm, w_o_a, w_o_b, w_o):
    y_prompt, y_sample = x_prompt, x_sample
    pk128, pk512, pk2048, pconv, pssm = [], [], [], [], []
    sk128, sk512, sk2048, sconv, sssm = [], [], [], [], []
    for i in range(DEPTH):
        lw = (w_ada[i], b_ada[i], g_pre[i], g_post[i], w_in[i], conv_w[i], conv_b[i], dt_bias[i], a_log[i],
              d_skip[i], g_ssm[i], w_o_a[i], w_o_b[i], w_o[i])
        b_p = x_prompt.shape[0]
        conv0 = jnp.zeros((b_p, SSM_CONV_W - 1, SSM_CONV_DIM), x_prompt.dtype)
        ssm0 = jnp.zeros((b_p, SSM_HEADS, SSM_HEAD_DIM, SSM_D_STATE), x_prompt.dtype)
        y_prompt, bufs_p, conv_p, ssm_p = _layer(y_prompt, c_prompt, None, conv0, ssm0, False, *lw)
        y_sample, bufs_s, conv_s, ssm_s = _layer(
            y_sample, c_sample, (cache_a_w128[i], cache_a_w512[i], cache_a_w2048[i]),
            state_conv[i], state_ssm[i], True, *lw)
        pk128.append(bufs_p[0]); pk512.append(bufs_p[1]); pk2048.append(bufs_p[2])
        pconv.append(conv_p); pssm.append(ssm_p)
        sk128.append(bufs_s[0]); sk512.append(bufs_s[1]); sk2048.append(bufs_s[2])
        sconv.append(conv_s); sssm.append(ssm_s)
    p_kv128, p_kv512, p_kv2048 = jnp.stack(pk128), jnp.stack(pk512), jnp.stack(pk2048)
    p_conv, p_ssm = jnp.stack(pconv), jnp.stack(pssm)
    s_kv128, s_kv512, s_kv2048 = jnp.stack(sk128), jnp.stack(sk512), jnp.stack(sk2048)
    s_conv, s_ssm = jnp.stack(sconv), jnp.stack(sssm)
    return (y_prompt, y_sample, p_kv128, p_kv512, p_kv2048, p_conv, p_ssm, s_kv128, s_kv512, s_kv2048, s_conv, s_ssm)
```

```python
import functools

import numpy as np
import jax
import jax.numpy as jnp
from jax import lax
from jax.experimental import pallas as pl
from jax.experimental.pallas import tpu as pltpu

F32 = jnp.float32
BF16 = jnp.bfloat16

D_MODEL = 2048
A_WINDOWS = (128, 512, 2048)
A_DILATIONS = (1, 4, 16)
A_GROUPS = 3
A_HEADS = 8
A_HEAD_DIM = 128
A_WIDTH = A_HEADS * A_HEAD_DIM
ATT_SCALE = A_HEAD_DIM ** -0.5
SSM_D_INNER = D_MODEL
SSM_HEAD_DIM = 64
SSM_HEADS = SSM_D_INNER // SSM_HEAD_DIM
SSM_GROUPS = 8
SSM_D_STATE = 128
SSM_CONV_W = 4
SSM_CHUNK = 128
SSM_CONV_DIM = SSM_D_INNER + 2 * SSM_GROUPS * SSM_D_STATE
QKV_WIDTH = A_GROUPS * A_WIDTH
NORM_EPS = 1e-6

COL = 1024
Q_BLK, K_BLK, V_BLK, ZA_BLK, ZB_BLK, XBC_BLK, RA_BLK, RB_BLK = 0, 3, 6, 9, 10, 12, 16, 18
N_MAIN = 20 * COL
DT_COL0 = 3 * QKV_WIDTH + A_WIDTH + SSM_D_INNER + SSM_CONV_DIM
DT_PAD = 128

LANES = 128
SUBLANES = 8
VMEM_LIMIT = 56 * 1024 * 1024
NEG = -1e30


def _cparams(*sem):
    return pltpu.CompilerParams(dimension_semantics=sem, vmem_limit_bytes=VMEM_LIMIT)


def _dot(a, b):
    return jnp.dot(a, b, preferred_element_type=F32)


def _dot_nt(a, b):
    return lax.dot_general(a, b, (((1,), (1,)), ((), ())), preferred_element_type=F32)


def _dot_f32_by_01(x, e):
    x1 = x.astype(BF16)
    r1 = x - x1.astype(F32)
    x2 = r1.astype(BF16)
    x3 = (r1 - x2.astype(F32)).astype(BF16)
    return _dot(x1, e) + _dot(x2, e) + _dot(x3, e)


def _dot_01_by_f32(e, x):
    x1 = x.astype(BF16)
    r1 = x - x1.astype(F32)
    x2 = r1.astype(BF16)
    x3 = (r1 - x2.astype(F32)).astype(BF16)
    return _dot(e, x1) + _dot(e, x2) + _dot(e, x3)


def _silu(x):
    return x * jax.nn.sigmoid(x)


def _ada_kernel(c_ref, w_ref, b_ref, o_ref):
    s = _silu(c_ref[...]).astype(BF16)
    o_ref[...] = _dot(s, w_ref[...].astype(BF16)) + b_ref[...]


def _ada(c, w_ada, b_ada):
    m, d = c.shape
    n = w_ada.shape[1]
    tn = 512
    return pl.pallas_call(
        _ada_kernel,
        grid=(n // tn,),
        in_specs=[pl.BlockSpec((m, d), lambda j: (0, 0)),
                  pl.BlockSpec((d, tn), lambda j: (0, j)),
                  pl.BlockSpec((1, tn), lambda j: (0, j))],
        out_specs=pl.BlockSpec((m, tn), lambda j: (0, j)),
        out_shape=jax.ShapeDtypeStruct((m, n), F32),
        compiler_params=_cparams("arbitrary"),
        name="ada",
    )(c, w_ada, b_ada.reshape(1, n))


def _in_proj_kernel(x_ref, sc_ref, sh_ref, g_ref, w_ref, wdt_ref, o_ref, dt_ref, h_scr, *, tm, rc):
    @pl.when(pl.program_id(1) == 0)
    def _():
        per_row = sc_ref.shape[0] != 1

        def body(r, carry):
            rows = pl.ds(pl.multiple_of(r * rc, rc), rc)
            x = x_ref[rows, :]
            y = x * lax.rsqrt(jnp.mean(x * x, axis=-1, keepdims=True) + NORM_EPS) * g_ref[...]
            sc = sc_ref[rows, :] if per_row else sc_ref[...]
            sh = sh_ref[rows, :] if per_row else sh_ref[...]
            h_scr[rows, :] = (y * (1.0 + sc) + sh).astype(BF16)
            return carry

        lax.fori_loop(0, tm // rc, body, 0)
        dt_ref[...] = _dot(h_scr[...], wdt_ref[...])

    o_ref[...] = _dot(h_scr[...], w_ref[...])


def _in_proj(x2d, sc3, sh3, g_pre, w_main, w_dt, tm, rows_per_mod):
    t, d = x2d.shape
    r = sc3.shape[1]
    tn = COL
    rc = min(tm, 256)
    mod_map = lambda i, j: ((i * tm) // rows_per_mod, 0, 0)
    return pl.pallas_call(
        functools.partial(_in_proj_kernel, tm=tm, rc=rc),
        grid=(t // tm, N_MAIN // tn),
        in_specs=[pl.BlockSpec((tm, d), lambda i, j: (i, 0)),
                  pl.BlockSpec((None, r, d), mod_map),
                  pl.BlockSpec((None, r, d), mod_map),
                  pl.BlockSpec((1, d), lambda i, j: (0, 0)),
                  pl.BlockSpec((d, tn), lambda i, j: (0, j)),
                  pl.BlockSpec((d, DT_PAD), lambda i, j: (0, 0))],
        out_specs=[pl.BlockSpec((tm, tn), lambda i, j: (i, j)),
                   pl.BlockSpec((tm, DT_PAD), lambda i, j: (i, 0))],
        out_shape=[jax.ShapeDtypeStruct((t, N_MAIN), F32),
                   jax.ShapeDtypeStruct((t, DT_PAD), F32)],
        scratch_shapes=[pltpu.VMEM((tm, d), BF16)],
        compiler_params=_cparams("arbitrary", "arbitrary"),
        name="in_proj",
    )(x2d, sc3, sh3, g_pre.reshape(1, d), w_main, w_dt)


def _attn_prompt_kernel(q_ref, kc_ref, kp_ref, vc_ref, vp_ref, o_ref, lse_ref, *, tq):
    i = pl.program_id(2)
    hd = A_HEAD_DIM
    row = lax.broadcasted_iota(jnp.int32, (hd, hd), 0)
    col = lax.broadcasted_iota(jnp.int32, (hd, hd), 1)
    cur_ok = col <= row
    prev_ok = col >= row
    prev_ok_first = col >= row + jnp.where(i > 0, 0, 2 * hd)
    for sb in range(tq // hd):
        rs = slice(sb * hd, (sb + 1) * hd)
        lse_acc = jnp.zeros((hd, LANES), F32)
        for h in range(A_HEADS):
            hs = slice(h * hd, (h + 1) * hd)
            qh = q_ref[rs, hs].astype(BF16)
            kc = kc_ref[rs, hs].astype(BF16)
            vc = vc_ref[rs, hs].astype(BF16)
            if sb == 0:
                kp = kp_ref[:, hs].astype(BF16)
                vp = vp_ref[:, hs].astype(BF16)
                pmask = prev_ok_first
            else:
                ps = slice((sb - 1) * hd, sb * hd)
                kp = kc_ref[ps, hs].astype(BF16)
                vp = vc_ref[ps, hs].astype(BF16)
                pmask = prev_ok
            s_c = jnp.where(cur_ok, _dot_nt(qh, kc) * ATT_SCALE, NEG)
            s_p = jnp.where(pmask, _dot_nt(qh, kp) * ATT_SCALE, NEG)
            m = jnp.maximum(jnp.max(s_c, axis=-1, keepdims=True), jnp.max(s_p, axis=-1, keepdims=True))
            e_c = jnp.exp(s_c - m)
            e_p = jnp.exp(s_p - m)
            z = jnp.sum(e_c, axis=-1, keepdims=True) + jnp.sum(e_p, axis=-1, keepdims=True)
            o = _dot(e_c.astype(BF16), vc) + _dot(e_p.astype(BF16), vp)
            o_ref[rs, hs] = o / z
            lse_acc = jnp.where(col == h, m + jnp.log(z), lse_acc)
        lse_ref[rs, :] = lse_acc


def _attn_prompt(proj3, g, d, tq):
    b, s, _ = proj3.shape
    l = s // d
    nblk = N_MAIN // COL
    pv = proj3.reshape(b, l, d * N_MAIN)
    hd = A_HEAD_DIM
    nsb = tq // hd
    cur = lambda blk: pl.BlockSpec((None, tq, COL), lambda bi, r, i: (bi, i, r * nblk + blk + g))
    prev = lambda blk: pl.BlockSpec((None, hd, COL),
                                    lambda bi, r, i: (bi, jnp.maximum(i * nsb - 1, 0), r * nblk + blk + g))
    o, lse = pl.pallas_call(
        functools.partial(_attn_prompt_kernel, tq=tq),
        grid=(b, d, l // tq),
        in_specs=[cur(Q_BLK), cur(K_BLK), prev(K_BLK), cur(V_BLK), prev(V_BLK)],
        out_specs=[pl.BlockSpec((None, tq, A_WIDTH), lambda bi, r, i: (bi, i, r)),
                   pl.BlockSpec((None, tq, LANES), lambda bi, r, i: (bi, i, r))],
        out_shape=[jax.ShapeDtypeStruct((b, l, d * A_WIDTH), F32),
                   jax.ShapeDtypeStruct((b, l, d * LANES), F32)],
        compiler_params=_cparams("arbitrary", "arbitrary", "arbitrary"),
        name=f"attn_p{g}",
    )(pv, pv, pv, pv, pv)
    return o.reshape(b * s, A_WIDTH), lse.reshape(b * s, LANES)


def _attn_decode_kernel(q_ref, kn_ref, vn_ref, cache_ref, next_ref, o_ref, lse_ref, kv_ref,
                        qbd_scr, m_scr, l_scr, acc_scr, *, tr, d, t):
    j = pl.program_id(1)
    last = j == pl.num_programs(1) - 1
    nrow = A_HEADS * t

    @pl.when(j == 0)
    def _():
        qt = jnp.concatenate([q_ref[...]] * A_HEADS, axis=0)
        r = lax.broadcasted_iota(jnp.int32, (nrow, A_WIDTH), 0)
        c = lax.broadcasted_iota(jnp.int32, (nrow, A_WIDTH), 1)
        qbd_scr[...] = jnp.where((r >> 3) == (c >> 7), qt, 0.0).astype(BF16)
        m_scr[...] = jnp.full(m_scr.shape, NEG, F32)
        l_scr[...] = jnp.zeros(l_scr.shape, F32)
        acc_scr[...] = jnp.zeros(acc_scr.shape, F32)

    kv_ref[0:tr - t, :] = cache_ref[t:tr, :]

    @pl.when(last)
    def _():
        kv_ref[tr - t:tr, 0:A_WIDTH] = kn_ref[...]
        kv_ref[tr - t:tr, A_WIDTH:2 * A_WIDTH] = vn_ref[...]

    @pl.when(jnp.logical_not(last))
    def _():
        kv_ref[tr - t:tr, :] = next_ref[...]

    def accumulate(kb, vb, valid):
        s = jnp.where(valid, _dot_nt(qbd_scr[...], kb) * ATT_SCALE, NEG)
        m_old = m_scr[...]
        m_new = jnp.maximum(m_old, jnp.max(s, axis=-1, keepdims=True))
        alpha = jnp.exp(m_old - m_new)
        p = jnp.exp(s - m_new)
        l_scr[...] = l_scr[...] * alpha + jnp.sum(p, axis=-1, keepdims=True)
        acc_scr[...] = acc_scr[...] * alpha + _dot(p.astype(BF16), vb)
        m_scr[...] = m_new

    tq = lax.broadcasted_iota(jnp.int32, (nrow, tr), 0) & (t - 1)
    cg = j * tr + lax.broadcasted_iota(jnp.int32, (nrow, tr), 1)
    valid = jnp.logical_and(cg >= tq, ((cg - tq) & (d - 1)) == 0)
    accumulate(cache_ref[:, 0:A_WIDTH].astype(BF16), cache_ref[:, A_WIDTH:2 * A_WIDTH].astype(BF16), valid)

    @pl.when(last)
    def _():
        pad = jnp.zeros((LANES - t, A_WIDTH), F32)
        kn = jnp.concatenate([kn_ref[...], pad], axis=0).astype(BF16)
        vn = jnp.concatenate([vn_ref[...], pad], axis=0).astype(BF16)
        tq2 = lax.broadcasted_iota(jnp.int32, (nrow, LANES), 0) & (t - 1)
        tk = lax.broadcasted_iota(jnp.int32, (nrow, LANES), 1)
        valid2 = jnp.logical_and(tk <= tq2, ((tq2 - tk) & (d - 1)) == 0)
        accumulate(kn, vn, valid2)
        o = acc_scr[...] / l_scr[...]
        lse = m_scr[...] + jnp.log(l_scr[...])
        lane = lax.broadcasted_iota(jnp.int32, (t, LANES), 1)
        lse_acc = jnp.zeros((t, LANES), F32)
        for h in range(A_HEADS):
            hs = slice(h * A_HEAD_DIM, (h + 1) * A_HEAD_DIM)
            o_ref[:, hs] = o[h * t:(h + 1) * t, hs]
            lse_acc = jnp.where(lane == h, lse[h * t:(h + 1) * t, :], lse_acc)
        lse_ref[...] = lse_acc


def _attn_decode(proj_s3, cache, g, d, tr):
    b, t, _ = proj_s3.shape
    wb = cache.shape[1]
    assert t == SUBLANES and wb % tr == 0 and tr % LANES == 0 and wb == A_WINDOWS[g]
    cache2 = cache.reshape(b, wb, 2 * A_WIDTH)
    nrow = A_HEADS * t
    new = lambda blk: pl.BlockSpec((None, t, COL), lambda bi, j: (bi, 0, blk + g))
    o, lse, kv = pl.pallas_call(
        functools.partial(_attn_decode_kernel, tr=tr, d=d, t=t),
        grid=(b, wb // tr),
        in_specs=[new(Q_BLK), new(K_BLK), new(V_BLK),
                  pl.BlockSpec((None, tr, 2 * A_WIDTH), lambda bi, j: (bi, j, 0)),
                  pl.BlockSpec((None, t, 2 * A_WIDTH),
                               lambda bi, j: (bi, jnp.minimum((j + 1) * (tr // t), wb // t - 1), 0))],
        out_specs=[pl.BlockSpec((None, t, A_WIDTH), lambda bi, j: (bi, 0, 0)),
                   pl.BlockSpec((None, t, LANES), lambda bi, j: (bi, 0, 0)),
                   pl.BlockSpec((None, tr, 2 * A_WIDTH), lambda bi, j: (bi, j, 0))],
        out_shape=[jax.ShapeDtypeStruct((b, t, A_WIDTH), F32),
                   jax.ShapeDtypeStruct((b, t, LANES), F32),
                   jax.ShapeDtypeStruct((b, wb, 2 * A_WIDTH), F32)],
        scratch_shapes=[pltpu.VMEM((nrow, A_WIDTH), BF16),
                        pltpu.VMEM((nrow, 1), F32),
                        pltpu.VMEM((nrow, 1), F32),
                        pltpu.VMEM((nrow, A_WIDTH), F32)],
        compiler_params=_cparams("arbitrary", "arbitrary"),
        name=f"attn_d{g}",
    )(proj_s3, proj_s3, proj_s3, cache2, cache2)
    return o.reshape(b * t, A_WIDTH), lse.reshape(b * t, LANES), kv.reshape(cache.shape)


def _ssd_kernel(xbc_ref, dt_ref, zb_ref, conv0_ref, h0_ref, convw_ref, convb_ref, dtb_ref, alog_ref,
                dskip_ref, gssm_ref, e_ref, yn_ref, convo_ref, ho_ref, ext_scr, st_scr, y_scr, *, nv):
    q = SSM_CHUNK
    c = pl.program_id(1)
    di = SSM_D_INNER
    gw = di // SSM_GROUPS
    ns = SSM_D_STATE
    hp = SUBLANES
    kw = SSM_CONV_W

    @pl.when(c == 0)
    def _():
        ext_scr[0:hp, :] = conv0_ref[...]
        for j in range(di // LANES):
            st_scr[:, j * LANES:(j + 1) * LANES] = h0_ref[j].T

    if nv == q:
        ext_scr[hp:hp + q, :] = xbc_ref[...]
    else:
        ext_scr[hp:hp + nv, :] = xbc_ref[...]
        ext_scr[hp + nv:hp + q, :] = jnp.zeros((q - nv, SSM_CONV_DIM), F32)

    def conv_act(c0, c1):
        acc = convb_ref[:, c0:c1]
        for k in range(kw):
            acc = acc + convw_ref[k:k + 1, c0:c1] * ext_scr[hp - (kw - 1) + k:hp - (kw - 1) + k + q, c0:c1]
        return _silu(acc)

    xs = conv_act(0, di)
    row =lax.broadcasted_iota(jnp.int32, (q, q), 0)
    col = lax.broadcasted_iota(jnp.int32, (q, q), 1)
    causal = row >= col
    if nv == q:
        dt_raw = dt_ref[...]
    else:
        dt_raw = jnp.concatenate([dt_ref[...], jnp.zeros((q - nv, DT_PAD), F32)], axis=0)
    dt = jax.nn.softplus(dt_raw + dtb_ref[...])
    if nv != q:
        dt = jnp.where(row < nv, dt, 0.0)
    da = dt * (-jnp.exp(alog_ref[...]))
    cs = _dot_01_by_f32(jnp.where(causal, 1.0, 0.0).astype(BF16), da)
    cs_t = cs.T
    e = e_ref[...]
    dt_x = _dot_f32_by_01(dt, e)
    cs_x = _dot_f32_by_01(cs, e)
    cs_last = cs_x[q - 1:q, :]
    xdt = xs * dt_x
    xdt_b = xdt.astype(BF16)
    w_b = (xdt * jnp.exp(cs_last - cs_x)).astype(BF16)
    ecs = jnp.exp(cs_x)
    e_last = jnp.exp(cs_last)
    half = lax.broadcasted_iota(jnp.int32, (q, LANES), 1) < SSM_HEAD_DIM

    for g in range(SSM_GROUPS):
        bg_t = conv_act(di + g * ns, di + (g + 1) * ns).T.astype(BF16)
        cg = conv_act(di + SSM_GROUPS * ns + g * ns, di + SSM_GROUPS * ns + (g + 1) * ns).astype(BF16)
        gs = slice(g * gw, (g + 1) * gw)
        cb = _dot(cg, bg_t)
        st = st_scr[:, gs]
        y_off = _dot(cg, st.astype(BF16)) * ecs[:, gs]
        st_scr[:, gs] = st * e_last[:, gs] + _dot(bg_t, w_b[:, gs])
        for pair in range(gw // LANES):
            h0 = (g * gw + pair * LANES) // SSM_HEAD_DIM
            ps = slice(g * gw + pair * LANES, g * gw + (pair + 1) * LANES)
            ys = []
            for h in (h0, h0 + 1):
                diff = cs[:, h:h + 1] - cs_t[h:h + 1, :]
                m_h = (cb * jnp.exp(jnp.where(causal, diff, NEG))).astype(BF16)
                ys.append(_dot(m_h, xdt_b[:, ps]))
            y_scr[:, ps] = jnp.where(half, ys[0], ys[1]) + y_off[:, pair * LANES:(pair + 1) * LANES]

    @pl.when(c == pl.num_programs(1) - 1)
    def _():
        for j in range(di // LANES):
            ho_ref[j] = st_scr[:, j * LANES:(j + 1) * LANES].T

    tail = ext_scr[hp + nv - (kw - 1):hp + nv, :]
    convo_ref[...] = tail
    ext_scr[hp - (kw - 1):hp, :] = tail

    y = y_scr[0:nv, :] + xs[0:nv, :] * dskip_ref[...]
    yg = y * _silu(zb_ref[...])
    for g in range(SSM_GROUPS):
        gs = slice(g * gw, (g + 1) * gw)
        v = yg[:, gs]
        vn = v * lax.rsqrt(jnp.mean(v * v, axis=-1, keepdims=True) + NORM_EPS)
        yn_ref[:, gs] = (vn * gssm_ref[:, gs]).astype(BF16)


def _ssd(proj3, dt3, conv0, h0, conv_w, conv_b, dt_bias, a_log, d_skip, g_ssm, nv):
    b, l, _ = proj3.shape
    q = SSM_CHUNK
    nc = l // nv
    assert nv == q or nc == 1
    di, cd, nh = SSM_D_INNER, SSM_CONV_DIM, SSM_HEADS
    hp = SUBLANES
    conv0p = jnp.pad(conv0, ((0, 0), (hp - (SSM_CONV_W - 1), 0), (0, 0)))
    h0v = h0.reshape(b, di // LANES, LANES, SSM_D_STATE)
    pad1 = lambda v: jnp.pad(v.reshape(1, nh), ((0, 0), (0, DT_PAD - nh)))
    expand = np.zeros((DT_PAD, di), np.float32)
    expand[np.arange(di) // SSM_HEAD_DIM, np.arange(di)] = 1.0
    const = lambda shape: pl.BlockSpec(shape, lambda bi, c: (0,) * len(shape))
    yn, conv_o, h_o = pl.pallas_call(
        functools.partial(_ssd_kernel, nv=nv),
        grid=(b, nc),
        in_specs=[pl.BlockSpec((None, nv, cd), lambda bi, c: (bi, c, XBC_BLK * COL // cd)),
                  pl.BlockSpec((None, nv, DT_PAD), lambda bi, c: (bi, c, 0)),
                  pl.BlockSpec((None, nv, di), lambda bi, c: (bi, c, ZB_BLK * COL // di)),
                  pl.BlockSpec((None, hp, cd), lambda bi, c: (bi, 0, 0)),
                  pl.BlockSpec((None, di // LANES, LANES, SSM_D_STATE), lambda bi, c: (bi, 0, 0, 0)),
                  const((SSM_CONV_W, cd)), const((1, cd)), const((1, DT_PAD)), const((1, DT_PAD)),
                  const((1, di)), const((1, di)), const((DT_PAD, di))],
        out_specs=[pl.BlockSpec((None, nv, di), lambda bi, c: (bi, c, 0)),
                   pl.BlockSpec((None, SSM_CONV_W - 1, cd), lambda bi, c: (bi, 0, 0)),
                   pl.BlockSpec((None, di // LANES, LANES, SSM_D_STATE), lambda bi, c: (bi, 0, 0, 0))],
        out_shape=[jax.ShapeDtypeStruct((b, l, di), BF16),
                   jax.ShapeDtypeStruct((b, SSM_CONV_W - 1, cd), F32),
                   jax.ShapeDtypeStruct((b, di // LANES, LANES, SSM_D_STATE), F32)],
        scratch_shapes=[pltpu.VMEM((hp + q, cd), F32),
                        pltpu.VMEM((SSM_D_STATE, di), F32),
                        pltpu.VMEM((q, di), F32)],
        compiler_params=_cparams("arbitrary", "arbitrary"),
        name=f"ssd{nv}",
    )(proj3, dt3, proj3, conv0p, h0v, conv_w, conv_b.reshape(1, cd), pad1(dt_bias), pad1(a_log),
      jnp.repeat(d_skip, SSM_HEAD_DIM).reshape(1, di), g_ssm.reshape(1, di), jnp.asarray(expand, BF16))
    return yn.reshape(b * l, di), conv_o, h_o.reshape(b, nh, SSM_HEAD_DIM, SSM_D_STATE)


def _mix_kernel(o0_ref, o1_ref, o2_ref, l0_ref, l1_ref, l2_ref, za_ref, yn_ref, ra_ref, rb_ref,
                woa_ref, wob_ref, mg_ref, oa_scr):
    l0, l1, l2 = l0_ref[...], l1_ref[...], l2_ref[...]
    m = jnp.maximum(jnp.maximum(l0, l1), l2)
    e0, e1, e2 = jnp.exp(l0 - m), jnp.exp(l1 - m), jnp.exp(l2 - m)
    z = e0 + e1 + e2
    w0, w1, w2 = e0 / z, e1 / z, e2 / z
    for h in range(A_HEADS):
        hs = slice(h * A_HEAD_DIM, (h + 1) * A_HEAD_DIM)
        oa = (w0[:, h:h + 1] * o0_ref[:, hs] + w1[:, h:h + 1] * o1_ref[:, hs]
              + w2[:, h:h + 1] * o2_ref[:, hs])
        oa_scr[:, hs] = (oa * _silu(za_ref[:, hs])).astype(BF16)
    p_a = _dot(oa_scr[...], woa_ref[...])
    p_b = _dot(yn_ref[...], wob_ref[...])
    mg_ref[...] = (jax.nn.sigmoid(ra_ref[...]) * p_a + jax.nn.sigmoid(rb_ref[...]) * p_b).astype(BF16)


def _mix(outs, lses, proj2, yn, w_o_a, w_o_b, tm):
    t = proj2.shape[0]
    d = D_MODEL
    row = lambda w, blk: pl.BlockSpec((tm, w), lambda i: (i, blk))
    const = lambda shape: pl.BlockSpec(shape, lambda i: (0, 0))
    return pl.pallas_call(
        _mix_kernel,
        grid=(t // tm,),
        in_specs=[row(A_WIDTH, 0)] * 3 + [row(LANES, 0)] * 3
        + [row(A_WIDTH, ZA_BLK * COL // A_WIDTH), row(d, 0),
           row(d, RA_BLK * COL // d), row(d, RB_BLK * COL // d),
           const((A_WIDTH, d)), const((d, d))],
        out_specs=row(d, 0),
        out_shape=jax.ShapeDtypeStruct((t, d), BF16),
        scratch_shapes=[pltpu.VMEM((tm, A_WIDTH), BF16)],
        compiler_params=_cparams("arbitrary"),
        name="mix",
    )(*outs, *lses, proj2, yn, proj2, proj2, w_o_a, w_o_b)


def _out_kernel(mg_ref, x_ref, gate_ref, gpost_ref, wo_ref, y_ref):
    out = _dot(mg_ref[...], wo_ref[...])
    nrm = out * lax.rsqrt(jnp.mean(out * out, axis=-1, keepdims=True) + NORM_EPS) * gpost_ref[...]
    y_ref[...] = x_ref[...] + gate_ref[...] * nrm


def _out(mg, x2d, gate3, g_post, w_o, tm, rows_per_mod):
    t, d = x2d.shape
    r = gate3.shape[1]
    return pl.pallas_call(
        _out_kernel,
        grid=(t // tm,),
        in_specs=[pl.BlockSpec((tm, d), lambda i: (i, 0)),
                  pl.BlockSpec((tm, d), lambda i: (i, 0)),
                  pl.BlockSpec((None, r, d), lambda i: ((i * tm) // rows_per_mod, 0, 0)),
                  pl.BlockSpec((1, d), lambda i: (0, 0)),
                  pl.BlockSpec((d, d), lambda i: (0, 0))],
        out_specs=pl.BlockSpec((tm, d), lambda i: (i, 0)),
        out_shape=jax.ShapeDtypeStruct((t, d), F32),
        compiler_params=_cparams("arbitrary"),
        name="out",
    )(mg, x2d, gate3, g_post.reshape(1, d), w_o)


def _window_rows(proj3, g, w):
    b, s, _ = proj3.shape
    k = proj3[:, s - w:, (K_BLK + g) * COL:(K_BLK + g + 1) * COL]
    v = proj3[:, s - w:, (V_BLK + g) * COL:(V_BLK + g + 1) * COL]
    return jnp.stack([k, v], axis=2).reshape(b, w, 2, A_HEADS, A_HEAD_DIM)


def _layer(x_p, x_s, caches, conv_s, ssm_s, c_p, c_s, w_ada, b_ada, g_pre, g_post, w_in, conv_w, conv_b,
           dt_bias, a_log, d_skip, g_ssm, w_o_a, w_o_b, w_o):
    bp, sp, d = x_p.shape
    bs, ts, _ = x_s.shape
    assert sp % (A_DILATIONS[-1] * 2 * A_HEAD_DIM) == 0 and sp % SSM_CHUNK == 0

    w_main = jnp.concatenate([w_in[:, :DT_COL0], w_in[:, DT_COL0 + SSM_HEADS:]], axis=1).astype(BF16)
    w_dt = jnp.pad(w_in[:, DT_COL0:DT_COL0 + SSM_HEADS], ((0, 0), (0, DT_PAD - SSM_HEADS))).astype(BF16)
    woa_b, wob_b, wo_b = w_o_a.astype(BF16), w_o_b.astype(BF16), w_o.astype(BF16)

    nmod = bp + bs
    mpad = -(-nmod // SUBLANES) * SUBLANES
    c_all = jnp.pad(jnp.concatenate([c_p, c_s], axis=0), ((0, mpad - nmod), (0, 0)))
    mod = _ada(c_all, w_ada, b_ada)
    shift, scale, gate = mod[:, :d], mod[:, d:2 * d], mod[:, 2 * d:]
    per_seq = lambda v: v[:bp].reshape(bp, 1, d)
    per_row = lambda v: jnp.repeat(v[bp:nmod], ts, axis=0).reshape(1, bs * ts, d)

    tp = bp * sp
    xp2 = x_p.reshape(tp, d)
    tm_p = 1024
    proj_p, dt_p = _in_proj(xp2, per_seq(scale), per_seq(shift), g_pre, w_main, w_dt, tm_p, sp)
    proj_p3 = proj_p.reshape(bp, sp, N_MAIN)
    outs, lses = [], []
    for g in range(A_GROUPS):
        o, lse = _attn_prompt(proj_p3, g, A_DILATIONS[g], 2 * A_HEAD_DIM)
        outs.append(o)
        lses.append(lse)
    conv0 = jnp.zeros((bp, SSM_CONV_W - 1, SSM_CONV_DIM), F32)
    ssm0 = jnp.zeros((bp, SSM_HEADS, SSM_HEAD_DIM, SSM_D_STATE), F32)
    yn_p, conv_p, ssm_p = _ssd(proj_p3, dt_p.reshape(bp, sp, DT_PAD), conv0, ssm0, conv_w, conv_b, dt_bias,
                               a_log, d_skip, g_ssm, SSM_CHUNK)
    mg_p = _mix(outs, lses, proj_p, yn_p, woa_b, wob_b, 256)
    y_p = _out(mg_p, xp2, per_seq(gate), g_post, wo_b, 512, sp).reshape(bp, sp, d)
    kv_p = [_window_rows(proj_p3, g, min(A_WINDOWS[g], sp)) for g in range(A_GROUPS)]

    tsn = bs * ts
    xs2 = x_s.reshape(tsn, d)
    proj_s, dt_s = _in_proj(xs2, per_row(scale), per_row(shift), g_pre, w_main, w_dt, tsn, tsn)
    proj_s3 = proj_s.reshape(bs, ts, N_MAIN)
    outs, lses, kv_s = [], [], []
    for g in range(A_GROUPS):
        o, lse, kv = _attn_decode(proj_s3, caches[g], g, A_DILATIONS[g], min(512, caches[g].shape[1]))
        outs.append(o)
        lses.append(lse)
        kv_s.append(kv)
    yn_s, conv_sn, ssm_sn = _ssd(proj_s3, dt_s.reshape(bs, ts, DT_PAD), conv_s, ssm_s, conv_w, conv_b,
                                 dt_bias, a_log, d_skip, g_ssm, ts)
    mg_s = _mix(outs, lses, proj_s, yn_s, woa_b, wob_b, tsn)
    y_s = _out(mg_s, xs2, per_row(gate), g_post, wo_b, tsn, tsn).reshape(bs, ts, d)
    return y_p, y_s, kv_p, conv_p, ssm_p, kv_s, conv_sn, ssm_sn


def kernel(x_prompt, x_sample, cache_a_w128, cache_a_w512, cache_a_w2048, state_conv, state_ssm, c_prompt, c_sample, w_ada, b_ada, g_pre, g_post, w_in, conv_w, conv_b, dt_bias, a_log, d_skip, g_ssm, w_o_a, w_o_b, w_o):
    depth = w_in.shape[0]
    y_p, y_s = x_prompt, x_sample
    acc = [[] for _ in range(10)]
    for i in range(depth):
        y_p, y_s, kv_p, conv_p, ssm_p, kv_s, conv_s, ssm_s = _layer(
            y_p, y_s, (cache_a_w128[i], cache_a_w512[i], cache_a_w2048[i]), state_conv[i], state_ssm[i],
            c_prompt, c_sample, w_ada[i], b_ada[i], g_pre[i], g_post[i], w_in[i], conv_w[i], conv_b[i],
            dt_bias[i], a_log[i], d_skip[i], g_ssm[i], w_o_a[i], w_o_b[i], w_o[i])
        for lst, v in zip(acc, (*kv_p, conv_p, ssm_p, *kv_s, conv_s, ssm_s)):
            lst.append(v)
    return (y_p, y_s, *[jnp.stack(v) for v in acc])
```

```python
import functools

import numpy as np
import jax
import jax.numpy as jnp
from jax import lax
from jax.experimental import pallas as pl
from jax.experimental.pallas import tpu as pltpu

F32 = jnp.float32
BF16 = jnp.bfloat16

D_MODEL = 2048
A_WINDOWS = (128, 512, 2048)
A_DILATIONS = (1, 4, 16)
A_GROUPS = 3
A_HEADS = 8
A_HEAD_DIM = 128
A_WIDTH = A_HEADS * A_HEAD_DIM
A_NK = 128
ATT_SCALE = A_HEAD_DIM ** -0.5
SSM_D_INNER = D_MODEL
SSM_HEAD_DIM = 64
SSM_HEADS = SSM_D_INNER // SSM_HEAD_DIM
SSM_GROUPS = 8
SSM_D_STATE = 128
SSM_CONV_W = 4
SSM_CHUNK = 128
SSM_CONV_DIM = SSM_D_INNER + 2 * SSM_GROUPS * SSM_D_STATE
QKV_WIDTH = A_GROUPS * A_WIDTH
NORM_EPS = 1e-6

LANES = 128
SUBLANES = 8
VMEM_LIMIT = 56 * 1024 * 1024
NEG = -1e30

COL = 1024
N_QKV_COLS = 3 * QKV_WIDTH
N_SLABS = N_QKV_COLS // LANES
XBC_OFF = 0
ZB_OFF = XBC_OFF + SSM_CONV_DIM
RA_OFF = ZB_OFF + SSM_D_INNER
RB_OFF = RA_OFF + D_MODEL
ZA_OFF = RB_OFF + D_MODEL
N_REST = ZA_OFF + A_WIDTH
DT_PAD = LANES
W_ZA = N_QKV_COLS
W_ZB = W_ZA + A_WIDTH
W_XBC = W_ZB + SSM_D_INNER
W_DT = W_XBC + SSM_CONV_DIM
W_RA = W_DT + SSM_HEADS
W_RB = W_RA + D_MODEL


def _cparams(*sem):
    return pltpu.CompilerParams(dimension_semantics=sem, vmem_limit_bytes=VMEM_LIMIT)


def _dot(a, b):
    return jnp.dot(a, b, preferred_element_type=F32)


def _dot_nt(a, b):
    return lax.dot_general(a, b, (((1,), (1,)), ((), ())), preferred_element_type=F32)


def _split3(x):
    x1 = x.astype(BF16)
    r1 = x - x1.astype(F32)
    x2 = r1.astype(BF16)
    x3 = (r1 - x2.astype(F32)).astype(BF16)
    return x1, x2, x3


def _dot_f32_by_01(x, e):
    x1, x2, x3 = _split3(x)
    return _dot(x1, e) + _dot(x2, e) + _dot(x3, e)


def _dot_01_by_f32(e, x):
    x1, x2, x3 = _split3(x)
    return _dot(e, x1) + _dot(e, x2) + _dot(e, x3)


def _silu(x):
    h = 0.5 * x
    return h + h * jnp.tanh(h)


def _ada_kernel(c_ref, w_ref, b_ref, o_ref):
    s = _silu(c_ref[...]).astype(BF16)
    o_ref[...] = _dot(s, w_ref[...].astype(BF16)) + b_ref[...]


def _ada(c, w_ada, b_ada):
    m, d = c.shape
    n = w_ada.shape[1]
    tn = 512
    return pl.pallas_call(
        _ada_kernel,
        grid=(n // tn,),
        in_specs=[pl.BlockSpec((m, d), lambda j: (0, 0)),
                  pl.BlockSpec((d, tn), lambda j: (0, j)),
                  pl.BlockSpec((1, tn), lambda j: (0, j))],
        out_specs=pl.BlockSpec((m, tn), lambda j: (0, j)),
        out_shape=jax.ShapeDtypeStruct((m, n), F32),
        compiler_params=_cparams("arbitrary"),
        name="ada",
    )(c, w_ada, b_ada.reshape(1, n))


def _in_proj_kernel(x_ref, sc_ref, sh_ref, g_ref, w_ref, wdt_ref, qkv_ref, rest_ref, dt_ref, h_scr,
                    *, tm, rc, n_qkv_steps):
    j = pl.program_id(1)

    @pl.when(j == 0)
    def _():
        per_row = sc_ref.shape[0] != 1

        def body(r, carry):
            rows = pl.ds(pl.multiple_of(r * rc, rc), rc)
            x = x_ref[rows, :]
            y = x * lax.rsqrt(jnp.mean(x * x, axis=-1, keepdims=True) + NORM_EPS) * g_ref[...]
            sc = sc_ref[rows, :] if per_row else sc_ref[...]
            sh = sh_ref[rows, :] if per_row else sh_ref[...]
            h_scr[rows, :] = (y * (1.0 + sc) + sh).astype(BF16)
            return carry

        lax.fori_loop(0, tm // rc, body, 0)
        dt_ref[...] = _dot(h_scr[...], wdt_ref[...])

    res = _dot(h_scr[...], w_ref[...])

    @pl.when(j < n_qkv_steps)
    def _():
        for c in range(COL // LANES):
            qkv_ref[c] = res[:, c * LANES:(c + 1) * LANES]

    @pl.when(j >= n_qkv_steps)
    def _():
        rest_ref[...] = res


def _in_proj(x2d, sc3, sh3, g_pre, w_main, w_dt, tm, rows_per_mod):
    t, d = x2d.shape
    r = sc3.shape[1]
    tn = COL
    rc = min(tm, 256)
    nq = N_QKV_COLS // tn
    spb = tn // LANES
    mod_map = lambda i, j: ((i * tm) // rows_per_mod, 0, 0)
    return pl.pallas_call(
        functools.partial(_in_proj_kernel, tm=tm, rc=rc, n_qkv_steps=nq),
        grid=(t // tm, (N_QKV_COLS + N_REST) // tn),
        in_specs=[pl.BlockSpec((tm, d), lambda i, j: (i, 0)),
                  pl.BlockSpec((None, r, d), mod_map),
                  pl.BlockSpec((None, r, d), mod_map),
                  pl.BlockSpec((1, d), lambda i, j: (0, 0)),
                  pl.BlockSpec((d, tn), lambda i, j: (0, j)),
                  pl.BlockSpec((d, DT_PAD), lambda i, j: (0, 0))],
        out_specs=[pl.BlockSpec((spb, tm, LANES), lambda i, j: (jnp.minimum(j, nq - 1), i, 0)),
                   pl.BlockSpec((tm, tn), lambda i, j: (i, jnp.maximum(j - nq, 0))),
                   pl.BlockSpec((tm, DT_PAD), lambda i, j: (i, 0))],
        out_shape=[jax.ShapeDtypeStruct((N_SLABS, t, LANES), F32),
                   jax.ShapeDtypeStruct((t, N_REST), F32),
                   jax.ShapeDtypeStruct((t, DT_PAD), F32)],
        scratch_shapes=[pltpu.VMEM((tm, d), BF16)],
        compiler_params=_cparams("arbitrary", "arbitrary"),
        name="in_proj",
    )(x2d, sc3, sh3, g_pre.reshape(1, d), w_main, w_dt)


def _attn_prompt_kernel(*refs, tb):
    ng = A_GROUPS
    q_refs, kc_refs, kp_refs = refs[0:ng], refs[ng:2 * ng], refs[2 * ng:3 * ng]
    vc_refs, vp_refs = refs[3 * ng:4 * ng], refs[4 * ng:5 * ng]
    za_ref, oa_ref, o_scr, lse_scr = refs[5 * ng:]
    i = pl.program_id(2)
    nk = A_NK
    row = lax.broadcasted_iota(jnp.int32, (nk, 2 * nk), 0)
    col = lax.broadcasted_iota(jnp.int32, (nk, 2 * nk), 1)
    ok = jnp.logical_and(col >= row, col <= row + nk)
    ok_first = jnp.logical_and(ok, col >= jnp.where(i > 0, 0, nk))

    def load(g, rows, k_prev_ref, v_prev_ref, prev_rows):
        qh = q_refs[g][rows, :].astype(BF16)
        k2 = jnp.concatenate([k_prev_ref[prev_rows, :], kc_refs[g][rows, :]], axis=0).astype(BF16)
        v2 = jnp.concatenate([v_prev_ref[prev_rows, :], vc_refs[g][rows, :]], axis=0).astype(BF16)
        return qh, k2, v2

    def attend(qh, k2, v2, mask):
        s = jnp.where(mask, _dot_nt(qh, k2) * ATT_SCALE, NEG)
        m = jnp.max(s, axis=-1, keepdims=True)
        e = jnp.exp(s - m)
        z = jnp.sum(e, axis=-1, keepdims=True)
        return _dot(e.astype(BF16), v2) / z, m + jnp.log(z)

    unroll = 4

    def rows_of(d, sb, r):
        start = sb * (nk * d) + r
        if d == 1:
            return pl.ds(pl.multiple_of(start, nk), nk)
        return pl.ds(start, nk, stride=d)

    def run(g, items):
        d = A_DILATIONS[g]
        loaded, masks = [], []
        for sb, r in items:
            if isinstance(sb, int) and sb == 0:
                loaded.append(load(g, rows_of(d, 0, r), kp_refs[g], vp_refs[g], rows_of(d, 0, r)))
                masks.append(ok_first)
            else:
                loaded.append(load(g, rows_of(d, sb, r), kc_refs[g], vc_refs[g], rows_of(d, sb - 1, r)))
                masks.append(ok)
        results = [attend(*x, mask) for x, mask in zip(loaded, masks)]
        for (sb, r), (o, lse) in zip(items, results):
            o_scr[g, rows_of(d, sb, r), :] = o
            lse_scr[g, rows_of(d, sb, r), :] = jnp.broadcast_to(lse, (nk, LANES))

    for g in range(ng):
        d = A_DILATIONS[g]
        nblk = tb // nk
        assert nblk % unroll == 0 and (d % unroll == 0 or unroll % d == 0)
        if d >= unroll:
            per_sb = d // unroll
            run_first = lambda it, c, g=g: (run(g, [(0, it * unroll + u) for u in range(unroll)]), c)[1]
            lax.fori_loop(0, per_sb, run_first, 0)
            if nblk > d:
                def later(it, c, g=g, per_sb=per_sb):
                    sb = 1 + it // per_sb
                    r0 = (it % per_sb) * unroll
                    run(g, [(sb, r0 + u) for u in range(unroll)])
                    return c
                lax.fori_loop(0, (nblk - d) // unroll, later, 0)
        else:
            sbs = unroll // d
            run(g, [(sb, r) for sb in range(sbs) for r in range(d)])
            def later(it, c, g=g, d=d, sbs=sbs):
                run(g, [(it * sbs + s, r) for s in range(sbs) for r in range(d)])
                return c
            lax.fori_loop(1, nblk // unroll, later, 0)

    mc = 256

    def merge(c, carry):
        rows = pl.ds(pl.multiple_of(c * mc, mc), mc)
        l0, l1, l2 = lse_scr[0, rows, :], lse_scr[1, rows, :], lse_scr[2, rows, :]
        m = jnp.maximum(jnp.maximum(l0, l1), l2)
        e0, e1, e2 = jnp.exp(l0 - m), jnp.exp(l1 - m), jnp.exp(l2 - m)
        inv = 1.0 / (e0 + e1 + e2)
        oa = (e0 * inv) * o_scr[0, rows, :] + (e1 * inv) * o_scr[1, rows, :] + (e2 * inv) * o_scr[2, rows, :]
        oa_ref[rows, :] = (oa * _silu(za_ref[rows, :])).astype(BF16)
        return carry

    lax.fori_loop(0, tb // mc, merge, 0)


def _attn_prompt(qkv, rest, b, s):
    t = b * s
    nk = A_NK
    tb = nk * A_DILATIONS[-1]
    assert s % tb == 0
    nb = s // tb
    nh = A_GROUPS * A_HEADS

    def cur(kind, g):
        return pl.BlockSpec((None, tb, LANES), lambda bi, h, i: (kind * nh + g * A_HEADS + h, bi * nb + i, 0))

    def prev(kind, g):
        pb = nk * A_DILATIONS[g]
        return pl.BlockSpec((None, pb, LANES),
                            lambda bi, h, i: (kind * nh + g * A_HEADS + h,
                                              jnp.maximum((bi * s + i * tb) // pb - 1, 0), 0))

    groups = range(A_GROUPS)
    in_specs = ([cur(0, g) for g in groups] + [cur(1, g) for g in groups] + [prev(1, g) for g in groups]
                + [cur(2, g) for g in groups] + [prev(2, g) for g in groups]
                + [pl.BlockSpec((tb, LANES), lambda bi, h, i: (bi * nb + i, ZA_OFF // LANES + h))])
    return pl.pallas_call(
        functools.partial(_attn_prompt_kernel, tb=tb),
        grid=(b, A_HEADS, nb),
        in_specs=in_specs,
        out_specs=pl.BlockSpec((tb, LANES), lambda bi, h, i: (bi * nb + i, h)),
        out_shape=jax.ShapeDtypeStruct((t, A_WIDTH), BF16),
        scratch_shapes=[pltpu.VMEM((A_GROUPS, tb, LANES), F32), pltpu.VMEM((A_GROUPS, tb, LANES), F32)],
        compiler_params=_cparams("arbitrary", "arbitrary", "arbitrary"),
        name="attn_p",
    )(*([qkv] * (5 * A_GROUPS)), rest)


def _attn_decode_kernel(qn_ref, kn_ref, vn_ref, za_ref, c0_ref, c1_ref, c2_ref, nx_ref,
                        oa_ref, kv0_ref, kv1_ref, kv2_ref, m_scr, l_scr, acc_scr, o_scr, lse_scr, *, tr, t):
    j = pl.program_id(1)
    last = j == pl.num_programs(1) - 1
    tile = (A_HEADS, A_HEAD_DIM)

    def shift(c_ref, kv_ref, n):
        kv_ref[0:n - t] = c_ref[t:n]

    def put_new(kv_ref, g, n):
        kv_ref[n - t:n, 0] = kn_ref[g]
        kv_ref[n - t:n, 1] = vn_ref[g]

    def part_cache(c_ref, g, tq, n_rows, row0):
        d = A_DILATIONS[g]
        start = tq % d
        n = n_rows // d
        k = c_ref[pl.ds(start, n, stride=d), 0]
        v = c_ref[pl.ds(start, n, stride=d), 1]
        q = qn_ref[g, tq]
        s = jnp.broadcast_to(jnp.sum(k * q[None], axis=-1, keepdims=True), (n,) + tile) * ATT_SCALE
        ri = row0 + start + d * lax.broadcasted_iota(jnp.int32, (n,) + tile, 0)
        s = jnp.where(ri >= tq, s, NEG)
        m = jnp.max(s, axis=0)
        p = jnp.exp(s - m[None])
        return m, jnp.sum(p, axis=0), jnp.sum(p * v, axis=0)

    def part_new(g, tq):
        d = A_DILATIONS[g]
        q = qn_ref[g, tq]
        js = [tk for tk in range(tq + 1) if (tq - tk) % d == 0]
        ss = [jnp.broadcast_to(jnp.sum(kn_ref[g, tk] * q, axis=-1, keepdims=True), tile) * ATT_SCALE
              for tk in js]
        m = functools.reduce(jnp.maximum, ss)
        ps = [jnp.exp(s - m) for s in ss]
        return m, sum(ps), sum(p * vn_ref[g, tk] for p, tk in zip(ps, js))

    def combine(a, b):
        m = jnp.maximum(a[0], b[0])
        fa, fb = jnp.exp(a[0] - m), jnp.exp(b[0] - m)
        return m, a[1] * fa + b[1] * fb, a[2] * fa + b[2] * fb

    @pl.when(j == 0)
    def _():
        for g, (c_ref, kv_ref) in enumerate(((c0_ref, kv0_ref), (c1_ref, kv1_ref))):
            n = A_WINDOWS[g]
            shift(c_ref, kv_ref, n)
            put_new(kv_ref, g, n)
            for tq in range(t):
                m, l, acc = combine(part_cache(c_ref, g, tq, n, 0), part_new(g, tq))
                o_scr[g, tq] = acc / l
                lse_scr[g, tq] = m + jnp.log(l)
        m_scr[...] = jnp.full(m_scr.shape, NEG, F32)
        l_scr[...] = jnp.zeros(l_scr.shape, F32)
        acc_scr[...] = jnp.zeros(acc_scr.shape, F32)

    g2 = A_GROUPS - 1
    shift(c2_ref, kv2_ref, tr)

    @pl.when(last)
    def _():
        put_new(kv2_ref, g2, tr)

    @pl.when(jnp.logical_not(last))
    def _():
        kv2_ref[tr - t:tr] = nx_ref[...]

    for tq in range(t):
        m, l, acc = combine((m_scr[tq], l_scr[tq], acc_scr[tq]), part_cache(c2_ref, g2, tq, tr, j * tr))
        m_scr[tq] = m
        l_scr[tq] = l
        acc_scr[tq] = acc

    @pl.when(last)
    def _():
        for tq in range(t):
            m, l, acc = combine((m_scr[tq], l_scr[tq], acc_scr[tq]), part_new(g2, tq))
            o2 = acc / l
            l2 = m + jnp.log(l)
            l0, l1 = lse_scr[0, tq], lse_scr[1, tq]
            mm = jnp.maximum(jnp.maximum(l0, l1), l2)
            e0, e1, e2 = jnp.exp(l0 - mm), jnp.exp(l1 - mm), jnp.exp(l2 - mm)
            inv = 1.0 / (e0 + e1 + e2)
            oa = (e0 * inv) * o_scr[0, tq] + (e1 * inv) * o_scr[1, tq] + (e2 * inv) * o2
            oa_ref[tq] = oa * _silu(za_ref[tq])


def _attn_decode(qkvn, za5, caches, depth_i, tr):
    _, ng, b, t, nh, hd = qkvn.shape
    wbs = [c.shape[2] for c in caches]
    assert t == SUBLANES and tuple(wbs) == A_WINDOWS and wbs[-1] % tr == 0 and tr % (A_DILATIONS[-1] * t) == 0
    new = lambda kind: pl.BlockSpec((None, ng, None, t, nh, hd), lambda bi, j: (kind, 0, bi, 0, 0, 0))
    whole = lambda n: pl.BlockSpec((None, None, n, 2, nh, hd), lambda bi, j: (depth_i, bi, 0, 0, 0, 0))
    whole_out = lambda n: pl.BlockSpec((None, n, 2, nh, hd), lambda bi, j: (bi, 0, 0, 0, 0))
    state = lambda: pltpu.VMEM((t, nh, hd), F32)
    res = pl.pallas_call(
        functools.partial(_attn_decode_kernel, tr=tr, t=t),
        grid=(b, wbs[-1] // tr),
        in_specs=[new(0), new(1), new(2),
                  pl.BlockSpec((None, t, nh, hd), lambda bi, j: (bi, 0, 0, 0)),
                  whole(wbs[0]), whole(wbs[1]),
                  pl.BlockSpec((None, None, tr, 2, nh, hd), lambda bi, j: (depth_i, bi, j, 0, 0, 0)),
                  pl.BlockSpec((None, None, t, 2, nh, hd),
                               lambda bi, j: (depth_i, bi, jnp.minimum((j + 1) * (tr // t), wbs[-1] // t - 1),
                                              0, 0, 0))],
        out_specs=[pl.BlockSpec((None, t, nh, hd), lambda bi, j: (bi, 0, 0, 0)),
                   whole_out(wbs[0]), whole_out(wbs[1]),
                   pl.BlockSpec((None, tr, 2, nh, hd), lambda bi, j: (bi, j, 0, 0, 0))],
        out_shape=[jax.ShapeDtypeStruct((b, t, nh, hd), F32)]
        + [jax.ShapeDtypeStruct((b, w, 2, nh, hd), F32) for w in wbs],
        scratch_shapes=[state(), state(), state(),
                        pltpu.VMEM((ng - 1, t, nh, hd), F32), pltpu.VMEM((ng - 1, t, nh, hd), F32)],
        compiler_params=_cparams("arbitrary", "arbitrary"),
        name="attn_d",
    )(qkvn, qkvn, qkvn, za5, caches[0], caches[1], caches[2], caches[2])
    return res[0], res[1:]


def _ssd_kernel(xbc_ref, dt_ref, zb_ref, conv0_ref, h0_ref, convw_ref, convb_ref, dtb_ref, alog_ref,
                dskip_ref, gssm_ref, e_ref, yn_ref, convo_ref, ho_ref, ext_scr, st_scr, y_scr, *, nv):
    q = SSM_CHUNK
    c = pl.program_id(1)
    di = SSM_D_INNER
    gw = di // SSM_GROUPS
    ns = SSM_D_STATE
    hp = SUBLANES
    kw = SSM_CONV_W

    nslab = SSM_CONV_DIM // LANES
    lanes_of = lambda j: slice(j * LANES, (j + 1) * LANES)

    @pl.when(c == 0)
    def _():
        for j in range(nslab):
            ext_scr[j, 0:hp, :] = conv0_ref[:, lanes_of(j)]
        for j in range(di // LANES):
            st_scr[:, lanes_of(j)] = h0_ref[j].T

    for j in range(nslab):
        ext_scr[j, hp:hp + nv, :] = xbc_ref[:, lanes_of(j)]
        if nv != q:
            ext_scr[j, hp + nv:hp + q, :] = jnp.zeros((q - nv, LANES), F32)

    def conv_act(c0, c1):
        pieces = []
        for j in range(c0 // LANES, c1 // LANES):
            acc = convb_ref[:, lanes_of(j)]
            for k in range(kw):
                acc = acc + convw_ref[k:k + 1, lanes_of(j)] * ext_scr[j, pl.ds(hp - (kw - 1) + k, q), :]
            pieces.append(_silu(acc))
        return pieces[0] if len(pieces) == 1 else jnp.concatenate(pieces, axis=1)

    xs = conv_act(0, di)
    row = lax.broadcasted_iota(jnp.int32, (q, q), 0)
    col = lax.broadcasted_iota(jnp.int32, (q, q), 1)
    causal = row >= col
    if nv == q:
        dt_raw = dt_ref[...]
    else:
        dt_raw = jnp.concatenate([dt_ref[...], jnp.zeros((q - nv, DT_PAD), F32)], axis=0)
    dt = jax.nn.softplus(dt_raw + dtb_ref[...])
    if nv != q:
        dt = jnp.where(row < nv, dt, 0.0)
    da = dt * (-jnp.exp(alog_ref[...]))
    cs = _dot_01_by_f32(jnp.where(causal, 1.0, 0.0).astype(BF16), da)
    cs_t = cs.T
    e = e_ref[...]
    dt_x = _dot_f32_by_01(dt, e)
    cs_x = _dot_f32_by_01(cs, e)
    cs_last = cs_x[q - 1:q, :]
    xdt = xs * dt_x
    xdt_b = xdt.astype(BF16)
    w_b = (xdt * jnp.exp(cs_last - cs_x)).astype(BF16)
    ecs = jnp.exp(cs_x)
    e_last = jnp.exp(cs_last)
    half = lax.broadcasted_iota(jnp.int32, (q, LANES), 1) < SSM_HEAD_DIM

    for g in range(SSM_GROUPS):
        bg_t = conv_act(di + g * ns, di + (g + 1) * ns).T.astype(BF16)
        cg = conv_act(di + SSM_GROUPS * ns + g * ns, di + SSM_GROUPS * ns + (g + 1) * ns).astype(BF16)
        gs = slice(g * gw, (g + 1) * gw)
        cb = _dot(cg, bg_t)
        st = st_scr[:, gs]
        y_off = _dot(cg, st.astype(BF16)) * ecs[:, gs]
        st_scr[:, gs] = st * e_last[:, gs] + _dot(bg_t, w_b[:, gs])
        for pair in range(gw // LANES):
            h0 = (g * gw + pair * LANES) // SSM_HEAD_DIM
            ps = slice(g * gw + pair * LANES, g * gw + (pair + 1) * LANES)
            ys = []
            for h in (h0, h0 + 1):
                diff = cs[:, h:h + 1] - cs_t[h:h + 1, :]
                m_h = (cb * jnp.exp(jnp.where(causal, diff, NEG))).astype(BF16)
                ys.append(_dot(m_h, xdt_b[:, ps]))
            y_scr[:, ps] = jnp.where(half, ys[0], ys[1]) + y_off[:, pair * LANES:(pair + 1) * LANES]

    @pl.when(c == pl.num_programs(1) - 1)
    def _():
        for j in range(di // LANES):
            ho_ref[j] = st_scr[:, j * LANES:(j + 1) * LANES].T

    for j in range(nslab):
        tail = ext_scr[j, hp + nv - (kw - 1):hp + nv, :]
        convo_ref[:, lanes_of(j)] = tail
        ext_scr[j, hp - (kw - 1):hp, :] = tail

    y = y_scr[0:nv, :] + xs[0:nv, :] * dskip_ref[...]
    yg = y * _silu(zb_ref[...])
    for g in range(SSM_GROUPS):
        gs = slice(g * gw, (g + 1) * gw)
        v = yg[:, gs]
        vn = v * lax.rsqrt(jnp.mean(v * v, axis=-1, keepdims=True) + NORM_EPS)
        yn_ref[:, gs] = (vn * gssm_ref[:, gs]).astype(BF16)


def _ssd(rest3, dt3, conv0, h0, conv_w, conv_b, dt_bias, a_log, d_skip, g_ssm, nv):
    b, l, _ = rest3.shape
    q = SSM_CHUNK
    nc = l // nv
    assert nv == q or nc == 1
    di, cd, nh = SSM_D_INNER, SSM_CONV_DIM, SSM_HEADS
    hp = SUBLANES
    conv0p = jnp.pad(conv0, ((0, 0), (hp - (SSM_CONV_W - 1), 0), (0, 0)))
    h0v = h0.reshape(b, di // LANES, LANES, SSM_D_STATE)
    pad1 = lambda v: jnp.pad(v.reshape(1, nh), ((0, 0), (0, DT_PAD - nh)))
    expand = np.zeros((DT_PAD, di), np.float32)
    expand[np.arange(di) // SSM_HEAD_DIM, np.arange(di)] = 1.0
    const = lambda shape: pl.BlockSpec(shape, lambda bi, c: (0,) * len(shape))
    yn, conv_o, h_o = pl.pallas_call(
        functools.partial(_ssd_kernel, nv=nv),
        grid=(b, nc),
        in_specs=[pl.BlockSpec((None, nv, cd), lambda bi, c: (bi, c, XBC_OFF // cd)),
                  pl.BlockSpec((None, nv, DT_PAD), lambda bi, c: (bi, c, 0)),
                  pl.BlockSpec((None, nv, di), lambda bi, c: (bi, c, ZB_OFF // di)),
                  pl.BlockSpec((None, hp, cd), lambda bi, c: (bi, 0, 0)),
                  pl.BlockSpec((None, di // LANES, LANES, SSM_D_STATE), lambda bi, c: (bi, 0, 0, 0)),
                  const((SSM_CONV_W, cd)), const((1, cd)), const((1, DT_PAD)), const((1, DT_PAD)),
                  const((1, di)), const((1, di)), const((DT_PAD, di))],
        out_specs=[pl.BlockSpec((None, nv, di), lambda bi, c: (bi, c, 0)),
                   pl.BlockSpec((None, SSM_CONV_W - 1, cd), lambda bi, c: (bi, 0, 0)),
                   pl.BlockSpec((None, di // LANES, LANES, SSM_D_STATE), lambda bi, c: (bi, 0, 0, 0))],
        out_shape=[jax.ShapeDtypeStruct((b, l, di), BF16),
                   jax.ShapeDtypeStruct((b, SSM_CONV_W - 1, cd), F32),
                   jax.ShapeDtypeStruct((b, di // LANES, LANES, SSM_D_STATE), F32)],
        scratch_shapes=[pltpu.VMEM((cd // LANES, hp + q, LANES), F32),
                        pltpu.VMEM((SSM_D_STATE, di), F32),
                        pltpu.VMEM((q, di), F32)],
        compiler_params=_cparams("arbitrary", "arbitrary"),
        name=f"ssd{nv}",
    )(rest3, dt3, rest3, conv0p, h0v, conv_w, conv_b.reshape(1, cd), pad1(dt_bias), pad1(a_log),
      jnp.repeat(d_skip, SSM_HEAD_DIM).reshape(1, di), g_ssm.reshape(1, di), jnp.asarray(expand, BF16))
    return yn.reshape(b * l, di), conv_o, h_o.reshape(b, nh, SSM_HEAD_DIM, SSM_D_STATE)


def _mix_kernel(oa_ref, yn_ref, ra_ref, rb_ref, woa_ref, wob_ref, mg_ref):
    p_a = _dot(oa_ref[...], woa_ref[...])
    p_b = _dot(yn_ref[...], wob_ref[...])
    mg_ref[...] = (jax.nn.sigmoid(ra_ref[...]) * p_a + jax.nn.sigmoid(rb_ref[...]) * p_b).astype(BF16)


def _mix(oa, yn, rest, w_o_a, w_o_b, tm):
    t = rest.shape[0]
    d = D_MODEL
    row = lambda w, off: pl.BlockSpec((tm, w), lambda i: (i, off // w))
    const = lambda shape: pl.BlockSpec(shape, lambda i: (0, 0))
    return pl.pallas_call(
        _mix_kernel,
        grid=(t // tm,),
        in_specs=[row(A_WIDTH, 0), row(d, 0), row(d, RA_OFF), row(d, RB_OFF),
                  const((A_WIDTH, d)), const((d, d))],
        out_specs=row(d, 0),
        out_shape=jax.ShapeDtypeStruct((t, d), BF16),
        compiler_params=_cparams("arbitrary"),
        name="mix",
    )(oa, yn, rest, rest, w_o_a, w_o_b)


def _out_kernel(mg_ref, x_ref, gate_ref, gpost_ref, wo_ref, y_ref):
    out = _dot(mg_ref[...], wo_ref[...])
    nrm = out * lax.rsqrt(jnp.mean(out * out, axis=-1, keepdims=True) + NORM_EPS) * gpost_ref[...]
    y_ref[...] = x_ref[...] + gate_ref[...] * nrm


def _out(mg, x2d, gate3, g_post, w_o, tm, rows_per_mod):
    t, d = x2d.shape
    r = gate3.shape[1]
    return pl.pallas_call(
        _out_kernel,
        grid=(t // tm,),
        in_specs=[pl.BlockSpec((tm, d), lambda i: (i, 0)),
                  pl.BlockSpec((tm, d), lambda i: (i, 0)),
                  pl.BlockSpec((None, r, d), lambda i: ((i * tm) // rows_per_mod, 0, 0)),
                  pl.BlockSpec((1, d), lambda i: (0, 0)),
                  pl.BlockSpec((d, d), lambda i: (0, 0))],
        out_specs=pl.BlockSpec((tm, d), lambda i: (i, 0)),
        out_shape=jax.ShapeDtypeStruct((t, d), F32),
        compiler_params=_cparams("arbitrary"),
        name="out",
    )(mg, x2d, gate3, g_post.reshape(1, d), w_o)


def _window_rows(qkv, g, b, s, w):
    nh = A_GROUPS * A_HEADS
    qkv4 = qkv.reshape(N_SLABS, b, s, A_HEAD_DIM)
    parts = []
    for kind in (1, 2):
        s0 = kind * nh + g * A_HEADS
        hm = lax.slice(qkv4, (s0, 0, s - w, 0), (s0 + A_HEADS, b, s, A_HEAD_DIM))
        parts.append(jnp.transpose(hm, (1, 2, 0, 3)))
    return jnp.stack(parts, axis=2)


def _layer(depth_i, x_p, x_s, caches, conv_s, ssm_s, c_p, c_s, w_ada, b_ada, g_pre, g_post, w_in, conv_w,
           conv_b, dt_bias, a_log, d_skip, g_ssm, w_o_a, w_o_b, w_o):
    bp, sp, d = x_p.shape
    bs, ts, _ = x_s.shape
    assert sp % SSM_CHUNK == 0

    seg = lambda off, n: w_in[:, off:off + n]
    w_main = jnp.concatenate([seg(0, N_QKV_COLS), seg(W_XBC, SSM_CONV_DIM), seg(W_ZB, SSM_D_INNER),
                              seg(W_RA, D_MODEL), seg(W_RB, D_MODEL), seg(W_ZA, A_WIDTH)], axis=1).astype(BF16)
    w_dt = jnp.pad(seg(W_DT, SSM_HEADS), ((0, 0), (0, DT_PAD - SSM_HEADS))).astype(BF16)
    woa_b, wob_b, wo_b = w_o_a.astype(BF16), w_o_b.astype(BF16), w_o.astype(BF16)

    nmod = bp + bs
    mpad = -(-nmod // SUBLANES) * SUBLANES
    c_all = jnp.pad(jnp.concatenate([c_p, c_s], axis=0), ((0, mpad - nmod), (0, 0)))
    mod = _ada(c_all, w_ada, b_ada)
    shift, scale, gate = mod[:, :d], mod[:, d:2 * d], mod[:, 2 * d:]
    per_seq = lambda v: v[:bp].reshape(bp, 1, d)
    per_row = lambda v: jnp.repeat(v[bp:nmod], ts, axis=0).reshape(1, bs * ts, d)

    tp = bp * sp
    xp2 = x_p.reshape(tp, d)
    qkv_p, rest_p, dt_p = _in_proj(xp2, per_seq(scale), per_seq(shift), g_pre, w_main, w_dt, 1024, sp)
    oa_p = _attn_prompt(qkv_p, rest_p, bp, sp)
    conv0 = jnp.zeros((bp, SSM_CONV_W - 1, SSM_CONV_DIM), F32)
    ssm0 = jnp.zeros((bp, SSM_HEADS, SSM_HEAD_DIM, SSM_D_STATE), F32)
    yn_p, conv_p, ssm_p = _ssd(rest_p.reshape(bp, sp, N_REST), dt_p.reshape(bp, sp, DT_PAD), conv0, ssm0,
                               conv_w, conv_b, dt_bias, a_log, d_skip, g_ssm, SSM_CHUNK)
    mg_p = _mix(oa_p, yn_p, rest_p, woa_b, wob_b, 256)
    y_p = _out(mg_p, xp2, per_seq(gate), g_post, wo_b, 512, sp).reshape(bp, sp, d)
    kv_p = [_window_rows(qkv_p, g, bp, sp, min(A_WINDOWS[g], sp)) for g in range(A_GROUPS)]

    tsn = bs * ts
    xs2 = x_s.reshape(tsn, d)
    qkv_s, rest_s, dt_s = _in_proj(xs2, per_row(scale), per_row(shift), g_pre, w_main, w_dt, tsn, tsn)
    qkvn = jnp.transpose(qkv_s.reshape(3, A_GROUPS, A_HEADS, bs, ts, A_HEAD_DIM), (0, 1, 3, 4, 2, 5))
    za5 = rest_s[:, ZA_OFF:ZA_OFF + A_WIDTH].reshape(bs, ts, A_HEADS, A_HEAD_DIM)
    oa5, kv_s = _attn_decode(qkvn, za5, caches, depth_i, 512)
    oa_s = oa5.reshape(tsn, A_WIDTH).astype(BF16)
    yn_s, conv_sn, ssm_sn = _ssd(rest_s.reshape(bs, ts, N_REST), dt_s.reshape(bs, ts, DT_PAD), conv_s, ssm_s,
                                 conv_w, conv_b, dt_bias, a_log, d_skip, g_ssm, ts)
    mg_s = _mix(oa_s, yn_s, rest_s, woa_b, wob_b, tsn)
    y_s = _out(mg_s, xs2, per_row(gate), g_post, wo_b, tsn, tsn).reshape(bs, ts, d)
    return y_p, y_s, kv_p, conv_p, ssm_p, list(kv_s), conv_sn, ssm_sn


def kernel(x_prompt, x_sample, cache_a_w128, cache_a_w512, cache_a_w2048, state_conv, state_ssm, c_prompt, c_sample, w_ada, b_ada, g_pre, g_post, w_in, conv_w, conv_b, dt_bias, a_log, d_skip, g_ssm, w_o_a, w_o_b, w_o):
    depth = w_in.shape[0]
    caches = (cache_a_w128, cache_a_w512, cache_a_w2048)
    y_p, y_s = x_prompt, x_sample
    acc = [[] for _ in range(10)]
    for i in range(depth):
        y_p, y_s, kv_p, conv_p, ssm_p, kv_s, conv_s, ssm_s = _layer(
            i, y_p, y_s, caches, state_conv[i], state_ssm[i],
            c_prompt, c_sample, w_ada[i], b_ada[i], g_pre[i], g_post[i], w_in[i], conv_w[i], conv_b[i],
            dt_bias[i], a_log[i], d_skip[i], g_ssm[i], w_o_a[i], w_o_b[i], w_o[i])
        for lst, v in zip(acc, (*kv_p, conv_p, ssm_p, *kv_s, conv_s, ssm_s)):
            lst.append(v)
    return (y_p, y_s, *[jnp.stack(v) for v in acc])
```

```python
import functools

import numpy as np
import jax
import jax.numpy as jnp
from jax import lax
from jax.experimental import pallas as pl
from jax.experimental.pallas import tpu as pltpu

F32 = jnp.float32
BF16 = jnp.bfloat16

D_MODEL = 2048
A_WINDOWS = (128, 512, 2048)
A_DILATIONS = (1, 4, 16)
A_GROUPS = 3
A_HEADS = 8
A_HEAD_DIM = 128
A_WIDTH = A_HEADS * A_HEAD_DIM
A_NK = 128
ATT_SCALE = A_HEAD_DIM ** -0.5
SSM_D_INNER = D_MODEL
SSM_HEAD_DIM = 64
SSM_HEADS = SSM_D_INNER // SSM_HEAD_DIM
SSM_GROUPS = 8
SSM_D_STATE = 128
SSM_CONV_W = 4
SSM_CHUNK = 128
SSM_CONV_DIM = SSM_D_INNER + 2 * SSM_GROUPS * SSM_D_STATE
QKV_WIDTH = A_GROUPS * A_WIDTH
NORM_EPS = 1e-6

LANES = 128
SUBLANES = 8
VMEM_LIMIT = 56 * 1024 * 1024
NEG = -1e30

COL = 1024
N_QKV_COLS = 3 * QKV_WIDTH
N_SLABS = N_QKV_COLS // LANES
XBC_OFF = 0
ZB_OFF = XBC_OFF + SSM_CONV_DIM
RA_OFF = ZB_OFF + SSM_D_INNER
RB_OFF = RA_OFF + D_MODEL
ZA_OFF = RB_OFF + D_MODEL
N_REST = ZA_OFF + A_WIDTH
DT_PAD = LANES
W_ZA = N_QKV_COLS
W_ZB = W_ZA + A_WIDTH
W_XBC = W_ZB + SSM_D_INNER
W_DT = W_XBC + SSM_CONV_DIM
W_RA = W_DT + SSM_HEADS
W_RB = W_RA + D_MODEL


def _cparams(*sem):
    return pltpu.CompilerParams(dimension_semantics=sem, vmem_limit_bytes=VMEM_LIMIT)


def _dot(a, b):
    return jnp.dot(a, b, preferred_element_type=F32)


def _dot_nt(a, b):
    return lax.dot_general(a, b, (((1,), (1,)), ((), ())), preferred_element_type=F32)


def _split3(x):
    x1 = x.astype(BF16)
    r1 = x - x1.astype(F32)
    x2 = r1.astype(BF16)
    x3 = (r1 - x2.astype(F32)).astype(BF16)
    return x1, x2, x3


def _dot_f32_by_01(x, e):
    x1, x2, x3 = _split3(x)
    return _dot(x1, e) + _dot(x2, e) + _dot(x3, e)


def _dot_01_by_f32(e, x):
    x1, x2, x3 = _split3(x)
    return _dot(e, x1) + _dot(e, x2) + _dot(e, x3)


def _silu(x):
    h = 0.5 * x
    return h + h * jnp.tanh(h)


def _ada_kernel(c_ref, w_ref, b_ref, o_ref):
    s = _silu(c_ref[...]).astype(BF16)
    o_ref[...] = _dot(s, w_ref[...].astype(BF16)) + b_ref[...]


def _ada(c, w_ada, b_ada):
    m, d = c.shape
    n = w_ada.shape[1]
    tn = 512
    return pl.pallas_call(
        _ada_kernel,
        grid=(n // tn,),
        in_specs=[pl.BlockSpec((m, d), lambda j: (0, 0)),
                  pl.BlockSpec((d, tn), lambda j: (0, j)),
                  pl.BlockSpec((1, tn), lambda j: (0, j))],
        out_specs=pl.BlockSpec((m, tn), lambda j: (0, j)),
        out_shape=jax.ShapeDtypeStruct((m, n), F32),
        compiler_params=_cparams("arbitrary"),
        name="ada",
    )(c, w_ada, b_ada.reshape(1, n))


def _in_proj_kernel(x_ref, sc_ref, sh_ref, g_ref, w_ref, wdt_ref, qkv_ref, rest_ref, dt_ref, h_scr,
                    *, tm, rc, n_qkv_steps):
    j = pl.program_id(1)

    @pl.when(j == 0)
    def _():
        per_row = sc_ref.shape[0] != 1

        def body(r, carry):
            rows = pl.ds(pl.multiple_of(r * rc, rc), rc)
            x = x_ref[rows, :]
            y = x * lax.rsqrt(jnp.mean(x * x, axis=-1, keepdims=True) + NORM_EPS) * g_ref[...]
            sc = sc_ref[rows, :] if per_row else sc_ref[...]
            sh = sh_ref[rows, :] if per_row else sh_ref[...]
            h_scr[rows, :] = (y * (1.0 + sc) + sh).astype(BF16)
            return carry

        lax.fori_loop(0, tm // rc, body, 0)
        dt_ref[...] = _dot(h_scr[...], wdt_ref[...])

    res = _dot(h_scr[...], w_ref[...])

    @pl.when(j < n_qkv_steps)
    def _():
        for c in range(COL // LANES):
            qkv_ref[c] = res[:, c * LANES:(c + 1) * LANES]

    @pl.when(j >= n_qkv_steps)
    def _():
        rest_ref[...] = res


def _in_proj(x2d, sc3, sh3, g_pre, w_main, w_dt, tm, rows_per_mod):
    t, d = x2d.shape
    r = sc3.shape[1]
    tn = COL
    rc = min(tm, 256)
    nq = N_QKV_COLS // tn
    spb = tn // LANES
    mod_map = lambda i, j: ((i * tm) // rows_per_mod, 0, 0)
    return pl.pallas_call(
        functools.partial(_in_proj_kernel, tm=tm, rc=rc, n_qkv_steps=nq),
        grid=(t // tm, (N_QKV_COLS + N_REST) // tn),
        in_specs=[pl.BlockSpec((tm, d), lambda i, j: (i, 0)),
                  pl.BlockSpec((None, r, d), mod_map),
                  pl.BlockSpec((None, r, d), mod_map),
                  pl.BlockSpec((1, d), lambda i, j: (0, 0)),
                  pl.BlockSpec((d, tn), lambda i, j: (0, j)),
                  pl.BlockSpec((d, DT_PAD), lambda i, j: (0, 0))],
        out_specs=[pl.BlockSpec((spb, tm, LANES), lambda i, j: (jnp.minimum(j, nq - 1), i, 0)),
                   pl.BlockSpec((tm, tn), lambda i, j: (i, jnp.maximum(j - nq, 0))),
                   pl.BlockSpec((tm, DT_PAD), lambda i, j: (i, 0))],
        out_shape=[jax.ShapeDtypeStruct((N_SLABS, t, LANES), F32),
                   jax.ShapeDtypeStruct((t, N_REST), F32),
                   jax.ShapeDtypeStruct((t, DT_PAD), F32)],
        scratch_shapes=[pltpu.VMEM((tm, d), BF16)],
        compiler_params=_cparams("arbitrary", "arbitrary"),
        name="in_proj",
    )(x2d, sc3, sh3, g_pre.reshape(1, d), w_main, w_dt)


def _attn_prompt_kernel(*refs, tb):
    ng = A_GROUPS
    q_refs, kc_refs, kp_refs = refs[0:ng], refs[ng:2 * ng], refs[2 * ng:3 * ng]
    vc_refs, vp_refs = refs[3 * ng:4 * ng], refs[4 * ng:5 * ng]
    za_ref, oa_ref, o_scr, lse_scr = refs[5 * ng:]
    i = pl.program_id(2)
    nk = A_NK
    row = lax.broadcasted_iota(jnp.int32, (nk, 2 * nk), 0)
    col = lax.broadcasted_iota(jnp.int32, (nk, 2 * nk), 1)
    ok = jnp.logical_and(col >= row, col <= row + nk)
    ok_first = jnp.logical_and(ok, col >= jnp.where(i > 0, 0, nk))

    def load(g, rows, k_prev_ref, v_prev_ref, prev_rows):
        qh = q_refs[g][rows, :].astype(BF16)
        k2 = jnp.concatenate([k_prev_ref[prev_rows, :], kc_refs[g][rows, :]], axis=0).astype(BF16)
        v2 = jnp.concatenate([v_prev_ref[prev_rows, :], vc_refs[g][rows, :]], axis=0).astype(BF16)
        return qh, k2, v2

    def attend(qh, k2, v2, mask):
        s = jnp.where(mask, _dot_nt(qh, k2) * ATT_SCALE, NEG)
        m = jnp.max(s, axis=-1, keepdims=True)
        e = jnp.exp(s - m)
        z = jnp.sum(e, axis=-1, keepdims=True)
        return _dot(e.astype(BF16), v2) / z, m + jnp.log(z)

    unroll = 8

    def rows_of(d, sb, r):
        start = sb * (nk * d) + r
        if d == 1:
            return pl.ds(pl.multiple_of(start, nk), nk)
        return pl.ds(start, nk, stride=d)

    def run(g, items):
        d = A_DILATIONS[g]
        loaded, masks = [], []
        for sb, r in items:
            if isinstance(sb, int) and sb == 0:
                loaded.append(load(g, rows_of(d, 0, r), kp_refs[g], vp_refs[g], rows_of(d, 0, r)))
                masks.append(ok_first)
            else:
                loaded.append(load(g, rows_of(d, sb, r), kc_refs[g], vc_refs[g], rows_of(d, sb - 1, r)))
                masks.append(ok)
        results = [attend(*x, mask) for x, mask in zip(loaded, masks)]
        for (sb, r), (o, lse) in zip(items, results):
            o_scr[g, rows_of(d, sb, r), :] = o
            lse_scr[g, rows_of(d, sb, r), :] = jnp.broadcast_to(lse, (nk, LANES))

    for g in range(ng):
        d = A_DILATIONS[g]
        nblk = tb // nk
        assert nblk % unroll == 0 and (d % unroll == 0 or unroll % d == 0)
        if d >= unroll:
            per_sb = d // unroll
            run_first = lambda it, c, g=g: (run(g, [(0, it * unroll + u) for u in range(unroll)]), c)[1]
            lax.fori_loop(0, per_sb, run_first, 0)
            if nblk > d:
                def later(it, c, g=g, per_sb=per_sb):
                    sb = 1 + it // per_sb
                    r0 = (it % per_sb) * unroll
                    run(g, [(sb, r0 + u) for u in range(unroll)])
                    return c
                lax.fori_loop(0, (nblk - d) // unroll, later, 0)
        else:
            sbs = unroll // d
            run(g, [(sb, r) for sb in range(sbs) for r in range(d)])
            def later(it, c, g=g, d=d, sbs=sbs):
                run(g, [(it * sbs + s, r) for s in range(sbs) for r in range(d)])
                return c
            lax.fori_loop(1, nblk // unroll, later, 0)

    mc = 256

    def merge(c, carry):
        rows = pl.ds(pl.multiple_of(c * mc, mc), mc)
        l0, l1, l2 = lse_scr[0, rows, :], lse_scr[1, rows, :], lse_scr[2, rows, :]
        m = jnp.maximum(jnp.maximum(l0, l1), l2)
        e0, e1, e2 = jnp.exp(l0 - m), jnp.exp(l1 - m), jnp.exp(l2 - m)
        inv = 1.0 / (e0 + e1 + e2)
        oa = (e0 * inv) * o_scr[0, rows, :] + (e1 * inv) * o_scr[1, rows, :] + (e2 * inv) * o_scr[2, rows, :]
        oa_ref[rows, :] = (oa * _silu(za_ref[rows, :])).astype(BF16)
        return carry

    lax.fori_loop(0, tb // mc, merge, 0)


def _attn_prompt(qkv, rest, b, s):
    t = b * s
    nk = A_NK
    tb = nk * A_DILATIONS[-1]
    assert s % tb == 0
    nb = s // tb
    nh = A_GROUPS * A_HEADS

    def cur(kind, g):
        return pl.BlockSpec((None, tb, LANES), lambda bi, h, i: (kind * nh + g * A_HEADS + h, bi * nb + i, 0))

    def prev(kind, g):
        pb = nk * A_DILATIONS[g]
        return pl.BlockSpec((None, pb, LANES),
                            lambda bi, h, i: (kind * nh + g * A_HEADS + h,
                                              jnp.maximum((bi * s + i * tb) // pb - 1, 0), 0))

    groups = range(A_GROUPS)
    in_specs = ([cur(0, g) for g in groups] + [cur(1, g) for g in groups] + [prev(1, g) for g in groups]
                + [cur(2, g) for g in groups] + [prev(2, g) for g in groups]
                + [pl.BlockSpec((tb, LANES), lambda bi, h, i: (bi * nb + i, ZA_OFF // LANES + h))])
    return pl.pallas_call(
        functools.partial(_attn_prompt_kernel, tb=tb),
        grid=(b, A_HEADS, nb),
        in_specs=in_specs,
        out_specs=pl.BlockSpec((tb, LANES), lambda bi, h, i: (bi * nb + i, h)),
        out_shape=jax.ShapeDtypeStruct((t, A_WIDTH), BF16),
        scratch_shapes=[pltpu.VMEM((A_GROUPS, tb, LANES), F32), pltpu.VMEM((A_GROUPS, tb, LANES), F32)],
        compiler_params=_cparams("arbitrary", "arbitrary", "arbitrary"),
        name="attn_p",
    )(*([qkv] * (5 * A_GROUPS)), rest)


def _attn_decode_kernel(qn_ref, kn_ref, vn_ref, za_ref, c0_ref, c1_ref, c2_ref, nx_ref,
                        oa_ref, kv0_ref, kv1_ref, kv2_ref, m_scr, l_scr, acc_scr, o_scr, lse_scr, *, tr, t, nt):
    j = pl.program_id(1)
    last = j == pl.num_programs(1) - 1
    tile = (A_HEADS, A_HEAD_DIM)

    def shift(c_ref, kv_ref, n):
        kv_ref[0:n - t] = c_ref[t:n]

    def put_new(kv_ref, g, n):
        kv_ref[n - t:n, 0] = kn_ref[g]
        kv_ref[n - t:n, 1] = vn_ref[g]

    def part_cache(c_ref, g, tq, n_rows, row0):
        d = A_DILATIONS[g]
        start = tq % d
        n = n_rows // d
        k = c_ref[pl.ds(start, n, stride=d), 0]
        v = c_ref[pl.ds(start, n, stride=d), 1]
        q = qn_ref[g, tq]
        s = jnp.broadcast_to(jnp.sum(k * q[None], axis=-1, keepdims=True), (n,) + tile) * ATT_SCALE
        ri = row0 + start + d * lax.broadcasted_iota(jnp.int32, (n,) + tile, 0)
        s = jnp.where(ri >= tq, s, NEG)
        m = jnp.max(s, axis=0)
        p = jnp.exp(s - m[None])
        return m, jnp.sum(p, axis=0), jnp.sum(p * v, axis=0)

    def part_new(g, tq):
        d = A_DILATIONS[g]
        q = qn_ref[g, tq]
        js = [tk for tk in range(tq + 1) if (tq - tk) % d == 0]
        ss = [jnp.broadcast_to(jnp.sum(kn_ref[g, tk] * q, axis=-1, keepdims=True), tile) * ATT_SCALE
              for tk in js]
        m = functools.reduce(jnp.maximum, ss)
        ps = [jnp.exp(s - m) for s in ss]
        return m, sum(ps), sum(p * vn_ref[g, tk] for p, tk in zip(ps, js))

    def combine(a, b):
        m = jnp.maximum(a[0], b[0])
        fa, fb = jnp.exp(a[0] - m), jnp.exp(b[0] - m)
        return m, a[1] * fa + b[1] * fb, a[2] * fa + b[2] * fb

    small = ((c0_ref, kv0_ref), (c1_ref, kv1_ref))
    units = [(g, tq) for g in range(len(small)) for tq in range(t)]
    per_step = -(-len(units) // nt)
    for step in range(nt):
        mine = units[step * per_step:(step + 1) * per_step]
        if not mine:
            continue

        @pl.when(j == step)
        def _(mine=mine):
            for g, tq in mine:
                c_ref, kv_ref = small[g]
                n = A_WINDOWS[g]
                if (g, tq) == min(u for u in units if u[0] == g):
                    shift(c_ref, kv_ref, n)
                    put_new(kv_ref, g, n)
                m, l, acc = combine(part_cache(c_ref, g, tq, n, 0), part_new(g, tq))
                o_scr[g, tq] = acc / l
                lse_scr[g, tq] = m + jnp.log(l)

    @pl.when(j == 0)
    def _():
        m_scr[...] = jnp.full(m_scr.shape, NEG, F32)
        l_scr[...] = jnp.zeros(l_scr.shape, F32)
        acc_scr[...] = jnp.zeros(acc_scr.shape, F32)

    g2 = A_GROUPS - 1
    shift(c2_ref, kv2_ref, tr)

    @pl.when(last)
    def _():
        put_new(kv2_ref, g2, tr)

    @pl.when(jnp.logical_not(last))
    def _():
        kv2_ref[tr - t:tr] = nx_ref[...]

    for tq in range(t):
        m, l, acc = combine((m_scr[tq], l_scr[tq], acc_scr[tq]), part_cache(c2_ref, g2, tq, tr, j * tr))
        m_scr[tq] = m
        l_scr[tq] = l
        acc_scr[tq] = acc

    @pl.when(last)
    def _():
        for tq in range(t):
            m, l, acc = combine((m_scr[tq], l_scr[tq], acc_scr[tq]), part_new(g2, tq))
            o2 = acc / l
            l2 = m + jnp.log(l)
            l0, l1 = lse_scr[0, tq], lse_scr[1, tq]
            mm = jnp.maximum(jnp.maximum(l0, l1), l2)
            e0, e1, e2 = jnp.exp(l0 - mm), jnp.exp(l1 - mm), jnp.exp(l2 - mm)
            inv = 1.0 / (e0 + e1 + e2)
            oa = (e0 * inv) * o_scr[0, tq] + (e1 * inv) * o_scr[1, tq] + (e2 * inv) * o2
            oa_ref[tq] = oa * _silu(za_ref[tq])


def _attn_decode(qkvn, za5, caches, depth_i, tr):
    _, ng, b, t, nh, hd = qkvn.shape
    wbs = [c.shape[2] for c in caches]
    assert t == SUBLANES and tuple(wbs) == A_WINDOWS and wbs[-1] % tr == 0 and tr % (A_DILATIONS[-1] * t) == 0
    new = lambda kind: pl.BlockSpec((None, ng, None, t, nh, hd), lambda bi, j: (kind, 0, bi, 0, 0, 0))
    whole = lambda n: pl.BlockSpec((None, None, n, 2, nh, hd), lambda bi, j: (depth_i, bi, 0, 0, 0, 0))
    whole_out = lambda n: pl.BlockSpec((None, n, 2, nh, hd), lambda bi, j: (bi, 0, 0, 0, 0))
    state = lambda: pltpu.VMEM((t, nh, hd), F32)
    res = pl.pallas_call(
        functools.partial(_attn_decode_kernel, tr=tr, t=t, nt=wbs[-1] // tr),
        grid=(b, wbs[-1] // tr),
        in_specs=[new(0), new(1), new(2),
                  pl.BlockSpec((None, t, nh, hd), lambda bi, j: (bi, 0, 0, 0)),
                  whole(wbs[0]), whole(wbs[1]),
                  pl.BlockSpec((None, None, tr, 2, nh, hd), lambda bi, j: (depth_i, bi, j, 0, 0, 0)),
                  pl.BlockSpec((None, None, t, 2, nh, hd),
                               lambda bi, j: (depth_i, bi, jnp.minimum((j + 1) * (tr // t), wbs[-1] // t - 1),
                                              0, 0, 0))],
        out_specs=[pl.BlockSpec((None, t, nh, hd), lambda bi, j: (bi, 0, 0, 0)),
                   whole_out(wbs[0]), whole_out(wbs[1]),
                   pl.BlockSpec((None, tr, 2, nh, hd), lambda bi, j: (bi, j, 0, 0, 0))],
        out_shape=[jax.ShapeDtypeStruct((b, t, nh, hd), F32)]
        + [jax.ShapeDtypeStruct((b, w, 2, nh, hd), F32) for w in wbs],
        scratch_shapes=[state(), state(), state(),
                        pltpu.VMEM((ng - 1, t, nh, hd), F32), pltpu.VMEM((ng - 1, t, nh, hd), F32)],
        compiler_params=_cparams("arbitrary", "arbitrary"),
        name="attn_d",
    )(qkvn, qkvn, qkvn, za5, caches[0], caches[1], caches[2], caches[2])
    return res[0], res[1:]


def _ssd_kernel(xbc_ref, dt_ref, zb_ref, conv0_ref, h0_ref, convw_ref, convb_ref, dtb_ref, alog_ref,
                dskip_ref, gssm_ref, e_ref, yn_ref, convo_ref, ho_ref, ext_scr, st_scr, *, nv):
    q = SSM_CHUNK
    c = pl.program_id(1)
    di = SSM_D_INNER
    gw = di // SSM_GROUPS
    ns = SSM_D_STATE
    hp = SUBLANES
    kw = SSM_CONV_W

    nslab = SSM_CONV_DIM // LANES
    lanes_of = lambda j: slice(j * LANES, (j + 1) * LANES)

    @pl.when(c == 0)
    def _():
        for j in range(nslab):
            ext_scr[j, 0:hp, :] = conv0_ref[:, lanes_of(j)]
        for j in range(di // LANES):
            st_scr[:, lanes_of(j)] = h0_ref[j].T

    for j in range(nslab):
        ext_scr[j, hp:hp + nv, :] = xbc_ref[:, lanes_of(j)]
        if nv != q:
            ext_scr[j, hp + nv:hp + q, :] = jnp.zeros((q - nv, LANES), F32)

    def conv_act(c0, c1):
        pieces = []
        for j in range(c0 // LANES, c1 // LANES):
            acc = convb_ref[:, lanes_of(j)]
            for k in range(kw):
                acc = acc + convw_ref[k:k + 1, lanes_of(j)] * ext_scr[j, pl.ds(hp - (kw - 1) + k, q), :]
            pieces.append(_silu(acc))
        return pieces[0] if len(pieces) == 1 else jnp.concatenate(pieces, axis=1)

    row = lax.broadcasted_iota(jnp.int32, (q, q), 0)
    col = lax.broadcasted_iota(jnp.int32, (q, q), 1)
    causal = row >= col
    if nv == q:
        dt_raw = dt_ref[...]
    else:
        dt_raw = jnp.concatenate([dt_ref[...], jnp.zeros((q - nv, DT_PAD), F32)], axis=0)
    dt = jax.nn.softplus(dt_raw + dtb_ref[...])
    if nv != q:
        dt = jnp.where(row < nv, dt, 0.0)
    da = dt * (-jnp.exp(alog_ref[...]))
    cs = _dot_01_by_f32(jnp.where(causal, 1.0, 0.0).astype(BF16), da)
    cs_t = cs.T
    dt_parts, cs_parts = _split3(dt), _split3(cs)
    half = lax.broadcasted_iota(jnp.int32, (q, LANES), 1) < SSM_HEAD_DIM

    for g in range(SSM_GROUPS):
        gs = slice(g * gw, (g + 1) * gw)
        e_g = e_ref[:, gs]
        dt_x = sum(_dot(p, e_g) for p in dt_parts)
        cs_x = sum(_dot(p, e_g) for p in cs_parts)
        cs_last = cs_x[q - 1:q, :]
        xs = conv_act(g * gw, (g + 1) * gw)
        xdt = xs * dt_x
        xdt_b = xdt.astype(BF16)
        w_b = (xdt * jnp.exp(cs_last - cs_x)).astype(BF16)
        bg_t = conv_act(di + g * ns, di + (g + 1) * ns).T.astype(BF16)
        cg = conv_act(di + SSM_GROUPS * ns + g * ns, di + SSM_GROUPS * ns + (g + 1) * ns).astype(BF16)
        cb = _dot(cg, bg_t)
        st = st_scr[:, gs]
        y_off = _dot(cg, st.astype(BF16)) * jnp.exp(cs_x)
        st_scr[:, gs] = st * jnp.exp(cs_last) + _dot(bg_t, w_b)
        y_pairs = []
        for pair in range(gw // LANES):
            h0 = (g * gw + pair * LANES) // SSM_HEAD_DIM
            ps = slice(pair * LANES, (pair + 1) * LANES)
            ys = []
            for h in (h0, h0 + 1):
                diff = cs[:, h:h + 1] - cs_t[h:h + 1, :]
                m_h = (cb * jnp.exp(jnp.where(causal, diff, NEG))).astype(BF16)
                ys.append(_dot(m_h, xdt_b[:, ps]))
            y_pairs.append(jnp.where(half, ys[0], ys[1]))
        y = jnp.concatenate(y_pairs, axis=1) + y_off + xs * dskip_ref[:, gs]
        v = y[0:nv, :] * _silu(zb_ref[:, gs])
        vn = v * lax.rsqrt(jnp.mean(v * v, axis=-1, keepdims=True) + NORM_EPS)
        yn_ref[:, gs] = (vn * gssm_ref[:, gs]).astype(BF16)

    @pl.when(c == pl.num_programs(1) - 1)
    def _():
        for j in range(di // LANES):
            ho_ref[j] = st_scr[:, j * LANES:(j + 1) * LANES].T

    for j in range(nslab):
        tail = ext_scr[j, hp + nv - (kw - 1):hp + nv, :]
        convo_ref[:, lanes_of(j)] = tail
        ext_scr[j, hp - (kw - 1):hp, :] = tail


def _ssd(rest3, dt3, conv0, h0, conv_w, conv_b, dt_bias, a_log, d_skip, g_ssm, nv):
    b, l, _ = rest3.shape
    q = SSM_CHUNK
    nc = l // nv
    assert nv == q or nc == 1
    di, cd, nh = SSM_D_INNER, SSM_CONV_DIM, SSM_HEADS
    hp = SUBLANES
    conv0p = jnp.pad(conv0, ((0, 0), (hp - (SSM_CONV_W - 1), 0), (0, 0)))
    h0v = h0.reshape(b, di // LANES, LANES, SSM_D_STATE)
    pad1 = lambda v: jnp.pad(v.reshape(1, nh), ((0, 0), (0, DT_PAD - nh)))
    expand = np.zeros((DT_PAD, di), np.float32)
    expand[np.arange(di) // SSM_HEAD_DIM, np.arange(di)] = 1.0
    const = lambda shape: pl.BlockSpec(shape, lambda bi, c: (0,) * len(shape))
    yn, conv_o, h_o = pl.pallas_call(
        functools.partial(_ssd_kernel, nv=nv),
        grid=(b, nc),
        in_specs=[pl.BlockSpec((None, nv, cd), lambda bi, c: (bi, c, XBC_OFF // cd)),
                  pl.BlockSpec((None, nv, DT_PAD), lambda bi, c: (bi, c, 0)),
                  pl.BlockSpec((None, nv, di), lambda bi, c: (bi, c, ZB_OFF // di)),
                  pl.BlockSpec((None, hp, cd), lambda bi, c: (bi, 0, 0)),
                  pl.BlockSpec((None, di // LANES, LANES, SSM_D_STATE), lambda bi, c: (bi, 0, 0, 0)),
                  const((SSM_CONV_W, cd)), const((1, cd)), const((1, DT_PAD)), const((1, DT_PAD)),
                  const((1, di)), const((1, di)), const((DT_PAD, di))],
        out_specs=[pl.BlockSpec((None, nv, di), lambda bi, c: (bi, c, 0)),
                   pl.BlockSpec((None, SSM_CONV_W - 1, cd), lambda bi, c: (bi, 0, 0)),
                   pl.BlockSpec((None, di // LANES, LANES, SSM_D_STATE), lambda bi, c: (bi, 0, 0, 0))],
        out_shape=[jax.ShapeDtypeStruct((b, l, di), BF16),
                   jax.ShapeDtypeStruct((b, SSM_CONV_W - 1, cd), F32),
                   jax.ShapeDtypeStruct((b, di // LANES, LANES, SSM_D_STATE), F32)],
        scratch_shapes=[pltpu.VMEM((cd // LANES, hp + q, LANES), F32),
                        pltpu.VMEM((SSM_D_STATE, di), F32)],
        compiler_params=_cparams("arbitrary", "arbitrary"),
        name=f"ssd{nv}",
    )(rest3, dt3, rest3, conv0p, h0v, conv_w, conv_b.reshape(1, cd), pad1(dt_bias), pad1(a_log),
      jnp.repeat(d_skip, SSM_HEAD_DIM).reshape(1, di), g_ssm.reshape(1, di), jnp.asarray(expand, BF16))
    return yn.reshape(b * l, di), conv_o, h_o.reshape(b, nh, SSM_HEAD_DIM, SSM_D_STATE)


def _mix_kernel(oa_ref, yn_ref, ra_ref, rb_ref, woa_ref, wob_ref, mg_ref):
    p_a = _dot(oa_ref[...], woa_ref[...])
    p_b = _dot(yn_ref[...], wob_ref[...])
    mg_ref[...] = (jax.nn.sigmoid(ra_ref[...]) * p_a + jax.nn.sigmoid(rb_ref[...]) * p_b).astype(BF16)


def _mix(oa, yn, rest, w_o_a, w_o_b, tm):
    t = rest.shape[0]
    d = D_MODEL
    row = lambda w, off: pl.BlockSpec((tm, w), lambda i: (i, off // w))
    const = lambda shape: pl.BlockSpec(shape, lambda i: (0, 0))
    return pl.pallas_call(
        _mix_kernel,
        grid=(t // tm,),
        in_specs=[row(A_WIDTH, 0), row(d, 0), row(d, RA_OFF), row(d, RB_OFF),
                  const((A_WIDTH, d)), const((d, d))],
        out_specs=row(d, 0),
        out_shape=jax.ShapeDtypeStruct((t, d), BF16),
        compiler_params=_cparams("arbitrary"),
        name="mix",
    )(oa, yn, rest, rest, w_o_a, w_o_b)


def _out_kernel(mg_ref, x_ref, gate_ref, gpost_ref, wo_ref, y_ref):
    out = _dot(mg_ref[...], wo_ref[...])
    nrm = out * lax.rsqrt(jnp.mean(out * out, axis=-1, keepdims=True) + NORM_EPS) * gpost_ref[...]
    y_ref[...] = x_ref[...] + gate_ref[...] * nrm


def _out(mg, x2d, gate3, g_post, w_o, tm, rows_per_mod):
    t, d = x2d.shape
    r = gate3.shape[1]
    return pl.pallas_call(
        _out_kernel,
        grid=(t // tm,),
        in_specs=[pl.BlockSpec((tm, d), lambda i: (i, 0)),
                  pl.BlockSpec((tm, d), lambda i: (i, 0)),
                  pl.BlockSpec((None, r, d), lambda i: ((i * tm) // rows_per_mod, 0, 0)),
                  pl.BlockSpec((1, d), lambda i: (0, 0)),
                  pl.BlockSpec((d, d), lambda i: (0, 0))],
        out_specs=pl.BlockSpec((tm, d), lambda i: (i, 0)),
        out_shape=jax.ShapeDtypeStruct((t, d), F32),
        compiler_params=_cparams("arbitrary"),
        name="out",
    )(mg, x2d, gate3, g_post.reshape(1, d), w_o)


def _window_kernel(k_ref, v_ref, o_ref, *, rows):
    per_row = 2 * A_HEADS
    for kind, src in enumerate((k_ref, v_ref)):
        for h in range(A_HEADS):
            o_ref[pl.ds(kind * A_HEADS + h, rows, stride=per_row), :] = src[h]


def _window_rows(qkv, g, b, s, w):
    nh = A_GROUPS * A_HEADS
    rows = min(w, 256)
    per_row = 2 * A_HEADS
    assert w % rows == 0 and (s - w) % rows == 0
    src = lambda kind: pl.BlockSpec(
        (A_HEADS, rows, A_HEAD_DIM),
        lambda bi, i: (kind * A_GROUPS + g, (bi * s + s - w) // rows + i, 0))
    out = pl.pallas_call(
        functools.partial(_window_kernel, rows=rows),
        grid=(b, w // rows),
        in_specs=[src(1), src(2)],
        out_specs=pl.BlockSpec((None, rows * per_row, A_HEAD_DIM), lambda bi, i: (bi, i, 0)),
        out_shape=jax.ShapeDtypeStruct((b, w * per_row, A_HEAD_DIM), F32),
        compiler_params=_cparams("arbitrary", "arbitrary"),
        name=f"window{g}",
    )(qkv, qkv)
    return out.reshape(b, w, 2, A_HEADS, A_HEAD_DIM)


def _prep_w_kernel(tbl_ref, a_ref, b_ref, dt_ref, w_ref, wdt_ref):
    j = pl.program_id(0)
    d = a_ref.shape[1]
    late = tbl_ref[1, j]
    t = LANES

    def put(r, tile_rows):
        for c in range(d // t):
            w_ref[c * t:(c + 1) * t, r * t:(r + 1) * t] = tile_rows[:, c * t:(c + 1) * t].T.astype(BF16)

    @pl.when(late == 0)
    def _():
        for r in range(COL // t):
            put(r, a_ref[r * t:(r + 1) * t, :])

    @pl.when(late != 0)
    def _():
        for r in range(COL // t - 1):
            put(r, a_ref[r * t + SSM_HEADS:(r + 1) * t + SSM_HEADS, :])
        put(COL // t - 1, jnp.concatenate([a_ref[COL - t + SSM_HEADS:COL, :], b_ref[...]], axis=0))

    @pl.when(j == 0)
    def _():
        lane = lax.broadcasted_iota(jnp.int32, (t, t), 1)
        for c in range(d // t):
            blk = dt_ref[:, c * t:(c + 1) * t].T
            wdt_ref[c * t:(c + 1) * t, :] = jnp.where(lane < SSM_HEADS, blk, 0.0).astype(BF16)


def _prep_w(w_t):
    d = w_t.shape[1]
    assert W_DT % COL == 0 and W_RB - W_RA == D_MODEL and DT_PAD == LANES and SSM_HEADS % SUBLANES == 0
    blk = lambda off, n: [off // COL + k for k in range(n // COL)]
    plain = blk(0, N_QKV_COLS) + blk(W_XBC, SSM_CONV_DIM) + blk(W_ZB, SSM_D_INNER)
    late = blk(W_DT, 2 * D_MODEL)
    za = blk(W_ZA, A_WIDTH)
    src = plain + late + za
    is_late = [0] * len(plain) + [1] * len(late) + [0] * len(za)
    nxt = [0] * len(plain) + [(k + 1) * COL // SSM_HEADS for k in late] + [0] * len(za)
    tbl = jnp.asarray(np.array([src, is_late, nxt], np.int32))
    n = len(src)
    grid_spec = pltpu.PrefetchScalarGridSpec(
        num_scalar_prefetch=1,
        grid=(n,),
        in_specs=[pl.BlockSpec((COL, d), lambda j, t: (t[0, j], 0)),
                  pl.BlockSpec((SSM_HEADS, d), lambda j, t: (t[2, j], 0)),
                  pl.BlockSpec((DT_PAD, d), lambda j, t: (W_DT // DT_PAD, 0))],
        out_specs=[pl.BlockSpec((d, COL), lambda j, t: (0, j)),
                   pl.BlockSpec((d, DT_PAD), lambda j, t: (0, 0))])
    return pl.pallas_call(
        _prep_w_kernel,
        grid_spec=grid_spec,
        out_shape=[jax.ShapeDtypeStruct((d, n * COL), BF16), jax.ShapeDtypeStruct((d, DT_PAD), BF16)],
        compiler_params=_cparams("arbitrary"),
        name="prep_w",
    )(tbl, w_t, w_t, w_t)


def _layer(depth_i, x_p, x_s, caches, conv_s, ssm_s, c_p, c_s, w_ada, b_ada, g_pre, g_post, w_in, conv_w,
           conv_b, dt_bias, a_log, d_skip, g_ssm, w_o_a, w_o_b, w_o):
    bp, sp, d = x_p.shape
    bs, ts, _ = x_s.shape
    assert sp % SSM_CHUNK == 0

    w_main, w_dt = _prep_w(jnp.swapaxes(w_in, 0, 1))
    woa_b, wob_b, wo_b = w_o_a.astype(BF16), w_o_b.astype(BF16), w_o.astype(BF16)

    nmod = bp + bs
    mpad = -(-nmod // SUBLANES) * SUBLANES
    c_all = jnp.pad(jnp.concatenate([c_p, c_s], axis=0), ((0, mpad - nmod), (0, 0)))
    mod = _ada(c_all, w_ada, b_ada)
    shift, scale, gate = mod[:, :d], mod[:, d:2 * d], mod[:, 2 * d:]
    per_seq = lambda v: v[:bp].reshape(bp, 1, d)
    per_row = lambda v: jnp.repeat(v[bp:nmod], ts, axis=0).reshape(1, bs * ts, d)

    tp = bp * sp
    xp2 = x_p.reshape(tp, d)
    qkv_p, rest_p, dt_p = _in_proj(xp2, per_seq(scale), per_seq(shift), g_pre, w_main, w_dt, 1024, sp)
    oa_p = _attn_prompt(qkv_p, rest_p, bp, sp)
    conv0 = jnp.zeros((bp, SSM_CONV_W - 1, SSM_CONV_DIM), F32)
    ssm0 = jnp.zeros((bp, SSM_HEADS, SSM_HEAD_DIM, SSM_D_STATE), F32)
    yn_p, conv_p, ssm_p = _ssd(rest_p.reshape(bp, sp, N_REST), dt_p.reshape(bp, sp, DT_PAD), conv0, ssm0,
                               conv_w, conv_b, dt_bias, a_log, d_skip, g_ssm, SSM_CHUNK)
    mg_p = _mix(oa_p, yn_p, rest_p, woa_b, wob_b, 256)
    y_p = _out(mg_p, xp2, per_seq(gate), g_post, wo_b, 512, sp).reshape(bp, sp, d)
    kv_p = [_window_rows(qkv_p, g, bp, sp, min(A_WINDOWS[g], sp)) for g in range(A_GROUPS)]

    tsn = bs * ts
    xs2 = x_s.reshape(tsn, d)
    qkv_s, rest_s, dt_s = _in_proj(xs2, per_row(scale), per_row(shift), g_pre, w_main, w_dt, tsn, tsn)
    qkvn = jnp.transpose(qkv_s.reshape(3, A_GROUPS, A_HEADS, bs, ts, A_HEAD_DIM), (0, 1, 3, 4, 2, 5))
    za5 = rest_s[:, ZA_OFF:ZA_OFF + A_WIDTH].reshape(bs, ts, A_HEADS, A_HEAD_DIM)
    oa5, kv_s = _attn_decode(qkvn, za5, caches, depth_i, 512)
    oa_s = oa5.reshape(tsn, A_WIDTH).astype(BF16)
    yn_s, conv_sn, ssm_sn = _ssd(rest_s.reshape(bs, ts, N_REST), dt_s.reshape(bs, ts, DT_PAD), conv_s, ssm_s,
                                 conv_w, conv_b, dt_bias, a_log, d_skip, g_ssm, ts)
    mg_s = _mix(oa_s, yn_s, rest_s, woa_b, wob_b, tsn)
    y_s = _out(mg_s, xs2, per_row(gate), g_post, wo_b, tsn, tsn).reshape(bs, ts, d)
    return y_p, y_s, kv_p, conv_p, ssm_p, list(kv_s), conv_sn, ssm_sn


def kernel(x_prompt, x_sample, cache_a_w128, cache_a_w512, cache_a_w2048, state_conv, state_ssm, c_prompt, c_sample, w_ada, b_ada, g_pre, g_post, w_in, conv_w, conv_b, dt_bias, a_log, d_skip, g_ssm, w_o_a, w_o_b, w_o):
    depth = w_in.shape[0]
    caches = (cache_a_w128, cache_a_w512, cache_a_w2048)
    y_p, y_s = x_prompt, x_sample
    acc = [[] for _ in range(10)]
    for i in range(depth):
        y_p, y_s, kv_p, conv_p, ssm_p, kv_s, conv_s, ssm_s = _layer(
            i, y_p, y_s, caches, state_conv[i], state_ssm[i],
            c_prompt, c_sample, w_ada[i], b_ada[i], g_pre[i], g_post[i], w_in[i], conv_w[i], conv_b[i],
            dt_bias[i], a_log[i], d_skip[i], g_ssm[i], w_o_a[i], w_o_b[i], w_o[i])
        for lst, v in zip(acc, (*kv_p, conv_p, ssm_p, *kv_s, conv_s, ssm_s)):
            lst.append(v)
    return (y_p, y_s, *[jnp.stack(v) for v in acc])
```

```python
import functools

import numpy as np
import jax
import jax.numpy as jnp
from jax import lax
from jax.experimental import pallas as pl
from jax.experimental.pallas import tpu as pltpu

F32 = jnp.float32
BF16 = jnp.bfloat16

D_MODEL = 2048
A_WINDOWS = (128, 512, 2048)
A_DILATIONS = (1, 4, 16)
A_GROUPS = 3
A_HEADS = 8
A_HEAD_DIM = 128
A_WIDTH = A_HEADS * A_HEAD_DIM
A_NK = 128
ATT_SCALE = A_HEAD_DIM ** -0.5
SSM_D_INNER = D_MODEL
SSM_HEAD_DIM = 64
SSM_HEADS = SSM_D_INNER // SSM_HEAD_DIM
SSM_GROUPS = 8
SSM_D_STATE = 128
SSM_CONV_W = 4
SSM_CHUNK = 128
SSM_CONV_DIM = SSM_D_INNER + 2 * SSM_GROUPS * SSM_D_STATE
QKV_WIDTH = A_GROUPS * A_WIDTH
NORM_EPS = 1e-6

LANES = 128
SUBLANES = 8
VMEM_LIMIT = 56 * 1024 * 1024
NEG = -1e30

COL = 1024
N_QKV_COLS = 3 * QKV_WIDTH
N_SLABS = N_QKV_COLS // LANES
XBC_OFF = 0
ZB_OFF = XBC_OFF + SSM_CONV_DIM
RA_OFF = ZB_OFF + SSM_D_INNER
RB_OFF = RA_OFF + D_MODEL
ZA_OFF = RB_OFF + D_MODEL
N_REST = ZA_OFF + A_WIDTH
DT_PAD = LANES
W_ZA = N_QKV_COLS
W_ZB = W_ZA + A_WIDTH
W_XBC = W_ZB + SSM_D_INNER
W_DT = W_XBC + SSM_CONV_DIM
W_RA = W_DT + SSM_HEADS
W_RB = W_RA + D_MODEL


def _cparams(*sem):
    return pltpu.CompilerParams(dimension_semantics=sem, vmem_limit_bytes=VMEM_LIMIT)


def _dot(a, b):
    return jnp.dot(a, b, preferred_element_type=F32)


def _dot_nt(a, b):
    return lax.dot_general(a, b, (((1,), (1,)), ((), ())), preferred_element_type=F32)


def _split3(x):
    x1 = x.astype(BF16)
    r1 = x - x1.astype(F32)
    x2 = r1.astype(BF16)
    x3 = (r1 - x2.astype(F32)).astype(BF16)
    return x1, x2, x3


def _dot_f32_by_01(x, e):
    x1, x2, x3 = _split3(x)
    return _dot(x1, e) + _dot(x2, e) + _dot(x3, e)


def _dot_01_by_f32(e, x):
    x1, x2, x3 = _split3(x)
    return _dot(e, x1) + _dot(e, x2) + _dot(e, x3)


def _silu(x):
    h = 0.5 * x
    return h + h * jnp.tanh(h)


def _ada_kernel(c_ref, w_ref, b_ref, o_ref):
    s = _silu(c_ref[...]).astype(BF16)
    o_ref[...] = _dot(s, w_ref[...].astype(BF16)) + b_ref[...]


def _ada(c, w_ada, b_ada):
    m, d = c.shape
    n = w_ada.shape[1]
    tn = 512
    return pl.pallas_call(
        _ada_kernel,
        grid=(n // tn,),
        in_specs=[pl.BlockSpec((m, d), lambda j: (0, 0)),
                  pl.BlockSpec((d, tn), lambda j: (0, j)),
                  pl.BlockSpec((1, tn), lambda j: (0, j))],
        out_specs=pl.BlockSpec((m, tn), lambda j: (0, j)),
        out_shape=jax.ShapeDtypeStruct((m, n), F32),
        compiler_params=_cparams("arbitrary"),
        name="ada",
    )(c, w_ada, b_ada.reshape(1, n))


def _shift_copies(c_ref, kv_ref, sem, depth_i, b, t):
    n = kv_ref.shape[1] - t
    return (pltpu.make_async_copy(c_ref.at[depth_i, b, pl.ds(t, n)], kv_ref.at[b, pl.ds(0, n)], sem),
            pltpu.make_async_copy(c_ref.at[depth_i, b, pl.ds(n, t)], kv_ref.at[b, pl.ds(n, t)], sem))


def _in_proj_kernel(*refs, tm, rc, n_qkv_steps, n_copy, depth_i, t_new):
    x_ref, sc_ref, sh_ref, g_ref, w_ref, wdt_ref = refs[:6]
    c_refs = refs[6:6 + n_copy]
    qkv_ref, rest_ref, dt_ref = refs[6 + n_copy:9 + n_copy]
    kv_refs = refs[9 + n_copy:9 + 2 * n_copy]
    h_scr = refs[9 + 2 * n_copy]
    sems = refs[10 + 2 * n_copy] if n_copy else None
    j = pl.program_id(1)

    if n_copy:
        step = pl.program_id(0) * pl.num_programs(1) + j
        first = 0
        for g in range(n_copy):
            nb = kv_refs[g].shape[0]

            @pl.when(jnp.logical_and(step >= first, step < first + nb))
            def _(g=g, first=first):
                for cp in _shift_copies(c_refs[g], kv_refs[g], sems.at[g], depth_i, step - first, t_new):
                    cp.start(priority=1)
            first += nb

        @pl.when(step == pl.num_programs(0) * pl.num_programs(1) - 1)
        def _():
            for g in range(n_copy):
                for _b in range(kv_refs[g].shape[0]):
                    for cp in _shift_copies(c_refs[g], kv_refs[g], sems.at[g], depth_i, 0, t_new):
                        cp.wait()

    @pl.when(j == 0)
    def _():
        per_row = sc_ref.shape[0] != 1

        def body(r, carry):
            rows = pl.ds(pl.multiple_of(r * rc, rc), rc)
            x = x_ref[rows, :]
            y = x * lax.rsqrt(jnp.mean(x * x, axis=-1, keepdims=True) + NORM_EPS) * g_ref[...]
            sc = sc_ref[rows, :] if per_row else sc_ref[...]
            sh = sh_ref[rows, :] if per_row else sh_ref[...]
            h_scr[rows, :] = (y * (1.0 + sc) + sh).astype(BF16)
            return carry

        lax.fori_loop(0, tm // rc, body, 0)
        dt_ref[...] = _dot(h_scr[...], wdt_ref[...])

    res = _dot(h_scr[...], w_ref[...])

    @pl.when(j < n_qkv_steps)
    def _():
        for c in range(COL // LANES):
            qkv_ref[c] = res[:, c * LANES:(c + 1) * LANES]

    @pl.when(j >= n_qkv_steps)
    def _():
        rest_ref[...] = res


def _in_proj(x2d, sc3, sh3, g_pre, w_main, w_dt, tm, rows_per_mod, caches=(), depth_i=0, t_new=0):
    t, d = x2d.shape
    r = sc3.shape[1]
    tn = COL
    rc = min(tm, 256)
    nq = N_QKV_COLS // tn
    spb = tn // LANES
    grid = (t // tm, (N_QKV_COLS + N_REST) // tn)
    n_copy = len(caches)
    assert sum(c.shape[1] for c in caches) <= grid[0] * grid[1]
    anyspec = pl.BlockSpec(memory_space=pl.ANY)
    mod_map = lambda i, j: ((i * tm) // rows_per_mod, 0, 0)
    return pl.pallas_call(
        functools.partial(_in_proj_kernel, tm=tm, rc=rc, n_qkv_steps=nq, n_copy=n_copy, depth_i=depth_i,
                          t_new=t_new),
        grid=grid,
        in_specs=[pl.BlockSpec((tm, d), lambda i, j: (i, 0)),
                  pl.BlockSpec((None, r, d), mod_map),
                  pl.BlockSpec((None, r, d), mod_map),
                  pl.BlockSpec((1, d), lambda i, j: (0, 0)),
                  pl.BlockSpec((d, tn), lambda i, j: (0, j)),
                  pl.BlockSpec((d, DT_PAD), lambda i, j: (0, 0))] + [anyspec] * n_copy,
        out_specs=[pl.BlockSpec((spb, tm, LANES), lambda i, j: (jnp.minimum(j, nq - 1), i, 0)),
                   pl.BlockSpec((tm, tn), lambda i, j: (i, jnp.maximum(j - nq, 0))),
                   pl.BlockSpec((tm, DT_PAD), lambda i, j: (i, 0))] + [anyspec] * n_copy,
        out_shape=[jax.ShapeDtypeStruct((N_SLABS, t, LANES), F32),
                   jax.ShapeDtypeStruct((t, N_REST), F32),
                   jax.ShapeDtypeStruct((t, DT_PAD), F32)]
        + [jax.ShapeDtypeStruct(c.shape[1:], c.dtype) for c in caches],
        scratch_shapes=[pltpu.VMEM((tm, d), BF16)] + ([pltpu.SemaphoreType.DMA((n_copy,))] if n_copy else []),
        compiler_params=_cparams("arbitrary", "arbitrary"),
        name="in_proj",
    )(x2d, sc3, sh3, g_pre.reshape(1, d), w_main, w_dt, *caches)


def _attn_prompt_kernel(*refs, tb):
    ng = A_GROUPS
    q_refs, kc_refs, kp_refs = refs[0:ng], refs[ng:2 * ng], refs[2 * ng:3 * ng]
    vc_refs, vp_refs = refs[3 * ng:4 * ng], refs[4 * ng:5 * ng]
    za_ref, oa_ref, o_scr, lse_scr = refs[5 * ng:]
    i = pl.program_id(2)
    nk = A_NK
    row = lax.broadcasted_iota(jnp.int32, (nk, 2 * nk), 0)
    col = lax.broadcasted_iota(jnp.int32, (nk, 2 * nk), 1)
    ok = jnp.logical_and(col >= row, col <= row + nk)
    ok_first = jnp.logical_and(ok, col >= jnp.where(i > 0, 0, nk))

    def load(g, rows, k_prev_ref, v_prev_ref, prev_rows):
        qh = q_refs[g][rows, :].astype(BF16)
        k2 = jnp.concatenate([k_prev_ref[prev_rows, :], kc_refs[g][rows, :]], axis=0).astype(BF16)
        v2 = jnp.concatenate([v_prev_ref[prev_rows, :], vc_refs[g][rows, :]], axis=0).astype(BF16)
        return qh, k2, v2

    def attend(qh, k2, v2, mask):
        s = jnp.where(mask, _dot_nt(qh, k2) * ATT_SCALE, NEG)
        m = jnp.max(s, axis=-1, keepdims=True)
        e = jnp.exp(s - m)
        z = jnp.sum(e, axis=-1, keepdims=True)
        return _dot(e.astype(BF16), v2) / z, m + jnp.log(z)

    unroll = 8

    def rows_of(d, sb, r):
        start = sb * (nk * d) + r
        if d == 1:
            return pl.ds(pl.multiple_of(start, nk), nk)
        return pl.ds(start, nk, stride=d)

    def run(g, items):
        d = A_DILATIONS[g]
        loaded, masks = [], []
        for sb, r in items:
            if isinstance(sb, int) and sb == 0:
                loaded.append(load(g, rows_of(d, 0, r), kp_refs[g], vp_refs[g], rows_of(d, 0, r)))
                masks.append(ok_first)
            else:
                loaded.append(load(g, rows_of(d, sb, r), kc_refs[g], vc_refs[g], rows_of(d, sb - 1, r)))
                masks.append(ok)
        results = [attend(*x, mask) for x, mask in zip(loaded, masks)]
        for (sb, r), (o, lse) in zip(items, results):
            o_scr[g, rows_of(d, sb, r), :] = o
            lse_scr[g, rows_of(d, sb, r), :] = jnp.broadcast_to(lse, (nk, LANES))

    for g in range(ng):
        d = A_DILATIONS[g]
        nblk = tb // nk
        assert nblk % unroll == 0 and (d % unroll == 0 or unroll % d == 0)
        if d >= unroll:
            per_sb = d // unroll
            run_first = lambda it, c, g=g: (run(g, [(0, it * unroll + u) for u in range(unroll)]), c)[1]
            lax.fori_loop(0, per_sb, run_first, 0)
            if nblk > d:
                def later(it, c, g=g, per_sb=per_sb):
                    sb = 1 + it // per_sb
                    r0 = (it % per_sb) * unroll
                    run(g, [(sb, r0 + u) for u in range(unroll)])
                    return c
                lax.fori_loop(0, (nblk - d) // unroll, later, 0)
        else:
            sbs = unroll // d
            run(g, [(sb, r) for sb in range(sbs) for r in range(d)])
            def later(it, c, g=g, d=d, sbs=sbs):
                run(g, [(it * sbs + s, r) for s in range(sbs) for r in range(d)])
                return c
            lax.fori_loop(1, nblk // unroll, later, 0)

    mc = 256

    def merge(c, carry):
        rows = pl.ds(pl.multiple_of(c * mc, mc), mc)
        l0, l1, l2 = lse_scr[0, rows, :], lse_scr[1, rows, :], lse_scr[2, rows, :]
        m = jnp.maximum(jnp.maximum(l0, l1), l2)
        e0, e1, e2 = jnp.exp(l0 - m), jnp.exp(l1 - m), jnp.exp(l2 - m)
        inv = 1.0 / (e0 + e1 + e2)
        oa = (e0 * inv) * o_scr[0, rows, :] + (e1 * inv) * o_scr[1, rows, :] + (e2 * inv) * o_scr[2, rows, :]
        oa_ref[rows, :] = (oa * _silu(za_ref[rows, :])).astype(BF16)
        return carry

    lax.fori_loop(0, tb // mc, merge, 0)


def _attn_prompt(qkv, rest, b, s):
    t = b * s
    nk = A_NK
    tb = nk * A_DILATIONS[-1]
    assert s % tb == 0
    nb = s // tb
    nh = A_GROUPS * A_HEADS

    def cur(kind, g):
        return pl.BlockSpec((None, tb, LANES), lambda bi, h, i: (kind * nh + g * A_HEADS + h, bi * nb + i, 0))

    def prev(kind, g):
        pb = nk * A_DILATIONS[g]
        return pl.BlockSpec((None, pb, LANES),
                            lambda bi, h, i: (kind * nh + g * A_HEADS + h,
                                              jnp.maximum((bi * s + i * tb) // pb - 1, 0), 0))

    groups = range(A_GROUPS)
    in_specs = ([cur(0, g) for g in groups] + [cur(1, g) for g in groups] + [prev(1, g) for g in groups]
                + [cur(2, g) for g in groups] + [prev(2, g) for g in groups]
                + [pl.BlockSpec((tb, LANES), lambda bi, h, i: (bi * nb + i, ZA_OFF // LANES + h))])
    return pl.pallas_call(
        functools.partial(_attn_prompt_kernel, tb=tb),
        grid=(b, A_HEADS, nb),
        in_specs=in_specs,
        out_specs=pl.BlockSpec((tb, LANES), lambda bi, h, i: (bi * nb + i, h)),
        out_shape=jax.ShapeDtypeStruct((t, A_WIDTH), BF16),
        scratch_shapes=[pltpu.VMEM((A_GROUPS, tb, LANES), F32), pltpu.VMEM((A_GROUPS, tb, LANES), F32)],
        compiler_params=_cparams("arbitrary", "arbitrary", "arbitrary"),
        name="attn_p",
    )(*([qkv] * (5 * A_GROUPS)), rest)


def _attn_decode_kernel(qn_ref, kn_ref, vn_ref, za_ref, c0_ref, c1_ref, c2_ref, kv0_any, kv1_any, kv2_any,
                        oa_ref, kv0_ref, kv1_ref, kv2_ref, *, t, chunk):
    del kv0_any, kv1_any, kv2_any
    tile = (A_HEADS, A_HEAD_DIM)
    log2e, ln2 = 1.4426950408889634, 0.6931471805599453

    for g, kv_ref in enumerate((kv0_ref, kv1_ref, kv2_ref)):
        kv_ref[:, 0] = kn_ref[g]
        kv_ref[:, 1] = vn_ref[g]

    def rows_of(g, tq, i0, n):
        d = A_DILATIONS[g]
        if g == A_GROUPS - 1:
            return c2_ref[i0:i0 + n, tq % d, 0], c2_ref[i0:i0 + n, tq % d, 1]
        c_ref = (c0_ref, c1_ref)[g]
        rows = pl.ds(tq % d + i0 * d, n, stride=d) if d > 1 else pl.ds(i0, n)
        return c_ref[rows, 0], c_ref[rows, 1]

    def partial(s, v):
        m = jnp.max(s, axis=0)
        p = jnp.exp2(s - m[None])
        return m, jnp.sum(p, axis=0), jnp.sum(p * v, axis=0)

    def combine(a, b):
        m = jnp.maximum(a[0], b[0])
        fa, fb = jnp.exp2(a[0] - m), jnp.exp2(b[0] - m)
        return m, a[1] * fa + b[1] * fb, a[2] * fa + b[2] * fb

    def attend(g, tq):
        d = A_DILATIONS[g]
        q = qn_ref[g, tq] * (ATT_SCALE * log2e)
        parts = []
        for i0 in range(0, A_NK, chunk):
            k, v = rows_of(g, tq, i0, chunk)
            s = jnp.broadcast_to(jnp.sum(k * q[None], axis=-1, keepdims=True), (chunk,) + tile)
            first_row = tq % d + i0 * d
            if first_row < tq:
                ri = first_row + d * lax.broadcasted_iota(jnp.int32, (chunk,) + tile, 0)
                s = jnp.where(ri >= tq, s, NEG)
            parts.append(partial(s, v))
        js = [tk for tk in range(tq + 1) if (tq - tk) % d == 0]
        s_new = jnp.stack([jnp.broadcast_to(jnp.sum(kn_ref[g, tk] * q, axis=-1, keepdims=True), tile)
                           for tk in js])
        parts.append(partial(s_new, jnp.stack([vn_ref[g, tk] for tk in js])))
        while len(parts) > 1:
            parts = [combine(parts[i], parts[i + 1]) if i + 1 < len(parts) else parts[i]
                     for i in range(0, len(parts), 2)]
        m, l, acc = parts[0]
        return acc / l, m * ln2 + jnp.log(l)

    for tq in range(t):
        outs = [attend(g, tq) for g in range(A_GROUPS)]
        lses = [o[1] for o in outs]
        mm = functools.reduce(jnp.maximum, lses)
        es = [jnp.exp(l - mm) for l in lses]
        inv = 1.0 / sum(es)
        oa = sum((e * inv) * o[0] for e, o in zip(es, outs))
        oa_ref[tq] = oa * _silu(za_ref[tq])


def _attn_decode(qkvn, za5, caches, kv_shifted, depth_i):
    _, ng, b, t, nh, hd = qkvn.shape
    wbs = [c.shape[2] for c in caches]
    d2 = A_DILATIONS[-1]
    assert t == SUBLANES and tuple(wbs) == A_WINDOWS and t <= d2
    assert all(w == A_NK * d for w, d in zip(wbs, A_DILATIONS))
    c2 = caches[-1].reshape(caches[-1].shape[:2] + (wbs[-1] // d2, d2, 2, nh, hd))
    new = lambda kind: pl.BlockSpec((None, ng, None, t, nh, hd), lambda bi: (kind, 0, bi, 0, 0, 0))
    whole = lambda n: pl.BlockSpec((None, None, n, 2, nh, hd), lambda bi: (depth_i, bi, 0, 0, 0, 0))
    tail = lambda n: pl.BlockSpec((None, t, 2, nh, hd), lambda bi: (bi, n // t - 1, 0, 0, 0))
    anyspec = pl.BlockSpec(memory_space=pl.ANY)
    res = pl.pallas_call(
        functools.partial(_attn_decode_kernel, t=t, chunk=32),
        grid=(b,),
        in_specs=[new(0), new(1), new(2),
                  pl.BlockSpec((None, t, nh, hd), lambda bi: (bi, 0, 0, 0)),
                  whole(wbs[0]), whole(wbs[1]),
                  pl.BlockSpec((None, None, wbs[-1] // d2, t, 2, nh, hd), lambda bi: (depth_i, bi, 0, 0, 0, 0, 0)),
                  anyspec, anyspec, anyspec],
        out_specs=[pl.BlockSpec((None, t, nh, hd), lambda bi: (bi, 0, 0, 0))] + [tail(w) for w in wbs],
        out_shape=[jax.ShapeDtypeStruct((b, t, nh, hd), F32)]
        + [jax.ShapeDtypeStruct((b, w, 2, nh, hd), F32) for w in wbs],
        input_output_aliases={7: 1, 8: 2, 9: 3},
        compiler_params=_cparams("arbitrary"),
        name="attn_d",
    )(qkvn, qkvn, qkvn, za5, caches[0], caches[1], c2, *kv_shifted)
    return res[0], res[1:]


def _ssd_kernel(xbc_ref, dt_ref, zb_ref, conv0_ref, h0_ref, convw_ref, convb_ref, dtb_ref, alog_ref,
                dskip_ref, gssm_ref, e_ref, yn_ref, convo_ref, ho_ref, ext_scr, st_scr, *, nv):
    q = SSM_CHUNK
    c = pl.program_id(1)
    di = SSM_D_INNER
    gw = di // SSM_GROUPS
    ns = SSM_D_STATE
    hp = SUBLANES
    kw = SSM_CONV_W

    nslab = SSM_CONV_DIM // LANES
    lanes_of = lambda j: slice(j * LANES, (j + 1) * LANES)

    @pl.when(c == 0)
    def _():
        for j in range(nslab):
            ext_scr[j, 0:hp, :] = conv0_ref[:, lanes_of(j)]
        for j in range(di // LANES):
            st_scr[:, lanes_of(j)] = h0_ref[j].T

    for j in range(nslab):
        ext_scr[j, hp:hp + nv, :] = xbc_ref[:, lanes_of(j)]
        if nv != q:
            ext_scr[j, hp + nv:hp + q, :] = jnp.zeros((q - nv, LANES), F32)

    def conv_act(c0, c1):
        pieces = []
        for j in range(c0 // LANES, c1 // LANES):
            acc = convb_ref[:, lanes_of(j)]
            for k in range(kw):
                acc = acc + convw_ref[k:k + 1, lanes_of(j)] * ext_scr[j, pl.ds(hp - (kw - 1) + k, q), :]
            pieces.append(_silu(acc))
        return pieces[0] if len(pieces) == 1 else jnp.concatenate(pieces, axis=1)

    row = lax.broadcasted_iota(jnp.int32, (q, q), 0)
    col = lax.broadcasted_iota(jnp.int32, (q, q), 1)
    causal = row >= col
    if nv == q:
        dt_raw = dt_ref[...]
    else:
        dt_raw = jnp.concatenate([dt_ref[...], jnp.zeros((q - nv, DT_PAD), F32)], axis=0)
    dt = jax.nn.softplus(dt_raw + dtb_ref[...])
    if nv != q:
        dt = jnp.where(row < nv, dt, 0.0)
    da = dt * (-jnp.exp(alog_ref[...]))
    cs = _dot_01_by_f32(jnp.where(causal, 1.0, 0.0).astype(BF16), da)
    cs_t = cs.T
    dt_parts, cs_parts = _split3(dt), _split3(cs)
    half = lax.broadcasted_iota(jnp.int32, (q, LANES), 1) < SSM_HEAD_DIM

    for g in range(SSM_GROUPS):
        gs = slice(g * gw, (g + 1) * gw)
        e_g = e_ref[:, gs]
        dt_x = sum(_dot(p, e_g) for p in dt_parts)
        cs_x = sum(_dot(p, e_g) for p in cs_parts)
        cs_last = cs_x[q - 1:q, :]
        xs = conv_act(g * gw, (g + 1) * gw)
        xdt = xs * dt_x
        xdt_b = xdt.astype(BF16)
        w_b = (xdt * jnp.exp(cs_last - cs_x)).astype(BF16)
        bg_t = conv_act(di + g * ns, di + (g + 1) * ns).T.astype(BF16)
        cg = conv_act(di + SSM_GROUPS * ns + g * ns, di + SSM_GROUPS * ns + (g + 1) * ns).astype(BF16)
        cb = _dot(cg, bg_t)
        st = st_scr[:, gs]
        y_off = _dot(cg, st.astype(BF16)) * jnp.exp(cs_x)
        st_scr[:, gs] = st * jnp.exp(cs_last) + _dot(bg_t, w_b)
        y_pairs = []
        for pair in range(gw // LANES):
            h0 = (g * gw + pair * LANES) // SSM_HEAD_DIM
            ps = slice(pair * LANES, (pair + 1) * LANES)
            ys = []
            for h in (h0, h0 + 1):
                diff = cs[:, h:h + 1] - cs_t[h:h + 1, :]
                m_h = (cb * jnp.exp(jnp.where(causal, diff, NEG))).astype(BF16)
                ys.append(_dot(m_h, xdt_b[:, ps]))
            y_pairs.append(jnp.where(half, ys[0], ys[1]))
        y = jnp.concatenate(y_pairs, axis=1) + y_off + xs * dskip_ref[:, gs]
        v = y[0:nv, :] * _silu(zb_ref[:, gs])
        vn = v * lax.rsqrt(jnp.mean(v * v, axis=-1, keepdims=True) + NORM_EPS)
        yn_ref[:, gs] = (vn * gssm_ref[:, gs]).astype(BF16)

    @pl.when(c == pl.num_programs(1) - 1)
    def _():
        for j in range(di // LANES):
            ho_ref[j] = st_scr[:, j * LANES:(j + 1) * LANES].T

    for j in range(nslab):
        tail = ext_scr[j, hp + nv - (kw - 1):hp + nv, :]
        convo_ref[:, lanes_of(j)] = tail
        ext_scr[j, hp - (kw - 1):hp, :] = tail


def _ssd(rest3, dt3, conv0, h0, conv_w, conv_b, dt_bias, a_log, d_skip, g_ssm, nv):
    b, l, _ = rest3.shape
    q = SSM_CHUNK
    nc = l // nv
    assert nv == q or nc == 1
    di, cd, nh = SSM_D_INNER, SSM_CONV_DIM, SSM_HEADS
    hp = SUBLANES
    conv0p = jnp.pad(conv0, ((0, 0), (hp - (SSM_CONV_W - 1), 0), (0, 0)))
    h0v = h0.reshape(b, di // LANES, LANES, SSM_D_STATE)
    pad1 = lambda v: jnp.pad(v.reshape(1, nh), ((0, 0), (0, DT_PAD - nh)))
    expand = np.zeros((DT_PAD, di), np.float32)
    expand[np.arange(di) // SSM_HEAD_DIM, np.arange(di)] = 1.0
    const = lambda shape: pl.BlockSpec(shape, lambda bi, c: (0,) * len(shape))
    yn, conv_o, h_o = pl.pallas_call(
        functools.partial(_ssd_kernel, nv=nv),
        grid=(b, nc),
        in_specs=[pl.BlockSpec((None, nv, cd), lambda bi, c: (bi, c, XBC_OFF // cd)),
                  pl.BlockSpec((None, nv, DT_PAD), lambda bi, c: (bi, c, 0)),
                  pl.BlockSpec((None, nv, di), lambda bi, c: (bi, c, ZB_OFF // di)),
                  pl.BlockSpec((None, hp, cd), lambda bi, c: (bi, 0, 0)),
                  pl.BlockSpec((None, di // LANES, LANES, SSM_D_STATE), lambda bi, c: (bi, 0, 0, 0)),
                  const((SSM_CONV_W, cd)), const((1, cd)), const((1, DT_PAD)), const((1, DT_PAD)),
                  const((1, di)), const((1, di)), const((DT_PAD, di))],
        out_specs=[pl.BlockSpec((None, nv, di), lambda bi, c: (bi, c, 0)),
                   pl.BlockSpec((None, SSM_CONV_W - 1, cd), lambda bi, c: (bi, 0, 0)),
                   pl.BlockSpec((None, di // LANES, LANES, SSM_D_STATE), lambda bi, c: (bi, 0, 0, 0))],
        out_shape=[jax.ShapeDtypeStruct((b, l, di), BF16),
                   jax.ShapeDtypeStruct((b, SSM_CONV_W - 1, cd), F32),
                   jax.ShapeDtypeStruct((b, di // LANES, LANES, SSM_D_STATE), F32)],
        scratch_shapes=[pltpu.VMEM((cd // LANES, hp + q, LANES), F32),
                        pltpu.VMEM((SSM_D_STATE, di), F32)],
        compiler_params=_cparams("arbitrary", "arbitrary"),
        name=f"ssd{nv}",
    )(rest3, dt3, rest3, conv0p, h0v, conv_w, conv_b.reshape(1, cd), pad1(dt_bias), pad1(a_log),
      jnp.repeat(d_skip, SSM_HEAD_DIM).reshape(1, di), g_ssm.reshape(1, di), jnp.asarray(expand, BF16))
    return yn.reshape(b * l, di), conv_o, h_o.reshape(b, nh, SSM_HEAD_DIM, SSM_D_STATE)


def _mix_kernel(oa_ref, yn_ref, ra_ref, rb_ref, woa_ref, wob_ref, mg_ref):
    p_a = _dot(oa_ref[...], woa_ref[...])
    p_b = _dot(yn_ref[...], wob_ref[...])
    mg_ref[...] = (jax.nn.sigmoid(ra_ref[...]) * p_a + jax.nn.sigmoid(rb_ref[...]) * p_b).astype(BF16)


def _mix(oa, yn, rest, w_o_a, w_o_b, tm):
    t = rest.shape[0]
    d = D_MODEL
    row = lambda w, off: pl.BlockSpec((tm, w), lambda i: (i, off // w))
    const = lambda shape: pl.BlockSpec(shape, lambda i: (0, 0))
    return pl.pallas_call(
        _mix_kernel,
        grid=(t // tm,),
        in_specs=[row(A_WIDTH, 0), row(d, 0), row(d, RA_OFF), row(d, RB_OFF),
                  const((A_WIDTH, d)), const((d, d))],
        out_specs=row(d, 0),
        out_shape=jax.ShapeDtypeStruct((t, d), BF16),
        compiler_params=_cparams("arbitrary"),
        name="mix",
    )(oa, yn, rest, rest, w_o_a, w_o_b)


def _out_kernel(mg_ref, x_ref, gate_ref, gpost_ref, wo_ref, y_ref):
    out = _dot(mg_ref[...], wo_ref[...])
    nrm = out * lax.rsqrt(jnp.mean(out * out, axis=-1, keepdims=True) + NORM_EPS) * gpost_ref[...]
    y_ref[...] = x_ref[...] + gate_ref[...] * nrm


def _out(mg, x2d, gate3, g_post, w_o, tm, rows_per_mod):
    t, d = x2d.shape
    r = gate3.shape[1]
    return pl.pallas_call(
        _out_kernel,
        grid=(t // tm,),
        in_specs=[pl.BlockSpec((tm, d), lambda i: (i, 0)),
                  pl.BlockSpec((tm, d), lambda i: (i, 0)),
                  pl.BlockSpec((None, r, d), lambda i: ((i * tm) // rows_per_mod, 0, 0)),
                  pl.BlockSpec((1, d), lambda i: (0, 0)),
                  pl.BlockSpec((d, d), lambda i: (0, 0))],
        out_specs=pl.BlockSpec((tm, d), lambda i: (i, 0)),
        out_shape=jax.ShapeDtypeStruct((t, d), F32),
        compiler_params=_cparams("arbitrary"),
        name="out",
    )(mg, x2d, gate3, g_post.reshape(1, d), w_o)


def _window_kernel(k_ref, v_ref, o_ref, *, rows):
    per_row = 2 * A_HEADS
    for kind, src in enumerate((k_ref, v_ref)):
        for h in range(A_HEADS):
            o_ref[pl.ds(kind * A_HEADS + h, rows, stride=per_row), :] = src[h]


def _window_rows(qkv, g, b, s, w):
    nh = A_GROUPS * A_HEADS
    rows = min(w, 256)
    per_row = 2 * A_HEADS
    assert w % rows == 0 and (s - w) % rows == 0
    src = lambda kind: pl.BlockSpec(
        (A_HEADS, rows, A_HEAD_DIM),
        lambda bi, i: (kind * A_GROUPS + g, (bi * s + s - w) // rows + i, 0))
    out = pl.pallas_call(
        functools.partial(_window_kernel, rows=rows),
        grid=(b, w // rows),
        in_specs=[src(1), src(2)],
        out_specs=pl.BlockSpec((None, rows * per_row, A_HEAD_DIM), lambda bi, i: (bi, i, 0)),
        out_shape=jax.ShapeDtypeStruct((b, w * per_row, A_HEAD_DIM), F32),
        compiler_params=_cparams("arbitrary", "arbitrary"),
        name=f"window{g}",
    )(qkv, qkv)
    return out.reshape(b, w, 2, A_HEADS, A_HEAD_DIM)


def _prep_w_kernel(tbl_ref, a_ref, b_ref, dt_ref, w_ref, wdt_ref):
    j = pl.program_id(0)
    d = a_ref.shape[1]
    late = tbl_ref[1, j]
    t = LANES

    def put(r, tile_rows):
        for c in range(d // t):
            w_ref[c * t:(c + 1) * t, r * t:(r + 1) * t] = tile_rows[:, c * t:(c + 1) * t].T.astype(BF16)

    @pl.when(late == 0)
    def _():
        for r in range(COL // t):
            put(r, a_ref[r * t:(r + 1) * t, :])

    @pl.when(late != 0)
    def _():
        for r in range(COL // t - 1):
            put(r, a_ref[r * t + SSM_HEADS:(r + 1) * t + SSM_HEADS, :])
        put(COL // t - 1, jnp.concatenate([a_ref[COL - t + SSM_HEADS:COL, :], b_ref[...]], axis=0))

    @pl.when(j == 0)
    def _():
        lane = lax.broadcasted_iota(jnp.int32, (t, t), 1)
        for c in range(d // t):
            blk = dt_ref[:, c * t:(c + 1) * t].T
            wdt_ref[c * t:(c + 1) * t, :] = jnp.where(lane < SSM_HEADS, blk, 0.0).astype(BF16)


def _prep_w(w_t):
    d = w_t.shape[1]
    assert W_DT % COL == 0 and W_RB - W_RA == D_MODEL and DT_PAD == LANES and SSM_HEADS % SUBLANES == 0
    blk = lambda off, n: [off // COL + k for k in range(n // COL)]
    plain = blk(0, N_QKV_COLS) + blk(W_XBC, SSM_CONV_DIM) + blk(W_ZB, SSM_D_INNER)
    late = blk(W_DT, 2 * D_MODEL)
    za = blk(W_ZA, A_WIDTH)
    src = plain + late + za
    is_late = [0] * len(plain) + [1] * len(late) + [0] * len(za)
    nxt = [0] * len(plain) + [(k + 1) * COL // SSM_HEADS for k in late] + [0] * len(za)
    tbl = jnp.asarray(np.array([src, is_late, nxt], np.int32))
    n = len(src)
    grid_spec = pltpu.PrefetchScalarGridSpec(
        num_scalar_prefetch=1,
        grid=(n,),
        in_specs=[pl.BlockSpec((COL, d), lambda j, t: (t[0, j], 0)),
                  pl.BlockSpec((SSM_HEADS, d), lambda j, t: (t[2, j], 0)),
                  pl.BlockSpec((DT_PAD, d), lambda j, t: (W_DT // DT_PAD, 0))],
        out_specs=[pl.BlockSpec((d, COL), lambda j, t: (0, j)),
                   pl.BlockSpec((d, DT_PAD), lambda j, t: (0, 0))])
    return pl.pallas_call(
        _prep_w_kernel,
        grid_spec=grid_spec,
        out_shape=[jax.ShapeDtypeStruct((d, n * COL), BF16), jax.ShapeDtypeStruct((d, DT_PAD), BF16)],
        compiler_params=_cparams("arbitrary"),
        name="prep_w",
    )(tbl, w_t, w_t, w_t)


def _layer(depth_i, x_p, x_s, caches, conv_s, ssm_s, c_p, c_s, w_ada, b_ada, g_pre, g_post, w_in, conv_w,
           conv_b, dt_bias, a_log, d_skip, g_ssm, w_o_a, w_o_b, w_o):
    bp, sp, d = x_p.shape
    bs, ts, _ = x_s.shape
    assert sp % SSM_CHUNK == 0

    w_main, w_dt = _prep_w(jnp.swapaxes(w_in, 0, 1))
    woa_b, wob_b, wo_b = w_o_a.astype(BF16), w_o_b.astype(BF16), w_o.astype(BF16)

    nmod = bp + bs
    mpad = -(-nmod // SUBLANES) * SUBLANES
    c_all = jnp.pad(jnp.concatenate([c_p, c_s], axis=0), ((0, mpad - nmod), (0, 0)))
    mod = _ada(c_all, w_ada, b_ada)
    shift, scale, gate = mod[:, :d], mod[:, d:2 * d], mod[:, 2 * d:]
    per_seq = lambda v: v[:bp].reshape(bp, 1, d)
    per_row = lambda v: jnp.repeat(v[bp:nmod], ts, axis=0).reshape(1, bs * ts, d)

    tp = bp * sp
    xp2 = x_p.reshape(tp, d)
    qkv_p, rest_p, dt_p, *kv_shifted = _in_proj(xp2, per_seq(scale), per_seq(shift), g_pre, w_main, w_dt, 1024, sp,
                                                caches, depth_i, ts)
    oa_p = _attn_prompt(qkv_p, rest_p, bp, sp)
    conv0 = jnp.zeros((bp, SSM_CONV_W - 1, SSM_CONV_DIM), F32)
    ssm0 = jnp.zeros((bp, SSM_HEADS, SSM_HEAD_DIM, SSM_D_STATE), F32)
    yn_p, conv_p, ssm_p = _ssd(rest_p.reshape(bp, sp, N_REST), dt_p.reshape(bp, sp, DT_PAD), conv0, ssm0,
                               conv_w, conv_b, dt_bias, a_log, d_skip, g_ssm, SSM_CHUNK)
    mg_p = _mix(oa_p, yn_p, rest_p, woa_b, wob_b, 256)
    y_p = _out(mg_p, xp2, per_seq(gate), g_post, wo_b, 512, sp).reshape(bp, sp, d)
    kv_p = [_window_rows(qkv_p, g, bp, sp, min(A_WINDOWS[g], sp)) for g in range(A_GROUPS)]

    tsn = bs * ts
    xs2 = x_s.reshape(tsn, d)
    qkv_s, rest_s, dt_s = _in_proj(xs2, per_row(scale), per_row(shift), g_pre, w_main, w_dt, tsn, tsn)[:3]
    qkvn = jnp.transpose(qkv_s.reshape(3, A_GROUPS, A_HEADS, bs, ts, A_HEAD_DIM), (0, 1, 3, 4, 2, 5))
    za5 = rest_s[:, ZA_OFF:ZA_OFF + A_WIDTH].reshape(bs, ts, A_HEADS, A_HEAD_DIM)
    oa5, kv_s = _attn_decode(qkvn, za5, caches, kv_shifted, depth_i)
    oa_s = oa5.reshape(tsn, A_WIDTH).astype(BF16)
    yn_s, conv_sn, ssm_sn = _ssd(rest_s.reshape(bs, ts, N_REST), dt_s.reshape(bs, ts, DT_PAD), conv_s, ssm_s,
                                 conv_w, conv_b, dt_bias, a_log, d_skip, g_ssm, ts)
    mg_s = _mix(oa_s, yn_s, rest_s, woa_b, wob_b, tsn)
    y_s = _out(mg_s, xs2, per_row(gate), g_post, wo_b, tsn, tsn).reshape(bs, ts, d)
    return y_p, y_s, kv_p, conv_p, ssm_p, list(kv_s), conv_sn, ssm_sn


def kernel(x_prompt, x_sample, cache_a_w128, cache_a_w512, cache_a_w2048, state_conv, state_ssm, c_prompt, c_sample, w_ada, b_ada, g_pre, g_post, w_in, conv_w, conv_b, dt_bias, a_log, d_skip, g_ssm, w_o_a, w_o_b, w_o):
    depth = w_in.shape[0]
    caches = (cache_a_w128, cache_a_w512, cache_a_w2048)
    y_p, y_s = x_prompt, x_sample
    acc = [[] for _ in range(10)]
    for i in range(depth):
        y_p, y_s, kv_p, conv_p, ssm_p, kv_s, conv_s, ssm_s = _layer(
            i, y_p, y_s, caches, state_conv[i], state_ssm[i],
            c_prompt, c_sample, w_ada[i], b_ada[i], g_pre[i], g_post[i], w_in[i], conv_w[i], conv_b[i],
            dt_bias[i], a_log[i], d_skip[i], g_ssm[i], w_o_a[i], w_o_b[i], w_o[i])
        for lst, v in zip(acc, (*kv_p, conv_p, ssm_p, *kv_s, conv_s, ssm_s)):
            lst.append(v)
    return (y_p, y_s, *[jnp.stack(v) for v in acc])
```

```python
import functools

import numpy as np
import jax
import jax.numpy as jnp
from jax import lax
from jax.experimental import pallas as pl
from jax.experimental.pallas import tpu as pltpu

F32 = jnp.float32
BF16 = jnp.bfloat16

D_MODEL = 2048
A_WINDOWS = (128, 512, 2048)
A_DILATIONS = (1, 4, 16)
A_GROUPS = 3
A_HEADS = 8
A_HEAD_DIM = 128
A_WIDTH = A_HEADS * A_HEAD_DIM
A_NK = 128
ATT_SCALE = A_HEAD_DIM ** -0.5
SSM_D_INNER = D_MODEL
SSM_HEAD_DIM = 64
SSM_HEADS = SSM_D_INNER // SSM_HEAD_DIM
SSM_GROUPS = 8
SSM_D_STATE = 128
SSM_CONV_W = 4
SSM_CHUNK = 128
SSM_CONV_DIM = SSM_D_INNER + 2 * SSM_GROUPS * SSM_D_STATE
QKV_WIDTH = A_GROUPS * A_WIDTH
NORM_EPS = 1e-6

LANES = 128
SUBLANES = 8
VMEM_LIMIT = 60 * 1024 * 1024
NEG = -1e30

COL = 1024
N_QKV_COLS = 3 * QKV_WIDTH
N_SLABS = N_QKV_COLS // LANES
XBC_OFF = 0
ZB_OFF = XBC_OFF + SSM_CONV_DIM
RA_OFF = ZB_OFF + SSM_D_INNER
RB_OFF = RA_OFF + D_MODEL
ZA_OFF = RB_OFF + D_MODEL
N_REST = ZA_OFF + A_WIDTH
DT_PAD = LANES
W_ZA = N_QKV_COLS
W_ZB = W_ZA + A_WIDTH
W_XBC = W_ZB + SSM_D_INNER
W_DT = W_XBC + SSM_CONV_DIM
W_RA = W_DT + SSM_HEADS
W_RB = W_RA + D_MODEL


def _cparams(*sem):
    return pltpu.CompilerParams(dimension_semantics=sem, vmem_limit_bytes=VMEM_LIMIT)


def _dot(a, b):
    return jnp.dot(a, b, preferred_element_type=F32)


def _dot_nt(a, b):
    return lax.dot_general(a, b, (((1,), (1,)), ((), ())), preferred_element_type=F32)


def _split3(x):
    x1 = x.astype(BF16)
    r1 = x - x1.astype(F32)
    x2 = r1.astype(BF16)
    x3 = (r1 - x2.astype(F32)).astype(BF16)
    return x1, x2, x3


def _dot_f32_by_01(x, e):
    x1, x2, x3 = _split3(x)
    return _dot(x1, e) + _dot(x2, e) + _dot(x3, e)


def _dot_01_by_f32(e, x):
    x1, x2, x3 = _split3(x)
    return _dot(e, x1) + _dot(e, x2) + _dot(e, x3)


def _silu(x):
    h = 0.5 * x
    return h + h * jnp.tanh(h)


def _ada_kernel(c_ref, w_ref, b_ref, o_ref):
    s = _silu(c_ref[...]).astype(BF16)
    o_ref[...] = _dot(s, w_ref[...].astype(BF16)) + b_ref[...]


def _ada(c, w_ada, b_ada):
    m, d = c.shape
    n = w_ada.shape[1]
    tn = 512
    return pl.pallas_call(
        _ada_kernel,
        grid=(n // tn,),
        in_specs=[pl.BlockSpec((m, d), lambda j: (0, 0)),
                  pl.BlockSpec((d, tn), lambda j: (0, j)),
                  pl.BlockSpec((1, tn), lambda j: (0, j))],
        out_specs=pl.BlockSpec((m, tn), lambda j: (0, j)),
        out_shape=jax.ShapeDtypeStruct((m, n), F32),
        compiler_params=_cparams("arbitrary"),
        name="ada",
    )(c, w_ada, b_ada.reshape(1, n))


SHIFT_CHUNK_ROWS = 384


def _shift_plan(caches, t):
    plan, first = [], 0
    for c in caches:
        nb, n = c.shape[1], c.shape[2] - t
        cps = next(k for k in range(-(-n // SHIFT_CHUNK_ROWS), n + 1) if n % k == 0)
        plan.append((n // cps, cps, nb, first))
        first += nb * cps
    return plan, first


def _shift_step(step, c_refs, kv_refs, buf, in_sem, out_sem, tail_sem, plan, depth_i, t):
    def where(g, c):
        rows, cps, _, first = plan[g]
        return c // cps, c % cps, (c + first % 2) % 2, rows, cps

    def read(g, c):
        b, ci, slot, rows, _ = where(g, c)
        return pltpu.make_async_copy(c_refs[g].at[depth_i, b, pl.ds(t + ci * rows, rows)],
                                     buf.at[slot, pl.ds(0, rows)], in_sem.at[slot])

    def write(g, c):
        b, ci, slot, rows, _ = where(g, c)
        return pltpu.make_async_copy(buf.at[slot, pl.ds(0, rows)], kv_refs[g].at[b, pl.ds(ci * rows, rows)],
                                     out_sem.at[slot])

    def write_tail(g, c):
        b, _, slot, rows, cps = where(g, c)
        return pltpu.make_async_copy(buf.at[slot, pl.ds(rows - t, t)], kv_refs[g].at[b, pl.ds(cps * rows, t)],
                                     tail_sem.at[slot])

    def ends_sequence(g, c):
        return c % plan[g][1] == plan[g][1] - 1

    def in_group(g, k):
        first, n = plan[g][3], plan[g][2] * plan[g][1]
        return jnp.logical_and(k >= first, k < first + n)

    for g in range(len(plan)):
        @pl.when(in_group(g, step - 2))
        def _(g=g):
            c = step - 2 - plan[g][3]
            write(g, c).wait()
            pl.when(ends_sequence(g, c))(lambda: write_tail(g, c).wait())

    for g in range(len(plan)):
        @pl.when(in_group(g, step))
        def _(g=g):
            read(g, step - plan[g][3]).start()

    for g in range(len(plan)):
        @pl.when(in_group(g, step - 1))
        def _(g=g):
            c = step - 1 - plan[g][3]
            read(g, c).wait()
            write(g, c).start()
            pl.when(ends_sequence(g, c))(lambda: write_tail(g, c).start())


def _in_proj_kernel(*refs, tm, rc, n_qkv_steps, plan, depth_i, t_new):
    n_copy = len(plan)
    x_ref, sc_ref, sh_ref, g_ref, w_ref, wdt_ref = refs[:6]
    c_refs = refs[6:6 + n_copy]
    qkv_ref, rest_ref, dt_ref = refs[6 + n_copy:9 + n_copy]
    kv_refs = refs[9 + n_copy:9 + 2 * n_copy]
    h_scr = refs[9 + 2 * n_copy]
    j = pl.program_id(1)

    if n_copy:
        buf, in_sem, out_sem, tail_sem = refs[10 + 2 * n_copy:]
        step = pl.program_id(0) * pl.num_programs(1) + j
        _shift_step(step, c_refs, kv_refs, buf, in_sem, out_sem, tail_sem, plan, depth_i, t_new)

    @pl.when(j == 0)
    def _():
        per_row = sc_ref.shape[0] != 1

        def body(r, carry):
            rows = pl.ds(pl.multiple_of(r * rc, rc), rc)
            x = x_ref[rows, :]
            y = x * lax.rsqrt(jnp.mean(x * x, axis=-1, keepdims=True) + NORM_EPS) * g_ref[...]
            sc = sc_ref[rows, :] if per_row else sc_ref[...]
            sh = sh_ref[rows, :] if per_row else sh_ref[...]
            h_scr[rows, :] = (y * (1.0 + sc) + sh).astype(BF16)
            return carry

        lax.fori_loop(0, tm // rc, body, 0)
        dt_ref[...] = _dot(h_scr[...], wdt_ref[...])

    res = _dot(h_scr[...], w_ref[...])

    @pl.when(j < n_qkv_steps)
    def _():
        for c in range(COL // LANES):
            qkv_ref[c] = res[:, c * LANES:(c + 1) * LANES]

    @pl.when(j >= n_qkv_steps)
    def _():
        rest_ref[...] = res


def _in_proj(x2d, sc3, sh3, g_pre, w_main, w_dt, tm, rows_per_mod, caches=(), depth_i=0, t_new=0):
    t, d = x2d.shape
    r = sc3.shape[1]
    tn = COL
    rc = min(tm, 256)
    nq = N_QKV_COLS // tn
    spb = tn // LANES
    grid = (t // tm, (N_QKV_COLS + N_REST) // tn)
    n_copy = len(caches)
    plan, n_chunks = _shift_plan(caches, t_new)
    assert n_chunks + 2 <= grid[0] * grid[1]
    shift_scratch = []
    if n_copy:
        stage_rows = max(p[0] for p in plan)
        shift_scratch = [pltpu.VMEM((2, stage_rows) + caches[0].shape[3:], caches[0].dtype)] + [
            pltpu.SemaphoreType.DMA((2,))] * 3
    anyspec = pl.BlockSpec(memory_space=pl.ANY)
    mod_map = lambda i, j: ((i * tm) // rows_per_mod, 0, 0)
    return pl.pallas_call(
        functools.partial(_in_proj_kernel, tm=tm, rc=rc, n_qkv_steps=nq, plan=tuple(plan), depth_i=depth_i,
                          t_new=t_new),
        grid=grid,
        in_specs=[pl.BlockSpec((tm, d), lambda i, j: (i, 0)),
                  pl.BlockSpec((None, r, d), mod_map),
                  pl.BlockSpec((None, r, d), mod_map),
                  pl.BlockSpec((1, d), lambda i, j: (0, 0)),
                  pl.BlockSpec((d, tn), lambda i, j: (0, j)),
                  pl.BlockSpec((d, DT_PAD), lambda i, j: (0, 0))] + [anyspec] * n_copy,
        out_specs=[pl.BlockSpec((spb, tm, LANES), lambda i, j: (jnp.minimum(j, nq - 1), i, 0)),
                   pl.BlockSpec((tm, tn), lambda i, j: (i, jnp.maximum(j - nq, 0))),
                   pl.BlockSpec((tm, DT_PAD), lambda i, j: (i, 0))] + [anyspec] * n_copy,
        out_shape=[jax.ShapeDtypeStruct((N_SLABS, t, LANES), F32),
                   jax.ShapeDtypeStruct((t, N_REST), F32),
                   jax.ShapeDtypeStruct((t, DT_PAD), F32)]
        + [jax.ShapeDtypeStruct(c.shape[1:], c.dtype) for c in caches],
        scratch_shapes=[pltpu.VMEM((tm, d), BF16)] + shift_scratch,
        compiler_params=_cparams("arbitrary", "arbitrary"),
        name="in_proj",
    )(x2d, sc3, sh3, g_pre.reshape(1, d), w_main, w_dt, *caches)


def _attn_prompt_kernel(*refs, tb):
    ng = A_GROUPS
    q_refs, kc_refs, kp_refs = refs[0:ng], refs[ng:2 * ng], refs[2 * ng:3 * ng]
    vc_refs, vp_refs = refs[3 * ng:4 * ng], refs[4 * ng:5 * ng]
    za_ref, oa_ref, o_scr, lse_scr = refs[5 * ng:]
    i = pl.program_id(2)
    nk = A_NK
    row = lax.broadcasted_iota(jnp.int32, (nk, 2 * nk), 0)
    col = lax.broadcasted_iota(jnp.int32, (nk, 2 * nk), 1)
    ok = jnp.logical_and(col >= row, col <= row + nk)
    ok_first = jnp.logical_and(ok, col >= jnp.where(i > 0, 0, nk))

    def load(g, rows, k_prev_ref, v_prev_ref, prev_rows):
        qh = q_refs[g][rows, :].astype(BF16)
        k2 = jnp.concatenate([k_prev_ref[prev_rows, :], kc_refs[g][rows, :]], axis=0).astype(BF16)
        v2 = jnp.concatenate([v_prev_ref[prev_rows, :], vc_refs[g][rows, :]], axis=0).astype(BF16)
        return qh, k2, v2

    def attend(qh, k2, v2, mask):
        s = jnp.where(mask, _dot_nt(qh, k2), NEG)
        m = jnp.max(s, axis=-1, keepdims=True)
        e = jnp.exp2((s - m) * (ATT_SCALE * 1.4426950408889634))
        z = jnp.sum(e, axis=-1, keepdims=True)
        return _dot(e.astype(BF16), v2) / z, m * ATT_SCALE + jnp.log(z)

    unroll = 8

    def rows_of(d, sb, r):
        start = sb * (nk * d) + r
        if d == 1:
            return pl.ds(pl.multiple_of(start, nk), nk)
        return pl.ds(start, nk, stride=d)

    def run(g, items):
        d = A_DILATIONS[g]
        loaded, masks = [], []
        for sb, r in items:
            if isinstance(sb, int) and sb == 0:
                loaded.append(load(g, rows_of(d, 0, r), kp_refs[g], vp_refs[g], rows_of(d, 0, r)))
                masks.append(ok_first)
            else:
                loaded.append(load(g, rows_of(d, sb, r), kc_refs[g], vc_refs[g], rows_of(d, sb - 1, r)))
                masks.append(ok)
        results = [attend(*x, mask) for x, mask in zip(loaded, masks)]
        for (sb, r), (o, lse) in zip(items, results):
            o_scr[g, rows_of(d, sb, r), :] = o
            lse_scr[g, rows_of(d, sb, r), :] = jnp.broadcast_to(lse, (nk, LANES))

    for g in range(ng):
        d = A_DILATIONS[g]
        nblk = tb // nk
        assert nblk % unroll == 0 and (d % unroll == 0 or unroll % d == 0)
        if d >= unroll:
            per_sb = d // unroll
            run_first = lambda it, c, g=g: (run(g, [(0, it * unroll + u) for u in range(unroll)]), c)[1]
            lax.fori_loop(0, per_sb, run_first, 0)
            if nblk > d:
                def later(it, c, g=g, per_sb=per_sb):
                    sb = 1 + it // per_sb
                    r0 = (it % per_sb) * unroll
                    run(g, [(sb, r0 + u) for u in range(unroll)])
                    return c
                lax.fori_loop(0, (nblk - d) // unroll, later, 0)
        else:
            sbs = unroll // d
            run(g, [(sb, r) for sb in range(sbs) for r in range(d)])
            def later(it, c, g=g, d=d, sbs=sbs):
                run(g, [(it * sbs + s, r) for s in range(sbs) for r in range(d)])
                return c
            lax.fori_loop(1, nblk // unroll, later, 0)

    mc = 256

    def merge(c, carry):
        rows = pl.ds(pl.multiple_of(c * mc, mc), mc)
        l0, l1, l2 = lse_scr[0, rows, :], lse_scr[1, rows, :], lse_scr[2, rows, :]
        m = jnp.maximum(jnp.maximum(l0, l1), l2)
        e0, e1, e2 = jnp.exp(l0 - m), jnp.exp(l1 - m), jnp.exp(l2 - m)
        inv = 1.0 / (e0 + e1 + e2)
        oa = (e0 * inv) * o_scr[0, rows, :] + (e1 * inv) * o_scr[1, rows, :] + (e2 * inv) * o_scr[2, rows, :]
        oa_ref[rows, :] = (oa * _silu(za_ref[rows, :])).astype(BF16)
        return carry

    lax.fori_loop(0, tb // mc, merge, 0)


def _attn_prompt(qkv, rest, b, s):
    t = b * s
    nk = A_NK
    tb = nk * A_DILATIONS[-1]
    assert s % tb == 0
    nb = s // tb
    nh = A_GROUPS * A_HEADS

    def cur(kind, g):
        return pl.BlockSpec((None, tb, LANES), lambda bi, h, i: (kind * nh + g * A_HEADS + h, bi * nb + i, 0))

    def prev(kind, g):
        pb = nk * A_DILATIONS[g]
        return pl.BlockSpec((None, pb, LANES),
                            lambda bi, h, i: (kind * nh + g * A_HEADS + h,
                                              jnp.maximum((bi * s + i * tb) // pb - 1, 0), 0))

    groups = range(A_GROUPS)
    in_specs = ([cur(0, g) for g in groups] + [cur(1, g) for g in groups] + [prev(1, g) for g in groups]
                + [cur(2, g) for g in groups] + [prev(2, g) for g in groups]
                + [pl.BlockSpec((tb, LANES), lambda bi, h, i: (bi * nb + i, ZA_OFF // LANES + h))])
    return pl.pallas_call(
        functools.partial(_attn_prompt_kernel, tb=tb),
        grid=(b, A_HEADS, nb),
        in_specs=in_specs,
        out_specs=pl.BlockSpec((tb, LANES), lambda bi, h, i: (bi * nb + i, h)),
        out_shape=jax.ShapeDtypeStruct((t, A_WIDTH), BF16),
        scratch_shapes=[pltpu.VMEM((A_GROUPS, tb, LANES), F32), pltpu.VMEM((A_GROUPS, tb, LANES), F32)],
        compiler_params=_cparams("arbitrary", "arbitrary", "arbitrary"),
        name="attn_p",
    )(*([qkv] * (5 * A_GROUPS)), rest)


def _attn_decode_kernel(qn_ref, kn_ref, vn_ref, za_ref, c0_ref, c1_ref, c2_ref, kv0_any, kv1_any, kv2_any,
                        oa_ref, kv0_ref, kv1_ref, kv2_ref, *, t, chunk):
    del kv0_any, kv1_any, kv2_any
    tile = (A_HEADS, A_HEAD_DIM)
    log2e, ln2 = 1.4426950408889634, 0.6931471805599453

    for g, kv_ref in enumerate((kv0_ref, kv1_ref, kv2_ref)):
        kv_ref[:, 0] = kn_ref[g]
        kv_ref[:, 1] = vn_ref[g]

    def rows_of(g, tq, i0, n):
        d = A_DILATIONS[g]
        if g == A_GROUPS - 1:
            return c2_ref[i0:i0 + n, tq % d, 0], c2_ref[i0:i0 + n, tq % d, 1]
        c_ref = (c0_ref, c1_ref)[g]
        rows = pl.ds(tq % d + i0 * d, n, stride=d) if d > 1 else pl.ds(i0, n)
        return c_ref[rows, 0], c_ref[rows, 1]

    def partial(s, v):
        m = jnp.max(s, axis=0)
        p = jnp.exp2(s - m[None])
        return m, jnp.sum(p, axis=0), jnp.sum(p * v, axis=0)

    def combine(a, b):
        m = jnp.maximum(a[0], b[0])
        fa, fb = jnp.exp2(a[0] - m), jnp.exp2(b[0] - m)
        return m, a[1] * fa + b[1] * fb, a[2] * fa + b[2] * fb

    def attend(g, tq):
        d = A_DILATIONS[g]
        q = qn_ref[g, tq] * (ATT_SCALE * log2e)
        parts = []
        for i0 in range(0, A_NK, chunk):
            k, v = rows_of(g, tq, i0, chunk)
            s = jnp.broadcast_to(jnp.sum(k * q[None], axis=-1, keepdims=True), (chunk,) + tile)
            first_row = tq % d + i0 * d
            if first_row < tq:
                ri = first_row + d * lax.broadcasted_iota(jnp.int32, (chunk,) + tile, 0)
                s = jnp.where(ri >= tq, s, NEG)
            parts.append(partial(s, v))
        js = [tk for tk in range(tq + 1) if (tq - tk) % d == 0]
        s_new = jnp.stack([jnp.broadcast_to(jnp.sum(kn_ref[g, tk] * q, axis=-1, keepdims=True), tile)
                           for tk in js])
        parts.append(partial(s_new, jnp.stack([vn_ref[g, tk] for tk in js])))
        while len(parts) > 1:
            parts = [combine(parts[i], parts[i + 1]) if i + 1 < len(parts) else parts[i]
                     for i in range(0, len(parts), 2)]
        m, l, acc = parts[0]
        return acc / l, m * ln2 + jnp.log(l)

    for tq in range(t):
        outs = [attend(g, tq) for g in range(A_GROUPS)]
        lses = [o[1] for o in outs]
        mm = functools.reduce(jnp.maximum, lses)
        es = [jnp.exp(l - mm) for l in lses]
        inv = 1.0 / sum(es)
        oa = sum((e * inv) * o[0] for e, o in zip(es, outs))
        oa_ref[tq] = oa * _silu(za_ref[tq])


def _attn_decode(qkvn, za5, caches, kv_shifted, depth_i):
    _, ng, b, t, nh, hd = qkvn.shape
    wbs = [c.shape[2] for c in caches]
    d2 = A_DILATIONS[-1]
    assert t == SUBLANES and tuple(wbs) == A_WINDOWS and t <= d2
    assert all(w == A_NK * d for w, d in zip(wbs, A_DILATIONS))
    c2 = caches[-1].reshape(caches[-1].shape[:2] + (wbs[-1] // d2, d2, 2, nh, hd))
    new = lambda kind: pl.BlockSpec((None, ng, None, t, nh, hd), lambda bi: (kind, 0, bi, 0, 0, 0))
    whole = lambda n: pl.BlockSpec((None, None, n, 2, nh, hd), lambda bi: (depth_i, bi, 0, 0, 0, 0))
    tail = lambda n: pl.BlockSpec((None, t, 2, nh, hd), lambda bi: (bi, n // t - 1, 0, 0, 0))
    anyspec = pl.BlockSpec(memory_space=pl.ANY)
    res = pl.pallas_call(
        functools.partial(_attn_decode_kernel, t=t, chunk=32),
        grid=(b,),
        in_specs=[new(0), new(1), new(2),
                  pl.BlockSpec((None, t, nh, hd), lambda bi: (bi, 0, 0, 0)),
                  whole(wbs[0]), whole(wbs[1]),
                  pl.BlockSpec((None, None, wbs[-1] // d2, t, 2, nh, hd), lambda bi: (depth_i, bi, 0, 0, 0, 0, 0)),
                  anyspec, anyspec, anyspec],
        out_specs=[pl.BlockSpec((None, t, nh, hd), lambda bi: (bi, 0, 0, 0))] + [tail(w) for w in wbs],
        out_shape=[jax.ShapeDtypeStruct((b, t, nh, hd), F32)]
        + [jax.ShapeDtypeStruct((b, w, 2, nh, hd), F32) for w in wbs],
        input_output_aliases={7: 1, 8: 2, 9: 3},
        compiler_params=_cparams("arbitrary"),
        name="attn_d",
    )(qkvn, qkvn, qkvn, za5, caches[0], caches[1], c2, *kv_shifted)
    return res[0], res[1:]


def _ssd_kernel(xbc_ref, dt_ref, zb_ref, conv0_ref, h0_ref, convw_ref, convb_ref, dtb_ref, alog_ref,
                dskip_ref, gssm_ref, e_ref, yn_ref, convo_ref, ho_ref, ext_scr, st_scr, *, nv):
    q = SSM_CHUNK
    c = pl.program_id(1)
    di = SSM_D_INNER
    gw = di // SSM_GROUPS
    ns = SSM_D_STATE
    hp = SUBLANES
    kw = SSM_CONV_W

    nslab = SSM_CONV_DIM // LANES
    lanes_of = lambda j: slice(j * LANES, (j + 1) * LANES)

    @pl.when(c == 0)
    def _():
        for j in range(nslab):
            ext_scr[j, 0:hp, :] = conv0_ref[:, lanes_of(j)]
        for j in range(di // LANES):
            st_scr[:, lanes_of(j)] = h0_ref[j].T

    for j in range(nslab):
        ext_scr[j, hp:hp + nv, :] = xbc_ref[:, lanes_of(j)]
        if nv != q:
            ext_scr[j, hp + nv:hp + q, :] = jnp.zeros((q - nv, LANES), F32)

    def conv_act(c0, c1):
        pieces = []
        for j in range(c0 // LANES, c1 // LANES):
            acc = convb_ref[:, lanes_of(j)]
            for k in range(kw):
                acc = acc + convw_ref[k:k + 1, lanes_of(j)] * ext_scr[j, pl.ds(hp - (kw - 1) + k, q), :]
            pieces.append(_silu(acc))
        return pieces[0] if len(pieces) == 1 else jnp.concatenate(pieces, axis=1)

    row = lax.broadcasted_iota(jnp.int32, (q, q), 0)
    col = lax.broadcasted_iota(jnp.int32, (q, q), 1)
    causal = row >= col
    if nv == q:
        dt_raw = dt_ref[...]
    else:
        dt_raw = jnp.concatenate([dt_ref[...], jnp.zeros((q - nv, DT_PAD), F32)], axis=0)
    dt = jax.nn.softplus(dt_raw + dtb_ref[...])
    if nv != q:
        dt = jnp.where(row < nv, dt, 0.0)
    da = dt * (-jnp.exp(alog_ref[...]))
    cs = _dot_01_by_f32(jnp.where(causal, 1.0, 0.0).astype(BF16), da)
    cs_t = cs.T
    dt_parts, cs_parts = _split3(dt), _split3(cs)
    half = lax.broadcasted_iota(jnp.int32, (q, LANES), 1) < SSM_HEAD_DIM

    for g in range(SSM_GROUPS):
        gs = slice(g * gw, (g + 1) * gw)
        e_g = e_ref[:, gs]
        dt_x = sum(_dot(p, e_g) for p in dt_parts)
        cs_x = sum(_dot(p, e_g) for p in cs_parts)
        cs_last = cs_x[q - 1:q, :]
        xs = conv_act(g * gw, (g + 1) * gw)
        xdt = xs * dt_x
        xdt_b = xdt.astype(BF16)
        w_b = (xdt * jnp.exp(cs_last - cs_x)).astype(BF16)
        bg_t = conv_act(di + g * ns, di + (g + 1) * ns).T.astype(BF16)
        cg = conv_act(di + SSM_GROUPS * ns + g * ns, di + SSM_GROUPS * ns + (g + 1) * ns).astype(BF16)
        cb = _dot(cg, bg_t)
        st = st_scr[:, gs]
        y_off = _dot(cg, st.astype(BF16)) * jnp.exp(cs_x)
        st_scr[:, gs] = st * jnp.exp(cs_last) + _dot(bg_t, w_b)
        y_pairs = []
        for pair in range(gw // LANES):
            h0 = (g * gw + pair * LANES) // SSM_HEAD_DIM
            ps = slice(pair * LANES, (pair + 1) * LANES)
            ys = []
            for h in (h0, h0 + 1):
                diff = cs[:, h:h + 1] - cs_t[h:h + 1, :]
                m_h = (cb * jnp.exp(jnp.where(causal, diff, NEG))).astype(BF16)
                ys.append(_dot(m_h, xdt_b[:, ps]))
            y_pairs.append(jnp.where(half, ys[0], ys[1]))
        y = jnp.concatenate(y_pairs, axis=1) + y_off + xs * dskip_ref[:, gs]
        v = y[0:nv, :] * _silu(zb_ref[:, gs])
        vn = v * lax.rsqrt(jnp.mean(v * v, axis=-1, keepdims=True) + NORM_EPS)
        yn_ref[:, gs] = (vn * gssm_ref[:, gs]).astype(BF16)

    @pl.when(c == pl.num_programs(1) - 1)
    def _():
        for j in range(di // LANES):
            ho_ref[j] = st_scr[:, j * LANES:(j + 1) * LANES].T

    for j in range(nslab):
        tail = ext_scr[j, hp + nv - (kw - 1):hp + nv, :]
        convo_ref[:, lanes_of(j)] = tail
        ext_scr[j, hp - (kw - 1):hp, :] = tail


def _ssd(rest3, dt3, conv0, h0, conv_w, conv_b, dt_bias, a_log, d_skip, g_ssm, nv):
    b, l, _ = rest3.shape
    q = SSM_CHUNK
    nc = l // nv
    assert nv == q or nc == 1
    di, cd, nh = SSM_D_INNER, SSM_CONV_DIM, SSM_HEADS
    hp = SUBLANES
    conv0p = jnp.pad(conv0, ((0, 0), (hp - (SSM_CONV_W - 1), 0), (0, 0)))
    h0v = h0.reshape(b, di // LANES, LANES, SSM_D_STATE)
    pad1 = lambda v: jnp.pad(v.reshape(1, nh), ((0, 0), (0, DT_PAD - nh)))
    expand = np.zeros((DT_PAD, di), np.float32)
    expand[np.arange(di) // SSM_HEAD_DIM, np.arange(di)] = 1.0
    const = lambda shape: pl.BlockSpec(shape, lambda bi, c: (0,) * len(shape))
    yn, conv_o, h_o = pl.pallas_call(
        functools.partial(_ssd_kernel, nv=nv),
        grid=(b, nc),
        in_specs=[pl.BlockSpec((None, nv, cd), lambda bi, c: (bi, c, XBC_OFF // cd)),
                  pl.BlockSpec((None, nv, DT_PAD), lambda bi, c: (bi, c, 0)),
                  pl.BlockSpec((None, nv, di), lambda bi, c: (bi, c, ZB_OFF // di)),
                  pl.BlockSpec((None, hp, cd), lambda bi, c: (bi, 0, 0)),
                  pl.BlockSpec((None, di // LANES, LANES, SSM_D_STATE), lambda bi, c: (bi, 0, 0, 0)),
                  const((SSM_CONV_W, cd)), const((1, cd)), const((1, DT_PAD)), const((1, DT_PAD)),
                  const((1, di)), const((1, di)), const((DT_PAD, di))],
        out_specs=[pl.BlockSpec((None, nv, di), lambda bi, c: (bi, c, 0)),
                   pl.BlockSpec((None, SSM_CONV_W - 1, cd), lambda bi, c: (bi, 0, 0)),
                   pl.BlockSpec((None, di // LANES, LANES, SSM_D_STATE), lambda bi, c: (bi, 0, 0, 0))],
        out_shape=[jax.ShapeDtypeStruct((b, l, di), BF16),
                   jax.ShapeDtypeStruct((b, SSM_CONV_W - 1, cd), F32),
                   jax.ShapeDtypeStruct((b, di // LANES, LANES, SSM_D_STATE), F32)],
        scratch_shapes=[pltpu.VMEM((cd // LANES, hp + q, LANES), F32),
                        pltpu.VMEM((SSM_D_STATE, di), F32)],
        compiler_params=_cparams("arbitrary", "arbitrary"),
        name=f"ssd{nv}",
    )(rest3, dt3, rest3, conv0p, h0v, conv_w, conv_b.reshape(1, cd), pad1(dt_bias), pad1(a_log),
      jnp.repeat(d_skip, SSM_HEAD_DIM).reshape(1, di), g_ssm.reshape(1, di), jnp.asarray(expand, BF16))
    return yn.reshape(b * l, di), conv_o, h_o.reshape(b, nh, SSM_HEAD_DIM, SSM_D_STATE)


def _mix_kernel(oa_ref, yn_ref, ra_ref, rb_ref, woa_ref, wob_ref, mg_ref):
    p_a = _dot(oa_ref[...], woa_ref[...])
    p_b = _dot(yn_ref[...], wob_ref[...])
    mg_ref[...] = (jax.nn.sigmoid(ra_ref[...]) * p_a + jax.nn.sigmoid(rb_ref[...]) * p_b).astype(BF16)


def _mix(oa, yn, rest, w_o_a, w_o_b, tm):
    t = rest.shape[0]
    d = D_MODEL
    row = lambda w, off: pl.BlockSpec((tm, w), lambda i: (i, off // w))
    const = lambda shape: pl.BlockSpec(shape, lambda i: (0, 0))
    return pl.pallas_call(
        _mix_kernel,
        grid=(t // tm,),
        in_specs=[row(A_WIDTH, 0), row(d, 0), row(d, RA_OFF), row(d, RB_OFF),
                  const((A_WIDTH, d)), const((d, d))],
        out_specs=row(d, 0),
        out_shape=jax.ShapeDtypeStruct((t, d), BF16),
        compiler_params=_cparams("arbitrary"),
        name="mix",
    )(oa, yn, rest, rest, w_o_a, w_o_b)


def _out_kernel(mg_ref, x_ref, gate_ref, gpost_ref, wo_ref, y_ref):
    out = _dot(mg_ref[...], wo_ref[...])
    nrm = out * lax.rsqrt(jnp.mean(out * out, axis=-1, keepdims=True) + NORM_EPS) * gpost_ref[...]
    y_ref[...] = x_ref[...] + gate_ref[...] * nrm


def _out(mg, x2d, gate3, g_post, w_o, tm, rows_per_mod):
    t, d = x2d.shape
    r = gate3.shape[1]
    return pl.pallas_call(
        _out_kernel,
        grid=(t // tm,),
        in_specs=[pl.BlockSpec((tm, d), lambda i: (i, 0)),
                  pl.BlockSpec((tm, d), lambda i: (i, 0)),
                  pl.BlockSpec((None, r, d), lambda i: ((i * tm) // rows_per_mod, 0, 0)),
                  pl.BlockSpec((1, d), lambda i: (0, 0)),
                  pl.BlockSpec((d, d), lambda i: (0, 0))],
        out_specs=pl.BlockSpec((tm, d), lambda i: (i, 0)),
        out_shape=jax.ShapeDtypeStruct((t, d), F32),
        compiler_params=_cparams("arbitrary"),
        name="out",
    )(mg, x2d, gate3, g_post.reshape(1, d), w_o)


def _window_kernel(k_ref, v_ref, o_ref, *, rows):
    per_row = 2 * A_HEADS
    for kind, src in enumerate((k_ref, v_ref)):
        for h in range(A_HEADS):
            o_ref[pl.ds(kind * A_HEADS + h, rows, stride=per_row), :] = src[h]


def _window_rows(qkv, g, b, s, w):
    nh = A_GROUPS * A_HEADS
    rows = min(w, 256)
    per_row = 2 * A_HEADS
    assert w % rows == 0 and (s - w) % rows == 0
    src = lambda kind: pl.BlockSpec(
        (A_HEADS, rows, A_HEAD_DIM),
        lambda bi, i: (kind * A_GROUPS + g, (bi * s + s - w) // rows + i, 0))
    out = pl.pallas_call(
        functools.partial(_window_kernel, rows=rows),
        grid=(b, w // rows),
        in_specs=[src(1), src(2)],
        out_specs=pl.BlockSpec((None, rows * per_row, A_HEAD_DIM), lambda bi, i: (bi, i, 0)),
        out_shape=jax.ShapeDtypeStruct((b, w * per_row, A_HEAD_DIM), F32),
        compiler_params=_cparams("arbitrary", "arbitrary"),
        name=f"window{g}",
    )(qkv, qkv)
    return out.reshape(b, w, 2, A_HEADS, A_HEAD_DIM)


def _prep_w_kernel(tbl_ref, a_ref, b_ref, dt_ref, w_ref, wdt_ref):
    j = pl.program_id(0)
    d = a_ref.shape[1]
    late = tbl_ref[1, j]
    t = LANES

    def put(r, tile_rows):
        for c in range(d // t):
            w_ref[c * t:(c + 1) * t, r * t:(r + 1) * t] = tile_rows[:, c * t:(c + 1) * t].T.astype(BF16)

    @pl.when(late == 0)
    def _():
        for r in range(COL // t):
            put(r, a_ref[r * t:(r + 1) * t, :])

    @pl.when(late != 0)
    def _():
        for r in range(COL // t - 1):
            put(r, a_ref[r * t + SSM_HEADS:(r + 1) * t + SSM_HEADS, :])
        put(COL // t - 1, jnp.concatenate([a_ref[COL - t + SSM_HEADS:COL, :], b_ref[...]], axis=0))

    @pl.when(j == 0)
    def _():
        lane = lax.broadcasted_iota(jnp.int32, (t, t), 1)
        for c in range(d // t):
            blk = dt_ref[:, c * t:(c + 1) * t].T
            wdt_ref[c * t:(c + 1) * t, :] = jnp.where(lane < SSM_HEADS, blk, 0.0).astype(BF16)


def _prep_w(w_t):
    d = w_t.shape[1]
    assert W_DT % COL == 0 and W_RB - W_RA == D_MODEL and DT_PAD == LANES and SSM_HEADS % SUBLANES == 0
    blk = lambda off, n: [off // COL + k for k in range(n // COL)]
    plain = blk(0, N_QKV_COLS) + blk(W_XBC, SSM_CONV_DIM) + blk(W_ZB, SSM_D_INNER)
    late = blk(W_DT, 2 * D_MODEL)
    za = blk(W_ZA, A_WIDTH)
    src = plain + late + za
    is_late = [0] * len(plain) + [1] * len(late) + [0] * len(za)
    nxt = [0] * len(plain) + [(k + 1) * COL // SSM_HEADS for k in late] + [0] * len(za)
    tbl = jnp.asarray(np.array([src, is_late, nxt], np.int32))
    n = len(src)
    grid_spec = pltpu.PrefetchScalarGridSpec(
        num_scalar_prefetch=1,
        grid=(n,),
        in_specs=[pl.BlockSpec((COL, d), lambda j, t: (t[0, j], 0)),
                  pl.BlockSpec((SSM_HEADS, d), lambda j, t: (t[2, j], 0)),
                  pl.BlockSpec((DT_PAD, d), lambda j, t: (W_DT // DT_PAD, 0))],
        out_specs=[pl.BlockSpec((d, COL), lambda j, t: (0, j)),
                   pl.BlockSpec((d, DT_PAD), lambda j, t: (0, 0))])
    return pl.pallas_call(
        _prep_w_kernel,
        grid_spec=grid_spec,
        out_shape=[jax.ShapeDtypeStruct((d, n * COL), BF16), jax.ShapeDtypeStruct((d, DT_PAD), BF16)],
        compiler_params=_cparams("arbitrary"),
        name="prep_w",
    )(tbl, w_t, w_t, w_t)


def _layer(depth_i, x_p, x_s, caches, conv_s, ssm_s, c_p, c_s, w_ada, b_ada, g_pre, g_post, w_in, conv_w,
           conv_b, dt_bias, a_log, d_skip, g_ssm, w_o_a, w_o_b, w_o):
    bp, sp, d = x_p.shape
    bs, ts, _ = x_s.shape
    assert sp % SSM_CHUNK == 0

    w_main, w_dt = _prep_w(jnp.swapaxes(w_in, 0, 1))
    woa_b, wob_b, wo_b = w_o_a.astype(BF16), w_o_b.astype(BF16), w_o.astype(BF16)

    nmod = bp + bs
    mpad = -(-nmod // SUBLANES) * SUBLANES
    c_all = jnp.pad(jnp.concatenate([c_p, c_s], axis=0), ((0, mpad - nmod), (0, 0)))
    mod = _ada(c_all, w_ada, b_ada)
    shift, scale, gate = mod[:, :d], mod[:, d:2 * d], mod[:, 2 * d:]
    per_seq = lambda v: v[:bp].reshape(bp, 1, d)
    per_row = lambda v: jnp.repeat(v[bp:nmod], ts, axis=0).reshape(1, bs * ts, d)

    tp = bp * sp
    xp2 = x_p.reshape(tp, d)
    qkv_p, rest_p, dt_p, *kv_shifted = _in_proj(xp2, per_seq(scale), per_seq(shift), g_pre, w_main, w_dt, 1024, sp,
                                                caches, depth_i, ts)
    oa_p = _attn_prompt(qkv_p, rest_p, bp, sp)
    conv0 = jnp.zeros((bp, SSM_CONV_W - 1, SSM_CONV_DIM), F32)
    ssm0 = jnp.zeros((bp, SSM_HEADS, SSM_HEAD_DIM, SSM_D_STATE), F32)
    yn_p, conv_p, ssm_p = _ssd(rest_p.reshape(bp, sp, N_REST), dt_p.reshape(bp, sp, DT_PAD), conv0, ssm0,
                               conv_w, conv_b, dt_bias, a_log, d_skip, g_ssm, SSM_CHUNK)
    mg_p = _mix(oa_p, yn_p, rest_p, woa_b, wob_b, 256)
    y_p = _out(mg_p, xp2, per_seq(gate), g_post, wo_b, 512, sp).reshape(bp, sp, d)
    kv_p = [_window_rows(qkv_p, g, bp, sp, min(A_WINDOWS[g], sp)) for g in range(A_GROUPS)]

    tsn = bs * ts
    xs2 = x_s.reshape(tsn, d)
    qkv_s, rest_s, dt_s = _in_proj(xs2, per_row(scale), per_row(shift), g_pre, w_main, w_dt, tsn, tsn)[:3]
    qkvn = jnp.transpose(qkv_s.reshape(3, A_GROUPS, A_HEADS, bs, ts, A_HEAD_DIM), (0, 1, 3, 4, 2, 5))
    za5 = rest_s[:, ZA_OFF:ZA_OFF + A_WIDTH].reshape(bs, ts, A_HEADS, A_HEAD_DIM)
    oa5, kv_s = _attn_decode(qkvn, za5, caches, kv_shifted, depth_i)
    oa_s = oa5.reshape(tsn, A_WIDTH).astype(BF16)
    yn_s, conv_sn, ssm_sn = _ssd(rest_s.reshape(bs, ts, N_REST), dt_s.reshape(bs, ts, DT_PAD), conv_s, ssm_s,
                                 conv_w, conv_b, dt_bias, a_log, d_skip, g_ssm, ts)
    mg_s = _mix(oa_s, yn_s, rest_s, woa_b, wob_b, tsn)
    y_s = _out(mg_s, xs2, per_row(gate), g_post, wo_b, tsn, tsn).reshape(bs, ts, d)
    return y_p, y_s, kv_p, conv_p, ssm_p, list(kv_s), conv_sn, ssm_sn


def kernel(x_prompt, x_sample, cache_a_w128, cache_a_w512, cache_a_w2048, state_conv, state_ssm, c_prompt, c_sample, w_ada, b_ada, g_pre, g_post, w_in, conv_w, conv_b, dt_bias, a_log, d_skip, g_ssm, w_o_a, w_o_b, w_o):
    depth = w_in.shape[0]
    caches = (cache_a_w128, cache_a_w512, cache_a_w2048)
    y_p, y_s = x_prompt, x_sample
    acc = [[] for _ in range(10)]
    for i in range(depth):
        y_p, y_s, kv_p, conv_p, ssm_p, kv_s, conv_s, ssm_s = _layer(
            i, y_p, y_s, caches, state_conv[i], state_ssm[i],
            c_prompt, c_sample, w_ada[i], b_ada[i], g_pre[i], g_post[i], w_in[i], conv_w[i], conv_b[i],
            dt_bias[i], a_log[i], d_skip[i], g_ssm[i], w_o_a[i], w_o_b[i], w_o[i])
        for lst, v in zip(acc, (*kv_p, conv_p, ssm_p, *kv_s, conv_s, ssm_s)):
            lst.append(v)
    return (y_p, y_s, *[jnp.stack(v) for v in acc])
```

```python
import functools

import numpy as np
import jax
import jax.numpy as jnp
from jax import lax
from jax.experimental import pallas as pl
from jax.experimental.pallas import tpu as pltpu

F32 = jnp.float32
BF16 = jnp.bfloat16

D_MODEL = 2048
A_WINDOWS = (128, 512, 2048)
A_DILATIONS = (1, 4, 16)
A_GROUPS = 3
A_HEADS = 8
A_HEAD_DIM = 128
A_WIDTH = A_HEADS * A_HEAD_DIM
A_NK = 128
ATT_SCALE = A_HEAD_DIM ** -0.5
SSM_D_INNER = D_MODEL
SSM_HEAD_DIM = 64
SSM_HEADS = SSM_D_INNER // SSM_HEAD_DIM
SSM_GROUPS = 8
SSM_D_STATE = 128
SSM_CONV_W = 4
SSM_CHUNK = 128
SSM_CONV_DIM = SSM_D_INNER + 2 * SSM_GROUPS * SSM_D_STATE
QKV_WIDTH = A_GROUPS * A_WIDTH
NORM_EPS = 1e-6

LANES = 128
SUBLANES = 8
VMEM_LIMIT = 60 * 1024 * 1024
NEG = -1e30

COL = 1024
N_QKV_COLS = 3 * QKV_WIDTH
N_SLABS = N_QKV_COLS // LANES
XBC_OFF = 0
ZB_OFF = XBC_OFF + SSM_CONV_DIM
RA_OFF = ZB_OFF + SSM_D_INNER
RB_OFF = RA_OFF + D_MODEL
ZA_OFF = RB_OFF + D_MODEL
N_REST = ZA_OFF + A_WIDTH
DT_PAD = LANES
W_ZA = N_QKV_COLS
W_ZB = W_ZA + A_WIDTH
W_XBC = W_ZB + SSM_D_INNER
W_DT = W_XBC + SSM_CONV_DIM
W_RA = W_DT + SSM_HEADS
W_RB = W_RA + D_MODEL


def _cparams(*sem):
    return pltpu.CompilerParams(dimension_semantics=sem, vmem_limit_bytes=VMEM_LIMIT)


def _dot(a, b):
    return jnp.dot(a, b, preferred_element_type=F32)


def _dot_nt(a, b):
    return lax.dot_general(a, b, (((1,), (1,)), ((), ())), preferred_element_type=F32)


def _split3(x):
    x1 = x.astype(BF16)
    r1 = x - x1.astype(F32)
    x2 = r1.astype(BF16)
    x3 = (r1 - x2.astype(F32)).astype(BF16)
    return x1, x2, x3


def _dot_f32_by_01(x, e):
    x1, x2, x3 = _split3(x)
    return _dot(x1, e) + _dot(x2, e) + _dot(x3, e)


def _dot_01_by_f32(e, x):
    x1, x2, x3 = _split3(x)
    return _dot(e, x1) + _dot(e, x2) + _dot(e, x3)


def _silu(x):
    h = 0.5 * x
    return h + h * jnp.tanh(h)


def _ada_kernel(c_ref, w_ref, b_ref, o_ref):
    s = _silu(c_ref[...]).astype(BF16)
    o_ref[...] = _dot(s, w_ref[...].astype(BF16)) + b_ref[...]


def _ada(c, w_ada, b_ada):
    m, d = c.shape
    n = w_ada.shape[1]
    tn = 512
    return pl.pallas_call(
        _ada_kernel,
        grid=(n // tn,),
        in_specs=[pl.BlockSpec((m, d), lambda j: (0, 0)),
                  pl.BlockSpec((d, tn), lambda j: (0, j)),
                  pl.BlockSpec((1, tn), lambda j: (0, j))],
        out_specs=pl.BlockSpec((m, tn), lambda j: (0, j)),
        out_shape=jax.ShapeDtypeStruct((m, n), F32),
        compiler_params=_cparams("arbitrary"),
        name="ada",
    )(c, w_ada, b_ada.reshape(1, n))


SHIFT_CHUNK_ROWS = 384
SHIFT_DMA_PRIORITY = 1


def _shift_plan(caches, t):
    plan, first = [], 0
    for c in caches:
        nb, n = c.shape[1], c.shape[2] - t
        cps = next(k for k in range(-(-n // SHIFT_CHUNK_ROWS), n + 1) if n % k == 0)
        plan.append((n // cps, cps, nb, first))
        first += nb * cps
    return plan, first


def _shift_step(step, c_refs, kv_refs, buf, in_sem, out_sem, tail_sem, plan, depth_i, t):
    def where(g, c):
        rows, cps, _, first = plan[g]
        return c // cps, c % cps, (c + first % 2) % 2, rows, cps

    def read(g, c):
        b, ci, slot, rows, _ = where(g, c)
        return pltpu.make_async_copy(c_refs[g].at[depth_i, b, pl.ds(t + ci * rows, rows)],
                                     buf.at[slot, pl.ds(0, rows)], in_sem.at[slot])

    def write(g, c):
        b, ci, slot, rows, _ = where(g, c)
        return pltpu.make_async_copy(buf.at[slot, pl.ds(0, rows)], kv_refs[g].at[b, pl.ds(ci * rows, rows)],
                                     out_sem.at[slot])

    def write_tail(g, c):
        b, _, slot, rows, cps = where(g, c)
        return pltpu.make_async_copy(buf.at[slot, pl.ds(rows - t, t)], kv_refs[g].at[b, pl.ds(cps * rows, t)],
                                     tail_sem.at[slot])

    def ends_sequence(g, c):
        return c % plan[g][1] == plan[g][1] - 1

    def in_group(g, k):
        first, n = plan[g][3], plan[g][2] * plan[g][1]
        return jnp.logical_and(k >= first, k < first + n)

    for g in range(len(plan)):
        @pl.when(in_group(g, step - 2))
        def _(g=g):
            c = step - 2 - plan[g][3]
            write(g, c).wait()
            pl.when(ends_sequence(g, c))(lambda: write_tail(g, c).wait())

    for g in range(len(plan)):
        @pl.when(in_group(g, step))
        def _(g=g):
            read(g, step - plan[g][3]).start(priority=SHIFT_DMA_PRIORITY)

    for g in range(len(plan)):
        @pl.when(in_group(g, step - 1))
        def _(g=g):
            c = step - 1 - plan[g][3]
            read(g, c).wait()
            write(g, c).start(priority=SHIFT_DMA_PRIORITY)
            pl.when(ends_sequence(g, c))(lambda: write_tail(g, c).start(priority=SHIFT_DMA_PRIORITY))


def _in_proj_kernel(*refs, tm, rc, n_qkv_steps, plan, depth_i, t_new):
    n_copy = len(plan)
    x_ref, sc_ref, sh_ref, g_ref, w_ref, wdt_ref = refs[:6]
    c_refs = refs[6:6 + n_copy]
    qkv_ref, rest_ref, dt_ref = refs[6 + n_copy:9 + n_copy]
    kv_refs = refs[9 + n_copy:9 + 2 * n_copy]
    h_scr = refs[9 + 2 * n_copy]
    j = pl.program_id(1)

    if n_copy:
        buf, in_sem, out_sem, tail_sem = refs[10 + 2 * n_copy:]
        step = pl.program_id(0) * pl.num_programs(1) + j
        _shift_step(step, c_refs, kv_refs, buf, in_sem, out_sem, tail_sem, plan, depth_i, t_new)

    @pl.when(j == 0)
    def _():
        per_row = sc_ref.shape[0] != 1

        def body(r, carry):
            rows = pl.ds(pl.multiple_of(r * rc, rc), rc)
            x = x_ref[rows, :]
            y = x * lax.rsqrt(jnp.mean(x * x, axis=-1, keepdims=True) + NORM_EPS) * g_ref[...]
            sc = sc_ref[rows, :] if per_row else sc_ref[...]
            sh = sh_ref[rows, :] if per_row else sh_ref[...]
            h_scr[rows, :] = (y * (1.0 + sc) + sh).astype(BF16)
            return carry

        lax.fori_loop(0, tm // rc, body, 0)
        dt_ref[...] = _dot(h_scr[...], wdt_ref[...])

    res = _dot(h_scr[...], w_ref[...])

    @pl.when(j < n_qkv_steps)
    def _():
        for c in range(COL // LANES):
            qkv_ref[c] = res[:, c * LANES:(c + 1) * LANES]

    @pl.when(j >= n_qkv_steps)
    def _():
        rest_ref[...] = res


def _in_proj(x2d, sc3, sh3, g_pre, w_main, w_dt, tm, rows_per_mod, caches=(), depth_i=0, t_new=0):
    t, d = x2d.shape
    r = sc3.shape[1]
    tn = COL
    rc = min(tm, 256)
    nq = N_QKV_COLS // tn
    spb = tn // LANES
    grid = (t // tm, (N_QKV_COLS + N_REST) // tn)
    n_copy = len(caches)
    plan, n_chunks = _shift_plan(caches, t_new)
    assert n_chunks + 2 <= grid[0] * grid[1]
    shift_scratch = []
    if n_copy:
        stage_rows = max(p[0] for p in plan)
        shift_scratch = [pltpu.VMEM((2, stage_rows) + caches[0].shape[3:], caches[0].dtype)] + [
            pltpu.SemaphoreType.DMA((2,))] * 3
    anyspec = pl.BlockSpec(memory_space=pl.ANY)
    mod_map = lambda i, j: ((i * tm) // rows_per_mod, 0, 0)
    return pl.pallas_call(
        functools.partial(_in_proj_kernel, tm=tm, rc=rc, n_qkv_steps=nq, plan=tuple(plan), depth_i=depth_i,
                          t_new=t_new),
        grid=grid,
        in_specs=[pl.BlockSpec((tm, d), lambda i, j: (i, 0)),
                  pl.BlockSpec((None, r, d), mod_map),
                  pl.BlockSpec((None, r, d), mod_map),
                  pl.BlockSpec((1, d), lambda i, j: (0, 0)),
                  pl.BlockSpec((d, tn), lambda i, j: (0, j)),
                  pl.BlockSpec((d, DT_PAD), lambda i, j: (0, 0))] + [anyspec] * n_copy,
        out_specs=[pl.BlockSpec((spb, tm, LANES), lambda i, j: (jnp.minimum(j, nq - 1), i, 0)),
                   pl.BlockSpec((tm, tn), lambda i, j: (i, jnp.maximum(j - nq, 0))),
                   pl.BlockSpec((tm, DT_PAD), lambda i, j: (i, 0))] + [anyspec] * n_copy,
        out_shape=[jax.ShapeDtypeStruct((N_SLABS, t, LANES), F32),
                   jax.ShapeDtypeStruct((t, N_REST), F32),
                   jax.ShapeDtypeStruct((t, DT_PAD), F32)]
        + [jax.ShapeDtypeStruct(c.shape[1:], c.dtype) for c in caches],
        scratch_shapes=[pltpu.VMEM((tm, d), BF16)] + shift_scratch,
        compiler_params=_cparams("arbitrary", "arbitrary"),
        name="in_proj",
    )(x2d, sc3, sh3, g_pre.reshape(1, d), w_main, w_dt, *caches)


def _attn_prompt_kernel(*refs, tb):
    ng = A_GROUPS
    q_refs, kc_refs, kp_refs = refs[0:ng], refs[ng:2 * ng], refs[2 * ng:3 * ng]
    vc_refs, vp_refs = refs[3 * ng:4 * ng], refs[4 * ng:5 * ng]
    za_ref, oa_ref, o_scr, lse_scr = refs[5 * ng:]
    i = pl.program_id(2)
    nk = A_NK
    row = lax.broadcasted_iota(jnp.int32, (nk, 2 * nk), 0)
    col = lax.broadcasted_iota(jnp.int32, (nk, 2 * nk), 1)
    ok = jnp.logical_and(col >= row, col <= row + nk)
    ok_first = jnp.logical_and(ok, col >= jnp.where(i > 0, 0, nk))

    def load(g, rows, k_prev_ref, v_prev_ref, prev_rows):
        qh = q_refs[g][rows, :].astype(BF16)
        k2 = jnp.concatenate([k_prev_ref[prev_rows, :], kc_refs[g][rows, :]], axis=0).astype(BF16)
        v2 = jnp.concatenate([v_prev_ref[prev_rows, :], vc_refs[g][rows, :]], axis=0).astype(BF16)
        return qh, k2, v2

    def attend(qh, k2, v2, mask):
        s = jnp.where(mask, _dot_nt(qh, k2), NEG)
        m = jnp.max(s, axis=-1, keepdims=True)
        e = jnp.exp2((s - m) * (ATT_SCALE * 1.4426950408889634))
        z = jnp.sum(e, axis=-1, keepdims=True)
        return _dot(e.astype(BF16), v2) / z, m * ATT_SCALE + jnp.log(z)

    unroll = 8

    def rows_of(d, sb, r):
        start = sb * (nk * d) + r
        if d == 1:
            return pl.ds(pl.multiple_of(start, nk), nk)
        return pl.ds(start, nk, stride=d)

    def run(g, items):
        d = A_DILATIONS[g]
        loaded, masks = [], []
        for sb, r in items:
            if isinstance(sb, int) and sb == 0:
                loaded.append(load(g, rows_of(d, 0, r), kp_refs[g], vp_refs[g], rows_of(d, 0, r)))
                masks.append(ok_first)
            else:
                loaded.append(load(g, rows_of(d, sb, r), kc_refs[g], vc_refs[g], rows_of(d, sb - 1, r)))
                masks.append(ok)
        results = [attend(*x, mask) for x, mask in zip(loaded, masks)]
        for (sb, r), (o, lse) in zip(items, results):
            o_scr[g, rows_of(d, sb, r), :] = o
            lse_scr[g, rows_of(d, sb, r), :] = jnp.broadcast_to(lse, (nk, LANES))

    for g in range(ng):
        d = A_DILATIONS[g]
        nblk = tb // nk
        assert nblk % unroll == 0 and (d % unroll == 0 or unroll % d == 0)
        if d >= unroll:
            per_sb = d // unroll
            run_first = lambda it, c, g=g: (run(g, [(0, it * unroll + u) for u in range(unroll)]), c)[1]
            lax.fori_loop(0, per_sb, run_first, 0)
            if nblk > d:
                def later(it, c, g=g, per_sb=per_sb):
                    sb = 1 + it // per_sb
                    r0 = (it % per_sb) * unroll
                    run(g, [(sb, r0 + u) for u in range(unroll)])
                    return c
                lax.fori_loop(0, (nblk - d) // unroll, later, 0)
        else:
            sbs = unroll // d
            run(g, [(sb, r) for sb in range(sbs) for r in range(d)])
            def later(it, c, g=g, d=d, sbs=sbs):
                run(g, [(it * sbs + s, r) for s in range(sbs) for r in range(d)])
                return c
            lax.fori_loop(1, nblk // unroll, later, 0)

    mc = 256

    def merge(c, carry):
        rows = pl.ds(pl.multiple_of(c * mc, mc), mc)
        l0, l1, l2 = lse_scr[0, rows, :], lse_scr[1, rows, :], lse_scr[2, rows, :]
        m = jnp.maximum(jnp.maximum(l0, l1), l2)
        e0, e1, e2 = jnp.exp(l0 - m), jnp.exp(l1 - m), jnp.exp(l2 - m)
        inv = 1.0 / (e0 + e1 + e2)
        oa = (e0 * inv) * o_scr[0, rows, :] + (e1 * inv) * o_scr[1, rows, :] + (e2 * inv) * o_scr[2, rows, :]
        oa_ref[rows, :] = (oa * _silu(za_ref[rows, :])).astype(BF16)
        return carry

    lax.fori_loop(0, tb // mc, merge, 0)


def _attn_prompt(qkv, rest, b, s):
    t = b * s
    nk = A_NK
    tb = nk * A_DILATIONS[-1]
    assert s % tb == 0
    nb = s // tb
    nh = A_GROUPS * A_HEADS

    def cur(kind, g):
        return pl.BlockSpec((None, tb, LANES), lambda bi, h, i: (kind * nh + g * A_HEADS + h, bi * nb + i, 0))

    def prev(kind, g):
        pb = nk * A_DILATIONS[g]
        return pl.BlockSpec((None, pb, LANES),
                            lambda bi, h, i: (kind * nh + g * A_HEADS + h,
                                              jnp.maximum((bi * s + i * tb) // pb - 1, 0), 0))

    groups = range(A_GROUPS)
    in_specs = ([cur(0, g) for g in groups] + [cur(1, g) for g in groups] + [prev(1, g) for g in groups]
                + [cur(2, g) for g in groups] + [prev(2, g) for g in groups]
                + [pl.BlockSpec((tb, LANES), lambda bi, h, i: (bi * nb + i, ZA_OFF // LANES + h))])
    return pl.pallas_call(
        functools.partial(_attn_prompt_kernel, tb=tb),
        grid=(b, A_HEADS, nb),
        in_specs=in_specs,
        out_specs=pl.BlockSpec((tb, LANES), lambda bi, h, i: (bi * nb + i, h)),
        out_shape=jax.ShapeDtypeStruct((t, A_WIDTH), BF16),
        scratch_shapes=[pltpu.VMEM((A_GROUPS, tb, LANES), F32), pltpu.VMEM((A_GROUPS, tb, LANES), F32)],
        compiler_params=_cparams("arbitrary", "arbitrary", "arbitrary"),
        name="attn_p",
    )(*([qkv] * (5 * A_GROUPS)), rest)


def _attn_decode_kernel(qn_ref, kn_ref, vn_ref, za_ref, c0_ref, c1_ref, c2_ref, kv0_any, kv1_any, kv2_any,
                        oa_ref, kv0_ref, kv1_ref, kv2_ref, *, t, chunk):
    del kv0_any, kv1_any, kv2_any
    tile = (A_HEADS, A_HEAD_DIM)
    log2e, ln2 = 1.4426950408889634, 0.6931471805599453

    for g, kv_ref in enumerate((kv0_ref, kv1_ref, kv2_ref)):
        kv_ref[:, 0] = kn_ref[g]
        kv_ref[:, 1] = vn_ref[g]

    def rows_of(g, tq, i0, n):
        d = A_DILATIONS[g]
        if g == A_GROUPS - 1:
            return c2_ref[i0:i0 + n, tq % d, 0], c2_ref[i0:i0 + n, tq % d, 1]
        c_ref = (c0_ref, c1_ref)[g]
        rows = pl.ds(tq % d + i0 * d, n, stride=d) if d > 1 else pl.ds(i0, n)
        return c_ref[rows, 0], c_ref[rows, 1]

    def partial(s, v):
        m = jnp.max(s, axis=0)
        p = jnp.exp2(s - m[None])
        return m, jnp.sum(p, axis=0), jnp.sum(p * v, axis=0)

    def combine(a, b):
        m = jnp.maximum(a[0], b[0])
        fa, fb = jnp.exp2(a[0] - m), jnp.exp2(b[0] - m)
        return m, a[1] * fa + b[1] * fb, a[2] * fa + b[2] * fb

    def attend(g, tq):
        d = A_DILATIONS[g]
        q = qn_ref[g, tq] * (ATT_SCALE * log2e)
        parts = []
        for i0 in range(0, A_NK, chunk):
            k, v = rows_of(g, tq, i0, chunk)
            s = jnp.broadcast_to(jnp.sum(k * q[None], axis=-1, keepdims=True), (chunk,) + tile)
            first_row = tq % d + i0 * d
            if first_row < tq:
                ri = first_row + d * lax.broadcasted_iota(jnp.int32, (chunk,) + tile, 0)
                s = jnp.where(ri >= tq, s, NEG)
            parts.append(partial(s, v))
        js = [tk for tk in range(tq + 1) if (tq - tk) % d == 0]
        s_new = jnp.stack([jnp.broadcast_to(jnp.sum(kn_ref[g, tk] * q, axis=-1, keepdims=True), tile)
                           for tk in js])
        parts.append(partial(s_new, jnp.stack([vn_ref[g, tk] for tk in js])))
        while len(parts) > 1:
            parts = [combine(parts[i], parts[i + 1]) if i + 1 < len(parts) else parts[i]
                     for i in range(0, len(parts), 2)]
        m, l, acc = parts[0]
        return acc / l, m * ln2 + jnp.log(l)

    for tq in range(t):
        outs = [attend(g, tq) for g in range(A_GROUPS)]
        lses = [o[1] for o in outs]
        mm = functools.reduce(jnp.maximum, lses)
        es = [jnp.exp(l - mm) for l in lses]
        inv = 1.0 / sum(es)
        oa = sum((e * inv) * o[0] for e, o in zip(es, outs))
        oa_ref[tq] = oa * _silu(za_ref[tq])


def _attn_decode(qkvn, za5, caches, kv_shifted, depth_i):
    _, ng, b, t, nh, hd = qkvn.shape
    wbs = [c.shape[2] for c in caches]
    d2 = A_DILATIONS[-1]
    assert t == SUBLANES and tuple(wbs) == A_WINDOWS and t <= d2
    assert all(w == A_NK * d for w, d in zip(wbs, A_DILATIONS))
    c2 = caches[-1].reshape(caches[-1].shape[:2] + (wbs[-1] // d2, d2, 2, nh, hd))
    new = lambda kind: pl.BlockSpec((None, ng, None, t, nh, hd), lambda bi: (kind, 0, bi, 0, 0, 0))
    whole = lambda n: pl.BlockSpec((None, None, n, 2, nh, hd), lambda bi: (depth_i, bi, 0, 0, 0, 0))
    tail = lambda n: pl.BlockSpec((None, t, 2, nh, hd), lambda bi: (bi, n // t - 1, 0, 0, 0))
    anyspec = pl.BlockSpec(memory_space=pl.ANY)
    res = pl.pallas_call(
        functools.partial(_attn_decode_kernel, t=t, chunk=32),
        grid=(b,),
        in_specs=[new(0), new(1), new(2),
                  pl.BlockSpec((None, t, nh, hd), lambda bi: (bi, 0, 0, 0)),
                  whole(wbs[0]), whole(wbs[1]),
                  pl.BlockSpec((None, None, wbs[-1] // d2, t, 2, nh, hd), lambda bi: (depth_i, bi, 0, 0, 0, 0, 0)),
                  anyspec, anyspec, anyspec],
        out_specs=[pl.BlockSpec((None, t, nh, hd), lambda bi: (bi, 0, 0, 0))] + [tail(w) for w in wbs],
        out_shape=[jax.ShapeDtypeStruct((b, t, nh, hd), F32)]
        + [jax.ShapeDtypeStruct((b, w, 2, nh, hd), F32) for w in wbs],
        input_output_aliases={7: 1, 8: 2, 9: 3},
        compiler_params=_cparams("arbitrary"),
        name="attn_d",
    )(qkvn, qkvn, qkvn, za5, caches[0], caches[1], c2, *kv_shifted)
    return res[0], res[1:]


def _ssd_kernel(xbc_ref, dt_ref, zb_ref, conv0_ref, h0_ref, convw_ref, convb_ref, dtb_ref, alog_ref,
                dskip_ref, gssm_ref, e_ref, yn_ref, convo_ref, ho_ref, ext_scr, st_scr, *, nv):
    q = SSM_CHUNK
    c = pl.program_id(1)
    di = SSM_D_INNER
    gw = di // SSM_GROUPS
    ns = SSM_D_STATE
    hp = SUBLANES
    kw = SSM_CONV_W

    nslab = SSM_CONV_DIM // LANES
    lanes_of = lambda j: slice(j * LANES, (j + 1) * LANES)

    @pl.when(c == 0)
    def _():
        for j in range(nslab):
            ext_scr[j, 0:hp, :] = conv0_ref[:, lanes_of(j)]
        for j in range(di // LANES):
            st_scr[:, lanes_of(j)] = h0_ref[j].T

    rows_c = -(-nv // SUBLANES) * SUBLANES
    for j in range(nslab):
        ext_scr[j, hp:hp + nv, :] = xbc_ref[:, lanes_of(j)]
        if nv != rows_c:
            ext_scr[j, hp + nv:hp + rows_c, :] = jnp.zeros((rows_c - nv, LANES), F32)

    def conv_act(c0, c1):
        pieces = []
        for j in range(c0 // LANES, c1 // LANES):
            acc = convb_ref[:, lanes_of(j)]
            for k in range(kw):
                acc = acc + convw_ref[k:k + 1, lanes_of(j)] * ext_scr[j, pl.ds(hp - (kw - 1) + k, rows_c), :]
            act = _silu(acc)
            if rows_c != q:
                act = jnp.concatenate([act, jnp.zeros((q - rows_c, LANES), F32)], axis=0)
            pieces.append(act)
        return pieces[0] if len(pieces) == 1 else jnp.concatenate(pieces, axis=1)

    row = lax.broadcasted_iota(jnp.int32, (q, q), 0)
    col = lax.broadcasted_iota(jnp.int32, (q, q), 1)
    causal = row >= col
    if nv == q:
        dt_raw = dt_ref[...]
    else:
        dt_raw = jnp.concatenate([dt_ref[...], jnp.zeros((q - nv, DT_PAD), F32)], axis=0)
    dt = jax.nn.softplus(dt_raw + dtb_ref[...])
    if nv != q:
        dt = jnp.where(row < nv, dt, 0.0)
    da = dt * (-jnp.exp(alog_ref[...]))
    cs = _dot_01_by_f32(jnp.where(causal, 1.0, 0.0).astype(BF16), da)
    cs_t = cs.T
    dt_parts, cs_parts = _split3(dt), _split3(cs)
    half = lax.broadcasted_iota(jnp.int32, (q, LANES), 1) < SSM_HEAD_DIM

    for g in range(SSM_GROUPS):
        gs = slice(g * gw, (g + 1) * gw)
        e_g = e_ref[:, gs]
        dt_x = sum(_dot(p, e_g) for p in dt_parts)
        cs_x = sum(_dot(p, e_g) for p in cs_parts)
        cs_last = cs_x[q - 1:q, :]
        xs = conv_act(g * gw, (g + 1) * gw)
        xdt = xs * dt_x
        xdt_b = xdt.astype(BF16)
        w_b = (xdt * jnp.exp(cs_last - cs_x)).astype(BF16)
        bg_t = conv_act(di + g * ns, di + (g + 1) * ns).T.astype(BF16)
        cg = conv_act(di + SSM_GROUPS * ns + g * ns, di + SSM_GROUPS * ns + (g + 1) * ns).astype(BF16)
        cb = _dot(cg, bg_t)
        st = st_scr[:, gs]
        y_off = _dot(cg, st.astype(BF16)) * jnp.exp(cs_x)
        st_scr[:, gs] = st * jnp.exp(cs_last) + _dot(bg_t, w_b)
        y_pairs = []
        for pair in range(gw // LANES):
            h0 = (g * gw + pair * LANES) // SSM_HEAD_DIM
            ps = slice(pair * LANES, (pair + 1) * LANES)
            ys = []
            for h in (h0, h0 + 1):
                diff = cs[:, h:h + 1] - cs_t[h:h + 1, :]
                m_h = (cb * jnp.exp(jnp.where(causal, diff, NEG))).astype(BF16)
                ys.append(_dot(m_h, xdt_b[:, ps]))
            y_pairs.append(jnp.where(half, ys[0], ys[1]))
        y = jnp.concatenate(y_pairs, axis=1) + y_off + xs * dskip_ref[:, gs]
        v = y[0:nv, :] * _silu(zb_ref[:, gs])
        vn = v * lax.rsqrt(jnp.mean(v * v, axis=-1, keepdims=True) + NORM_EPS)
        yn_ref[:, gs] = (vn * gssm_ref[:, gs]).astype(BF16)

    @pl.when(c == pl.num_programs(1) - 1)
    def _():
        for j in range(di // LANES):
            ho_ref[j] = st_scr[:, j * LANES:(j + 1) * LANES].T

    for j in range(nslab):
        tail = ext_scr[j, hp + nv - (kw - 1):hp + nv, :]
        convo_ref[:, lanes_of(j)] = tail
        ext_scr[j, hp - (kw - 1):hp, :] = tail


def _ssd(rest3, dt3, conv0, h0, conv_w, conv_b, dt_bias, a_log, d_skip, g_ssm, nv):
    b, l, _ = rest3.shape
    q = SSM_CHUNK
    nc = l // nv
    assert nv == q or nc == 1
    di, cd, nh = SSM_D_INNER, SSM_CONV_DIM, SSM_HEADS
    hp = SUBLANES
    conv0p = jnp.pad(conv0, ((0, 0), (hp - (SSM_CONV_W - 1), 0), (0, 0)))
    h0v = h0.reshape(b, di // LANES, LANES, SSM_D_STATE)
    pad1 = lambda v: jnp.pad(v.reshape(1, nh), ((0, 0), (0, DT_PAD - nh)))
    expand = np.zeros((DT_PAD, di), np.float32)
    expand[np.arange(di) // SSM_HEAD_DIM, np.arange(di)] = 1.0
    const = lambda shape: pl.BlockSpec(shape, lambda bi, c: (0,) * len(shape))
    yn, conv_o, h_o = pl.pallas_call(
        functools.partial(_ssd_kernel, nv=nv),
        grid=(b, nc),
        in_specs=[pl.BlockSpec((None, nv, cd), lambda bi, c: (bi, c, XBC_OFF // cd)),
                  pl.BlockSpec((None, nv, DT_PAD), lambda bi, c: (bi, c, 0)),
                  pl.BlockSpec((None, nv, di), lambda bi, c: (bi, c, ZB_OFF // di)),
                  pl.BlockSpec((None, hp, cd), lambda bi, c: (bi, 0, 0)),
                  pl.BlockSpec((None, di // LANES, LANES, SSM_D_STATE), lambda bi, c: (bi, 0, 0, 0)),
                  const((SSM_CONV_W, cd)), const((1, cd)), const((1, DT_PAD)), const((1, DT_PAD)),
                  const((1, di)), const((1, di)), const((DT_PAD, di))],
        out_specs=[pl.BlockSpec((None, nv, di), lambda bi, c: (bi, c, 0)),
                   pl.BlockSpec((None, SSM_CONV_W - 1, cd), lambda bi, c: (bi, 0, 0)),
                   pl.BlockSpec((None, di // LANES, LANES, SSM_D_STATE), lambda bi, c: (bi, 0, 0, 0))],
        out_shape=[jax.ShapeDtypeStruct((b, l, di), BF16),
                   jax.ShapeDtypeStruct((b, SSM_CONV_W - 1, cd), F32),
                   jax.ShapeDtypeStruct((b, di // LANES, LANES, SSM_D_STATE), F32)],
        scratch_shapes=[pltpu.VMEM((cd // LANES, hp + q, LANES), F32),
                        pltpu.VMEM((SSM_D_STATE, di), F32)],
        compiler_params=_cparams("arbitrary", "arbitrary"),
        name=f"ssd{nv}",
    )(rest3, dt3, rest3, conv0p, h0v, conv_w, conv_b.reshape(1, cd), pad1(dt_bias), pad1(a_log),
      jnp.repeat(d_skip, SSM_HEAD_DIM).reshape(1, di), g_ssm.reshape(1, di), jnp.asarray(expand, BF16))
    return yn.reshape(b * l, di), conv_o, h_o.reshape(b, nh, SSM_HEAD_DIM, SSM_D_STATE)


def _sigmoid(x):
    return 0.5 + 0.5 * jnp.tanh(0.5 * x)


def _mix_out_kernel(oa_ref, yn_ref, ra_ref, rb_ref, x_ref, gate_ref, gpost_ref, woa_ref, wob_ref, wo_ref, y_ref):
    p_a = _dot(oa_ref[...], woa_ref[...])
    p_b = _dot(yn_ref[...], wob_ref[...])
    merged = (_sigmoid(ra_ref[...]) * p_a + _sigmoid(rb_ref[...]) * p_b).astype(BF16)
    out = _dot(merged, wo_ref[...])
    nrm = out * lax.rsqrt(jnp.mean(out * out, axis=-1, keepdims=True) + NORM_EPS) * gpost_ref[...]
    y_ref[...] = x_ref[...] + gate_ref[...] * nrm


def _mix_out(oa, yn, rest, x2d, gate3, g_post, w_o_a, w_o_b, w_o, tm, rows_per_mod):
    t, d = x2d.shape
    r = gate3.shape[1]
    row = lambda w, off: pl.BlockSpec((tm, w), lambda i: (i, off // w))
    const = lambda shape: pl.BlockSpec(shape, lambda i: (0, 0), pipeline_mode=pl.Buffered(1))
    return pl.pallas_call(
        _mix_out_kernel,
        grid=(t // tm,),
        in_specs=[row(A_WIDTH, 0), row(d, 0), row(d, RA_OFF), row(d, RB_OFF), row(d, 0),
                  pl.BlockSpec((None, r, d), lambda i: ((i * tm) // rows_per_mod, 0, 0)),
                  pl.BlockSpec((1, d), lambda i: (0, 0)),
                  const((A_WIDTH, d)), const((d, d)), const((d, d))],
        out_specs=row(d, 0),
        out_shape=jax.ShapeDtypeStruct((t, d), F32),
        compiler_params=_cparams("arbitrary"),
        name="mix_out",
    )(oa, yn, rest, rest, x2d, gate3, g_post.reshape(1, d), w_o_a, w_o_b, w_o)


def _window_kernel(k_ref, v_ref, o_ref, *, rows):
    per_row = 2 * A_HEADS
    for kind, src in enumerate((k_ref, v_ref)):
        for h in range(A_HEADS):
            o_ref[pl.ds(kind * A_HEADS + h, rows, stride=per_row), :] = src[h]


def _window_rows(qkv, g, b, s, w):
    nh = A_GROUPS * A_HEADS
    rows = min(w, 256)
    per_row = 2 * A_HEADS
    assert w % rows == 0 and (s - w) % rows == 0
    src = lambda kind: pl.BlockSpec(
        (A_HEADS, rows, A_HEAD_DIM),
        lambda bi, i: (kind * A_GROUPS + g, (bi * s + s - w) // rows + i, 0))
    out = pl.pallas_call(
        functools.partial(_window_kernel, rows=rows),
        grid=(b, w // rows),
        in_specs=[src(1), src(2)],
        out_specs=pl.BlockSpec((None, rows * per_row, A_HEAD_DIM), lambda bi, i: (bi, i, 0)),
        out_shape=jax.ShapeDtypeStruct((b, w * per_row, A_HEAD_DIM), F32),
        compiler_params=_cparams("arbitrary", "arbitrary"),
        name=f"window{g}",
    )(qkv, qkv)
    return out.reshape(b, w, 2, A_HEADS, A_HEAD_DIM)


def _prep_w_kernel(tbl_ref, a_ref, b_ref, dt_ref, w_ref, wdt_ref):
    j = pl.program_id(0)
    d = a_ref.shape[1]
    late = tbl_ref[1, j]
    t = LANES

    def put(r, tile_rows):
        for c in range(d // t):
            w_ref[c * t:(c + 1) * t, r * t:(r + 1) * t] = tile_rows[:, c * t:(c + 1) * t].T.astype(BF16)

    @pl.when(late == 0)
    def _():
        for r in range(COL // t):
            put(r, a_ref[r * t:(r + 1) * t, :])

    @pl.when(late != 0)
    def _():
        for r in range(COL // t - 1):
            put(r, a_ref[r * t + SSM_HEADS:(r + 1) * t + SSM_HEADS, :])
        put(COL // t - 1, jnp.concatenate([a_ref[COL - t + SSM_HEADS:COL, :], b_ref[...]], axis=0))

    @pl.when(j == 0)
    def _():
        lane = lax.broadcasted_iota(jnp.int32, (t, t), 1)
        for c in range(d // t):
            blk = dt_ref[:, c * t:(c + 1) * t].T
            wdt_ref[c * t:(c + 1) * t, :] = jnp.where(lane < SSM_HEADS, blk, 0.0).astype(BF16)


def _prep_w(w_t):
    d = w_t.shape[1]
    assert W_DT % COL == 0 and W_RB - W_RA == D_MODEL and DT_PAD == LANES and SSM_HEADS % SUBLANES == 0
    blk = lambda off, n: [off // COL + k for k in range(n // COL)]
    plain = blk(0, N_QKV_COLS) + blk(W_XBC, SSM_CONV_DIM) + blk(W_ZB, SSM_D_INNER)
    late = blk(W_DT, 2 * D_MODEL)
    za = blk(W_ZA, A_WIDTH)
    src = plain + late + za
    is_late = [0] * len(plain) + [1] * len(late) + [0] * len(za)
    nxt = [0] * len(plain) + [(k + 1) * COL // SSM_HEADS for k in late] + [0] * len(za)
    tbl = jnp.asarray(np.array([src, is_late, nxt], np.int32))
    n = len(src)
    grid_spec = pltpu.PrefetchScalarGridSpec(
        num_scalar_prefetch=1,
        grid=(n,),
        in_specs=[pl.BlockSpec((COL, d), lambda j, t: (t[0, j], 0)),
                  pl.BlockSpec((SSM_HEADS, d), lambda j, t: (t[2, j], 0)),
                  pl.BlockSpec((DT_PAD, d), lambda j, t: (W_DT // DT_PAD, 0))],
        out_specs=[pl.BlockSpec((d, COL), lambda j, t: (0, j)),
                   pl.BlockSpec((d, DT_PAD), lambda j, t: (0, 0))])
    return pl.pallas_call(
        _prep_w_kernel,
        grid_spec=grid_spec,
        out_shape=[jax.ShapeDtypeStruct((d, n * COL), BF16), jax.ShapeDtypeStruct((d, DT_PAD), BF16)],
        compiler_params=_cparams("arbitrary"),
        name="prep_w",
    )(tbl, w_t, w_t, w_t)


def _layer(depth_i, x_p, x_s, caches, conv_s, ssm_s, c_p, c_s, w_ada, b_ada, g_pre, g_post, w_in, conv_w,
           conv_b, dt_bias, a_log, d_skip, g_ssm, w_o_a, w_o_b, w_o):
    bp, sp, d = x_p.shape
    bs, ts, _ = x_s.shape
    assert sp % SSM_CHUNK == 0

    w_main, w_dt = _prep_w(jnp.swapaxes(w_in, 0, 1))
    woa_b, wob_b, wo_b = w_o_a.astype(BF16), w_o_b.astype(BF16), w_o.astype(BF16)

    nmod = bp + bs
    mpad = -(-nmod // SUBLANES) * SUBLANES
    c_all = jnp.pad(jnp.concatenate([c_p, c_s], axis=0), ((0, mpad - nmod), (0, 0)))
    mod = _ada(c_all, w_ada, b_ada)
    shift, scale, gate = mod[:, :d], mod[:, d:2 * d], mod[:, 2 * d:]
    per_seq = lambda v: v[:bp].reshape(bp, 1, d)
    per_row = lambda v: jnp.repeat(v[bp:nmod], ts, axis=0).reshape(1, bs * ts, d)

    tp = bp * sp
    xp2 = x_p.reshape(tp, d)
    qkv_p, rest_p, dt_p, *kv_shifted = _in_proj(xp2, per_seq(scale), per_seq(shift), g_pre, w_main, w_dt, 1024, sp,
                                                caches, depth_i, ts)
    oa_p = _attn_prompt(qkv_p, rest_p, bp, sp)
    conv0 = jnp.zeros((bp, SSM_CONV_W - 1, SSM_CONV_DIM), F32)
    ssm0 = jnp.zeros((bp, SSM_HEADS, SSM_HEAD_DIM, SSM_D_STATE), F32)
    yn_p, conv_p, ssm_p = _ssd(rest_p.reshape(bp, sp, N_REST), dt_p.reshape(bp, sp, DT_PAD), conv0, ssm0,
                               conv_w, conv_b, dt_bias, a_log, d_skip, g_ssm, SSM_CHUNK)
    y_p = _mix_out(oa_p, yn_p, rest_p, xp2, per_seq(gate), g_post, woa_b, wob_b, wo_b, 256, sp).reshape(bp, sp, d)
    kv_p = [_window_rows(qkv_p, g, bp, sp, min(A_WINDOWS[g], sp)) for g in range(A_GROUPS)]

    tsn = bs * ts
    xs2 = x_s.reshape(tsn, d)
    qkv_s, rest_s, dt_s = _in_proj(xs2, per_row(scale), per_row(shift), g_pre, w_main, w_dt, tsn, tsn)[:3]
    qkvn = jnp.transpose(qkv_s.reshape(3, A_GROUPS, A_HEADS, bs, ts, A_HEAD_DIM), (0, 1, 3, 4, 2, 5))
    za5 = rest_s[:, ZA_OFF:ZA_OFF + A_WIDTH].reshape(bs, ts, A_HEADS, A_HEAD_DIM)
    oa5, kv_s = _attn_decode(qkvn, za5, caches, kv_shifted, depth_i)
    oa_s = oa5.reshape(tsn, A_WIDTH).astype(BF16)
    yn_s, conv_sn, ssm_sn = _ssd(rest_s.reshape(bs, ts, N_REST), dt_s.reshape(bs, ts, DT_PAD), conv_s, ssm_s,
                                 conv_w, conv_b, dt_bias, a_log, d_skip, g_ssm, ts)
    y_s = _mix_out(oa_s, yn_s, rest_s, xs2, per_row(gate), g_post, woa_b, wob_b, wo_b, tsn, tsn).reshape(bs, ts, d)
    return y_p, y_s, kv_p, conv_p, ssm_p, list(kv_s), conv_sn, ssm_sn


def kernel(x_prompt, x_sample, cache_a_w128, cache_a_w512, cache_a_w2048, state_conv, state_ssm, c_prompt, c_sample, w_ada, b_ada, g_pre, g_post, w_in, conv_w, conv_b, dt_bias, a_log, d_skip, g_ssm, w_o_a, w_o_b, w_o):
    depth = w_in.shape[0]
    caches = (cache_a_w128, cache_a_w512, cache_a_w2048)
    y_p, y_s = x_prompt, x_sample
    acc = [[] for _ in range(10)]
    for i in range(depth):
        y_p, y_s, kv_p, conv_p, ssm_p, kv_s, conv_s, ssm_s = _layer(
            i, y_p, y_s, caches, state_conv[i], state_ssm[i],
            c_prompt, c_sample, w_ada[i], b_ada[i], g_pre[i], g_post[i], w_in[i], conv_w[i], conv_b[i],
            dt_bias[i], a_log[i], d_skip[i], g_ssm[i], w_o_a[i], w_o_b[i], w_o[i])
        for lst, v in zip(acc, (*kv_p, conv_p, ssm_p, *kv_s, conv_s, ssm_s)):
            lst.append(v)
    return (y_p, y_s, *[jnp.stack(v) for v in acc])
```

```python
import functools

import numpy as np
import jax
import jax.numpy as jnp
from jax import lax
from jax.experimental import pallas as pl
from jax.experimental.pallas import tpu as pltpu

F32 = jnp.float32
BF16 = jnp.bfloat16

D_MODEL = 2048
A_WINDOWS = (128, 512, 2048)
A_DILATIONS = (1, 4, 16)
A_GROUPS = 3
A_HEADS = 8
A_HEAD_DIM = 128
A_WIDTH = A_HEADS * A_HEAD_DIM
A_NK = 128
ATT_SCALE = A_HEAD_DIM ** -0.5
SSM_D_INNER = D_MODEL
SSM_HEAD_DIM = 64
SSM_HEADS = SSM_D_INNER // SSM_HEAD_DIM
SSM_GROUPS = 8
SSM_D_STATE = 128
SSM_CONV_W = 4
SSM_CHUNK = 128
SSM_CONV_DIM = SSM_D_INNER + 2 * SSM_GROUPS * SSM_D_STATE
QKV_WIDTH = A_GROUPS * A_WIDTH
NORM_EPS = 1e-6

LANES = 128
SUBLANES = 8
VMEM_LIMIT = 60 * 1024 * 1024
NEG = -1e30

COL = 1024
N_QKV_COLS = 3 * QKV_WIDTH
N_SLABS = N_QKV_COLS // LANES
XBC_OFF = 0
ZB_OFF = XBC_OFF + SSM_CONV_DIM
RA_OFF = ZB_OFF + SSM_D_INNER
RB_OFF = RA_OFF + D_MODEL
ZA_OFF = RB_OFF + D_MODEL
N_REST = ZA_OFF + A_WIDTH
DT_PAD = LANES
W_ZA = N_QKV_COLS
W_ZB = W_ZA + A_WIDTH
W_XBC = W_ZB + SSM_D_INNER
W_DT = W_XBC + SSM_CONV_DIM
W_RA = W_DT + SSM_HEADS
W_RB = W_RA + D_MODEL


def _cparams(*sem):
    return pltpu.CompilerParams(dimension_semantics=sem, vmem_limit_bytes=VMEM_LIMIT)


def _dot(a, b):
    return jnp.dot(a, b, preferred_element_type=F32)


def _dot_nt(a, b):
    return lax.dot_general(a, b, (((1,), (1,)), ((), ())), preferred_element_type=F32)


def _split3(x):
    x1 = x.astype(BF16)
    r1 = x - x1.astype(F32)
    x2 = r1.astype(BF16)
    x3 = (r1 - x2.astype(F32)).astype(BF16)
    return x1, x2, x3


def _silu(x):
    h = 0.5 * x
    return h + h * jnp.tanh(h)


def _ada_kernel(c_ref, w_ref, b_ref, o_ref):
    s = _silu(c_ref[...]).astype(BF16)
    o_ref[...] = _dot(s, w_ref[...].astype(BF16)) + b_ref[...]


def _ada(c, w_ada, b_ada):
    m, d = c.shape
    n = w_ada.shape[1]
    tn = 512
    return pl.pallas_call(
        _ada_kernel,
        grid=(n // tn,),
        in_specs=[pl.BlockSpec((m, d), lambda j: (0, 0)),
                  pl.BlockSpec((d, tn), lambda j: (0, j)),
                  pl.BlockSpec((1, tn), lambda j: (0, j))],
        out_specs=pl.BlockSpec((m, tn), lambda j: (0, j)),
        out_shape=jax.ShapeDtypeStruct((m, n), F32),
        compiler_params=_cparams("arbitrary"),
        name="ada",
    )(c, w_ada, b_ada.reshape(1, n))


SHIFT_CHUNK_ROWS = 384
SHIFT_DMA_PRIORITY = 0


def _shift_plan(caches, t):
    plan, first = [], 0
    for c in caches:
        nb, n = c.shape[1], c.shape[2] - t
        cps = next(k for k in range(-(-n // SHIFT_CHUNK_ROWS), n + 1) if n % k == 0)
        plan.append((n // cps, cps, nb, first))
        first += nb * cps
    return plan, first


def _shift_step(step, c_refs, kv_refs, buf, in_sem, out_sem, tail_sem, plan, depth_i, t):
    def where(g, c):
        rows, cps, _, first = plan[g]
        return c // cps, c % cps, (c + first % 2) % 2, rows, cps

    def read(g, c):
        b, ci, slot, rows, _ = where(g, c)
        return pltpu.make_async_copy(c_refs[g].at[depth_i, b, pl.ds(t + ci * rows, rows)],
                                     buf.at[slot, pl.ds(0, rows)], in_sem.at[slot])

    def write(g, c):
        b, ci, slot, rows, _ = where(g, c)
        return pltpu.make_async_copy(buf.at[slot, pl.ds(0, rows)], kv_refs[g].at[b, pl.ds(ci * rows, rows)],
                                     out_sem.at[slot])

    def write_tail(g, c):
        b, _, slot, rows, cps = where(g, c)
        return pltpu.make_async_copy(buf.at[slot, pl.ds(rows - t, t)], kv_refs[g].at[b, pl.ds(cps * rows, t)],
                                     tail_sem.at[slot])

    def ends_sequence(g, c):
        return c % plan[g][1] == plan[g][1] - 1

    def in_group(g, k):
        first, n = plan[g][3], plan[g][2] * plan[g][1]
        return jnp.logical_and(k >= first, k < first + n)

    for g in range(len(plan)):
        @pl.when(in_group(g, step - 2))
        def _(g=g):
            c = step - 2 - plan[g][3]
            write(g, c).wait()
            pl.when(ends_sequence(g, c))(lambda: write_tail(g, c).wait())

    for g in range(len(plan)):
        @pl.when(in_group(g, step))
        def _(g=g):
            read(g, step - plan[g][3]).start(priority=SHIFT_DMA_PRIORITY)

    for g in range(len(plan)):
        @pl.when(in_group(g, step - 1))
        def _(g=g):
            c = step - 1 - plan[g][3]
            read(g, c).wait()
            write(g, c).start(priority=SHIFT_DMA_PRIORITY)
            pl.when(ends_sequence(g, c))(lambda: write_tail(g, c).start(priority=SHIFT_DMA_PRIORITY))


def _in_proj_kernel(*refs, tm, rc, n_qkv_steps, plan, depth_i, t_new):
    n_copy = len(plan)
    x_ref, sc_ref, sh_ref, g_ref, w_ref, wdt_ref = refs[:6]
    c_refs = refs[6:6 + n_copy]
    qkv_ref, rest_ref, dt_ref = refs[6 + n_copy:9 + n_copy]
    kv_refs = refs[9 + n_copy:9 + 2 * n_copy]
    h_scr = refs[9 + 2 * n_copy]
    j = pl.program_id(1)

    if n_copy:
        buf, in_sem, out_sem, tail_sem = refs[10 + 2 * n_copy:]
        step = pl.program_id(0) * pl.num_programs(1) + j
        _shift_step(step, c_refs, kv_refs, buf, in_sem, out_sem, tail_sem, plan, depth_i, t_new)

    @pl.when(j == 0)
    def _():
        per_row = sc_ref.shape[0] != 1

        def body(r, carry):
            rows = pl.ds(pl.multiple_of(r * rc, rc), rc)
            x = x_ref[rows, :]
            y = x * lax.rsqrt(jnp.mean(x * x, axis=-1, keepdims=True) + NORM_EPS) * g_ref[...]
            sc = sc_ref[rows, :] if per_row else sc_ref[...]
            sh = sh_ref[rows, :] if per_row else sh_ref[...]
            h_scr[rows, :] = (y * (1.0 + sc) + sh).astype(BF16)
            return carry

        lax.fori_loop(0, tm // rc, body, 0)
        dt_ref[...] = _dot(h_scr[...], wdt_ref[...])

    res = _dot(h_scr[...], w_ref[...])

    @pl.when(j < n_qkv_steps)
    def _():
        for c in range(COL // LANES):
            qkv_ref[c] = res[:, c * LANES:(c + 1) * LANES]

    @pl.when(j >= n_qkv_steps)
    def _():
        rest_ref[...] = res


def _in_proj(x2d, sc3, sh3, g_pre, w_main, w_dt, tm, rows_per_mod, caches=(), depth_i=0, t_new=0):
    t, d = x2d.shape
    r = sc3.shape[1]
    tn = COL
    rc = min(tm, 256)
    nq = N_QKV_COLS // tn
    spb = tn // LANES
    grid = (t // tm, (N_QKV_COLS + N_REST) // tn)
    n_copy = len(caches)
    plan, n_chunks = _shift_plan(caches, t_new)
    assert n_chunks + 2 <= grid[0] * grid[1]
    shift_scratch = []
    if n_copy:
        stage_rows = max(p[0] for p in plan)
        shift_scratch = [pltpu.VMEM((2, stage_rows) + caches[0].shape[3:], caches[0].dtype)] + [
            pltpu.SemaphoreType.DMA((2,))] * 3
    anyspec = pl.BlockSpec(memory_space=pl.ANY)
    mod_map = lambda i, j: ((i * tm) // rows_per_mod, 0, 0)
    return pl.pallas_call(
        functools.partial(_in_proj_kernel, tm=tm, rc=rc, n_qkv_steps=nq, plan=tuple(plan), depth_i=depth_i,
                          t_new=t_new),
        grid=grid,
        in_specs=[pl.BlockSpec((tm, d), lambda i, j: (i, 0)),
                  pl.BlockSpec((None, r, d), mod_map),
                  pl.BlockSpec((None, r, d), mod_map),
                  pl.BlockSpec((1, d), lambda i, j: (0, 0)),
                  pl.BlockSpec((d, tn), lambda i, j: (0, j)),
                  pl.BlockSpec((d, DT_PAD), lambda i, j: (0, 0))] + [anyspec] * n_copy,
        out_specs=[pl.BlockSpec((spb, tm, LANES), lambda i, j: (jnp.minimum(j, nq - 1), i, 0)),
                   pl.BlockSpec((tm, tn), lambda i, j: (i, jnp.maximum(j - nq, 0))),
                   pl.BlockSpec((tm, DT_PAD), lambda i, j: (i, 0))] + [anyspec] * n_copy,
        out_shape=[jax.ShapeDtypeStruct((N_SLABS, t, LANES), F32),
                   jax.ShapeDtypeStruct((t, N_REST), F32),
                   jax.ShapeDtypeStruct((t, DT_PAD), F32)]
        + [jax.ShapeDtypeStruct(c.shape[1:], c.dtype) for c in caches],
        scratch_shapes=[pltpu.VMEM((tm, d), BF16)] + shift_scratch,
        compiler_params=_cparams("arbitrary", "arbitrary"),
        name="in_proj",
    )(x2d, sc3, sh3, g_pre.reshape(1, d), w_main, w_dt, *caches)


def _attn_prompt_kernel(*refs, tb):
    ng = A_GROUPS
    q_refs, kc_refs, kp_refs = refs[0:ng], refs[ng:2 * ng], refs[2 * ng:3 * ng]
    vc_refs, vp_refs = refs[3 * ng:4 * ng], refs[4 * ng:5 * ng]
    za_ref, oa_ref, o_scr, lse_scr = refs[5 * ng:]
    i = pl.program_id(2)
    nk = A_NK
    row = lax.broadcasted_iota(jnp.int32, (nk, 2 * nk), 0)
    col = lax.broadcasted_iota(jnp.int32, (nk, 2 * nk), 1)
    ok = jnp.logical_and(col >= row, col <= row + nk)
    ok_first = jnp.logical_and(ok, col >= jnp.where(i > 0, 0, nk))

    def load(g, rows, k_prev_ref, v_prev_ref, prev_rows):
        qh = q_refs[g][rows, :].astype(BF16)
        k2 = jnp.concatenate([k_prev_ref[prev_rows, :], kc_refs[g][rows, :]], axis=0).astype(BF16)
        v2 = jnp.concatenate([v_prev_ref[prev_rows, :], vc_refs[g][rows, :]], axis=0).astype(BF16)
        return qh, k2, v2

    def attend(qh, k2, v2, mask):
        s = jnp.where(mask, _dot_nt(qh, k2), NEG)
        m = jnp.max(s, axis=-1, keepdims=True)
        e = jnp.exp2((s - m) * (ATT_SCALE * 1.4426950408889634))
        z = jnp.sum(e, axis=-1, keepdims=True)
        return _dot(e.astype(BF16), v2) / z, m * ATT_SCALE + jnp.log(z)

    unroll = 16

    def rows_of(d, sb, r):
        start = sb * (nk * d) + r
        if d == 1:
            return pl.ds(pl.multiple_of(start, nk), nk)
        return pl.ds(start, nk, stride=d)

    def run(g, items):
        d = A_DILATIONS[g]
        loaded, masks = [], []
        for sb, r in items:
            if isinstance(sb, int) and sb == 0:
                loaded.append(load(g, rows_of(d, 0, r), kp_refs[g], vp_refs[g], rows_of(d, 0, r)))
                masks.append(ok_first)
            else:
                loaded.append(load(g, rows_of(d, sb, r), kc_refs[g], vc_refs[g], rows_of(d, sb - 1, r)))
                masks.append(ok)
        results = [attend(*x, mask) for x, mask in zip(loaded, masks)]
        for (sb, r), (o, lse) in zip(items, results):
            o_scr[g, rows_of(d, sb, r), :] = o
            lse_scr[g, rows_of(d, sb, r), :] = jnp.broadcast_to(lse, (nk, LANES))

    for g in range(ng):
        d = A_DILATIONS[g]
        nblk = tb // nk
        assert nblk % unroll == 0 and (d % unroll == 0 or unroll % d == 0)
        if d >= unroll:
            per_sb = d // unroll
            run_first = lambda it, c, g=g: (run(g, [(0, it * unroll + u) for u in range(unroll)]), c)[1]
            lax.fori_loop(0, per_sb, run_first, 0)
            if nblk > d:
                def later(it, c, g=g, per_sb=per_sb):
                    sb = 1 + it // per_sb
                    r0 = (it % per_sb) * unroll
                    run(g, [(sb, r0 + u) for u in range(unroll)])
                    return c
                lax.fori_loop(0, (nblk - d) // unroll, later, 0)
        else:
            sbs = unroll // d
            run(g, [(sb, r) for sb in range(sbs) for r in range(d)])
            def later(it, c, g=g, d=d, sbs=sbs):
                run(g, [(it * sbs + s, r) for s in range(sbs) for r in range(d)])
                return c
            lax.fori_loop(1, nblk // unroll, later, 0)

    mc = 256

    def merge(c, carry):
        rows = pl.ds(pl.multiple_of(c * mc, mc), mc)
        l0, l1, l2 = lse_scr[0, rows, :], lse_scr[1, rows, :], lse_scr[2, rows, :]
        m = jnp.maximum(jnp.maximum(l0, l1), l2)
        e0, e1, e2 = jnp.exp(l0 - m), jnp.exp(l1 - m), jnp.exp(l2 - m)
        inv = 1.0 / (e0 + e1 + e2)
        oa = (e0 * inv) * o_scr[0, rows, :] + (e1 * inv) * o_scr[1, rows, :] + (e2 * inv) * o_scr[2, rows, :]
        oa_ref[rows, :] = (oa * _silu(za_ref[rows, :])).astype(BF16)
        return carry

    lax.fori_loop(0, tb // mc, merge, 0)


def _attn_prompt(qkv, rest, b, s):
    t = b * s
    nk = A_NK
    tb = nk * A_DILATIONS[-1]
    assert s % tb == 0
    nb = s // tb
    nh = A_GROUPS * A_HEADS

    def cur(kind, g):
        return pl.BlockSpec((None, tb, LANES), lambda bi, h, i: (kind * nh + g * A_HEADS + h, bi * nb + i, 0))

    def prev(kind, g):
        pb = nk * A_DILATIONS[g]
        return pl.BlockSpec((None, pb, LANES),
                            lambda bi, h, i: (kind * nh + g * A_HEADS + h,
                                              jnp.maximum((bi * s + i * tb) // pb - 1, 0), 0))

    groups = range(A_GROUPS)
    in_specs = ([cur(0, g) for g in groups] + [cur(1, g) for g in groups] + [prev(1, g) for g in groups]
                + [cur(2, g) for g in groups] + [prev(2, g) for g in groups]
                + [pl.BlockSpec((tb, LANES), lambda bi, h, i: (bi * nb + i, ZA_OFF // LANES + h))])
    return pl.pallas_call(
        functools.partial(_attn_prompt_kernel, tb=tb),
        grid=(b, A_HEADS, nb),
        in_specs=in_specs,
        out_specs=pl.BlockSpec((tb, LANES), lambda bi, h, i: (bi * nb + i, h)),
        out_shape=jax.ShapeDtypeStruct((t, A_WIDTH), BF16),
        scratch_shapes=[pltpu.VMEM((A_GROUPS, tb, LANES), F32), pltpu.VMEM((A_GROUPS, tb, LANES), F32)],
        compiler_params=_cparams("arbitrary", "arbitrary", "arbitrary"),
        name="attn_p",
    )(*([qkv] * (5 * A_GROUPS)), rest)


def _attn_decode_kernel(qn_ref, kn_ref, vn_ref, za_ref, c0_ref, c1_ref, c2_ref, kv0_any, kv1_any, kv2_any,
                        oa_ref, kv0_ref, kv1_ref, kv2_ref, *, t, chunk):
    del kv0_any, kv1_any, kv2_any
    tile = (A_HEADS, A_HEAD_DIM)
    log2e, ln2 = 1.4426950408889634, 0.6931471805599453

    for g, kv_ref in enumerate((kv0_ref, kv1_ref, kv2_ref)):
        kv_ref[:, 0] = kn_ref[g]
        kv_ref[:, 1] = vn_ref[g]

    def rows_of(g, tq, i0, n):
        d = A_DILATIONS[g]
        if g == A_GROUPS - 1:
            return c2_ref[i0:i0 + n, tq % d, 0], c2_ref[i0:i0 + n, tq % d, 1]
        c_ref = (c0_ref, c1_ref)[g]
        rows = pl.ds(tq % d + i0 * d, n, stride=d) if d > 1 else pl.ds(i0, n)
        return c_ref[rows, 0], c_ref[rows, 1]

    def partial(s, v):
        m = jnp.max(s, axis=0)
        p = jnp.exp2(s - m[None])
        return m, jnp.sum(p, axis=0), jnp.sum(p * v, axis=0)

    def combine(a, b):
        m = jnp.maximum(a[0], b[0])
        fa, fb = jnp.exp2(a[0] - m), jnp.exp2(b[0] - m)
        return m, a[1] * fa + b[1] * fb, a[2] * fa + b[2] * fb

    def attend(g, tq):
        d = A_DILATIONS[g]
        q = qn_ref[g, tq] * (ATT_SCALE * log2e)
        parts = []
        for i0 in range(0, A_NK, chunk):
            k, v = rows_of(g, tq, i0, chunk)
            s = jnp.broadcast_to(jnp.sum(k * q[None], axis=-1, keepdims=True), (chunk,) + tile)
            first_row = tq % d + i0 * d
            if first_row < tq:
                ri = first_row + d * lax.broadcasted_iota(jnp.int32, (chunk,) + tile, 0)
                s = jnp.where(ri >= tq, s, NEG)
            parts.append(partial(s, v))
        js = [tk for tk in range(tq + 1) if (tq - tk) % d == 0]
        s_new = jnp.stack([jnp.broadcast_to(jnp.sum(kn_ref[g, tk] * q, axis=-1, keepdims=True), tile)
                           for tk in js])
        parts.append(partial(s_new, jnp.stack([vn_ref[g, tk] for tk in js])))
        while len(parts) > 1:
            parts = [combine(parts[i], parts[i + 1]) if i + 1 < len(parts) else parts[i]
                     for i in range(0, len(parts), 2)]
        m, l, acc = parts[0]
        return acc / l, m * ln2 + jnp.log(l)

    for tq in range(t):
        outs = [attend(g, tq) for g in range(A_GROUPS)]
        lses = [o[1] for o in outs]
        mm = functools.reduce(jnp.maximum, lses)
        es = [jnp.exp(l - mm) for l in lses]
        inv = 1.0 / sum(es)
        oa = sum((e * inv) * o[0] for e, o in zip(es, outs))
        oa_ref[tq] = oa * _silu(za_ref[tq])


def _attn_decode(qkvn, za5, caches, kv_shifted, depth_i):
    _, ng, b, t, nh, hd = qkvn.shape
    wbs = [c.shape[2] for c in caches]
    d2 = A_DILATIONS[-1]
    assert t == SUBLANES and tuple(wbs) == A_WINDOWS and t <= d2
    assert all(w == A_NK * d for w, d in zip(wbs, A_DILATIONS))
    c2 = caches[-1].reshape(caches[-1].shape[:2] + (wbs[-1] // d2, d2, 2, nh, hd))
    new = lambda kind: pl.BlockSpec((None, ng, None, t, nh, hd), lambda bi: (kind, 0, bi, 0, 0, 0))
    whole = lambda n: pl.BlockSpec((None, None, n, 2, nh, hd), lambda bi: (depth_i, bi, 0, 0, 0, 0))
    tail = lambda n: pl.BlockSpec((None, t, 2, nh, hd), lambda bi: (bi, n // t - 1, 0, 0, 0))
    anyspec = pl.BlockSpec(memory_space=pl.ANY)
    res = pl.pallas_call(
        functools.partial(_attn_decode_kernel, t=t, chunk=32),
        grid=(b,),
        in_specs=[new(0), new(1), new(2),
                  pl.BlockSpec((None, t, nh, hd), lambda bi: (bi, 0, 0, 0)),
                  whole(wbs[0]), whole(wbs[1]),
                  pl.BlockSpec((None, None, wbs[-1] // d2, t, 2, nh, hd), lambda bi: (depth_i, bi, 0, 0, 0, 0, 0)),
                  anyspec, anyspec, anyspec],
        out_specs=[pl.BlockSpec((None, t, nh, hd), lambda bi: (bi, 0, 0, 0))] + [tail(w) for w in wbs],
        out_shape=[jax.ShapeDtypeStruct((b, t, nh, hd), F32)]
        + [jax.ShapeDtypeStruct((b, w, 2, nh, hd), F32) for w in wbs],
        input_output_aliases={7: 1, 8: 2, 9: 3},
        compiler_params=_cparams("arbitrary"),
        name="attn_d",
    )(qkvn, qkvn, qkvn, za5, caches[0], caches[1], c2, *kv_shifted)
    return res[0], res[1:]


def _ssd_kernel(xbc_ref, dt_ref, zb_ref, conv0_ref, h0_ref, convw_ref, convb_ref, dtb_ref, alog_ref,
                dskip_ref, gssm_ref, e_ref, yn_ref, convo_ref, ho_ref, ext_scr, st_scr, *, nv):
    q = SSM_CHUNK
    c = pl.program_id(1)
    di = SSM_D_INNER
    gw = di // SSM_GROUPS
    ns = SSM_D_STATE
    hp = SUBLANES
    kw = SSM_CONV_W

    nslab = SSM_CONV_DIM // LANES
    lanes_of = lambda j: slice(j * LANES, (j + 1) * LANES)

    @pl.when(c == 0)
    def _():
        for j in range(nslab):
            ext_scr[j, 0:hp, :] = conv0_ref[:, lanes_of(j)]
        for j in range(di // LANES):
            st_scr[:, lanes_of(j)] = h0_ref[j].T

    rows_c = -(-nv // SUBLANES) * SUBLANES
    for j in range(nslab):
        ext_scr[j, hp:hp + nv, :] = xbc_ref[:, lanes_of(j)]
        if nv != rows_c:
            ext_scr[j, hp + nv:hp + rows_c, :] = jnp.zeros((rows_c - nv, LANES), F32)

    def conv_act(c0, c1):
        pieces = []
        for j in range(c0 // LANES, c1 // LANES):
            acc = convb_ref[:, lanes_of(j)]
            for k in range(kw):
                acc = acc + convw_ref[k:k + 1, lanes_of(j)] * ext_scr[j, pl.ds(hp - (kw - 1) + k, rows_c), :]
            act = _silu(acc)
            if rows_c != q:
                act = jnp.concatenate([act, jnp.zeros((q - rows_c, LANES), F32)], axis=0)
            pieces.append(act)
        return pieces[0] if len(pieces) == 1 else jnp.concatenate(pieces, axis=1)

    row = lax.broadcasted_iota(jnp.int32, (q, q), 0)
    col = lax.broadcasted_iota(jnp.int32, (q, q), 1)
    causal = row >= col
    if nv == q:
        dt_raw = dt_ref[...]
    else:
        dt_raw = jnp.concatenate([dt_ref[...], jnp.zeros((q - nv, DT_PAD), F32)], axis=0)
    dt = jax.nn.softplus(dt_raw + dtb_ref[...])
    if nv != q:
        dt = jnp.where(row < nv, dt, 0.0)
    da = dt * (-jnp.exp(alog_ref[...]))
    tri = jnp.where(causal, 1.0, 0.0).astype(BF16)
    cs = _dot(jnp.concatenate([tri] * 3, axis=1), jnp.concatenate(_split3(da), axis=0)) * 1.4426950408889634
    cs_t = cs.T
    dt_cat = jnp.concatenate(_split3(dt), axis=1)
    cs_cat = jnp.concatenate(_split3(cs), axis=1)
    half = lax.broadcasted_iota(jnp.int32, (q, LANES), 1) < SSM_HEAD_DIM

    for g in range(SSM_GROUPS):
        gs = slice(g * gw, (g + 1) * gw)
        e_g = e_ref[:, gs]
        dt_x = _dot(dt_cat, e_g)
        cs_x = _dot(cs_cat, e_g)
        cs_last = cs_x[q - 1:q, :]
        xs = conv_act(g * gw, (g + 1) * gw)
        xdt = xs * dt_x
        xdt_b = xdt.astype(BF16)
        w_b = (xdt * jnp.exp2(cs_last - cs_x)).astype(BF16)
        bg_t = conv_act(di + g * ns, di + (g + 1) * ns).T.astype(BF16)
        cg = conv_act(di + SSM_GROUPS * ns + g * ns, di + SSM_GROUPS * ns + (g + 1) * ns).astype(BF16)
        cb = _dot(cg, bg_t)
        st = st_scr[:, gs]
        y_off = _dot(cg, st.astype(BF16)) * jnp.exp2(cs_x)
        st_scr[:, gs] = st * jnp.exp2(cs_last) + _dot(bg_t, w_b)
        y_pairs = []
        for pair in range(gw // LANES):
            h0 = (g * gw + pair * LANES) // SSM_HEAD_DIM
            ps = slice(pair * LANES, (pair + 1) * LANES)
            ys = []
            for h in (h0, h0 + 1):
                diff = cs[:, h:h + 1] - cs_t[h:h + 1, :]
                m_h = (cb * jnp.exp2(jnp.where(causal, diff, NEG))).astype(BF16)
                ys.append(_dot(m_h, xdt_b[:, ps]))
            y_pairs.append(jnp.where(half, ys[0], ys[1]))
        y = jnp.concatenate(y_pairs, axis=1) + y_off + xs * dskip_ref[:, gs]
        v = y[0:nv, :] * _silu(zb_ref[:, gs])
        vn = v * lax.rsqrt(jnp.mean(v * v, axis=-1, keepdims=True) + NORM_EPS)
        yn_ref[:, gs] = (vn * gssm_ref[:, gs]).astype(BF16)

    @pl.when(c == pl.num_programs(1) - 1)
    def _():
        for j in range(di // LANES):
            ho_ref[j] = st_scr[:, j * LANES:(j + 1) * LANES].T

    for j in range(nslab):
        tail = ext_scr[j, hp + nv - (kw - 1):hp + nv, :]
        convo_ref[:, lanes_of(j)] = tail
        ext_scr[j, hp - (kw - 1):hp, :] = tail


def _ssd(rest3, dt3, conv0, h0, conv_w, conv_b, dt_bias, a_log, d_skip, g_ssm, nv):
    b, l, _ = rest3.shape
    q = SSM_CHUNK
    nc = l // nv
    assert nv == q or nc == 1
    di, cd, nh = SSM_D_INNER, SSM_CONV_DIM, SSM_HEADS
    hp = SUBLANES
    conv0p = jnp.pad(conv0, ((0, 0), (hp - (SSM_CONV_W - 1), 0), (0, 0)))
    h0v = h0.reshape(b, di // LANES, LANES, SSM_D_STATE)
    pad1 = lambda v: jnp.pad(v.reshape(1, nh), ((0, 0), (0, DT_PAD - nh)))
    expand = np.zeros((DT_PAD, di), np.float32)
    expand[np.arange(di) // SSM_HEAD_DIM, np.arange(di)] = 1.0
    expand = np.tile(expand, (3, 1))
    const = lambda shape: pl.BlockSpec(shape, lambda bi, c: (0,) * len(shape))
    yn, conv_o, h_o = pl.pallas_call(
        functools.partial(_ssd_kernel, nv=nv),
        grid=(b, nc),
        in_specs=[pl.BlockSpec((None, nv, cd), lambda bi, c: (bi, c, XBC_OFF // cd)),
                  pl.BlockSpec((None, nv, DT_PAD), lambda bi, c: (bi, c, 0)),
                  pl.BlockSpec((None, nv, di), lambda bi, c: (bi, c, ZB_OFF // di)),
                  pl.BlockSpec((None, hp, cd), lambda bi, c: (bi, 0, 0)),
                  pl.BlockSpec((None, di // LANES, LANES, SSM_D_STATE), lambda bi, c: (bi, 0, 0, 0)),
                  const((SSM_CONV_W, cd)), const((1, cd)), const((1, DT_PAD)), const((1, DT_PAD)),
                  const((1, di)), const((1, di)), const((3 * DT_PAD, di))],
        out_specs=[pl.BlockSpec((None, nv, di), lambda bi, c: (bi, c, 0)),
                   pl.BlockSpec((None, SSM_CONV_W - 1, cd), lambda bi, c: (bi, 0, 0)),
                   pl.BlockSpec((None, di // LANES, LANES, SSM_D_STATE), lambda bi, c: (bi, 0, 0, 0))],
        out_shape=[jax.ShapeDtypeStruct((b, l, di), BF16),
                   jax.ShapeDtypeStruct((b, SSM_CONV_W - 1, cd), F32),
                   jax.ShapeDtypeStruct((b, di // LANES, LANES, SSM_D_STATE), F32)],
        scratch_shapes=[pltpu.VMEM((cd // LANES, hp + q, LANES), F32),
                        pltpu.VMEM((SSM_D_STATE, di), F32)],
        compiler_params=_cparams("arbitrary", "arbitrary"),
        name=f"ssd{nv}",
    )(rest3, dt3, rest3, conv0p, h0v, conv_w, conv_b.reshape(1, cd), pad1(dt_bias), pad1(a_log),
      jnp.repeat(d_skip, SSM_HEAD_DIM).reshape(1, di), g_ssm.reshape(1, di), jnp.asarray(expand, BF16))
    return yn.reshape(b * l, di), conv_o, h_o.reshape(b, nh, SSM_HEAD_DIM, SSM_D_STATE)


def _sigmoid(x):
    return 0.5 + 0.5 * jnp.tanh(0.5 * x)


def _mix_out_kernel(oa_ref, yn_ref, ra_ref, rb_ref, x_ref, gate_ref, gpost_ref, woa_ref, wob_ref, wo_ref, y_ref):
    p_a = _dot(oa_ref[...], woa_ref[...])
    p_b = _dot(yn_ref[...], wob_ref[...])
    merged = (_sigmoid(ra_ref[...]) * p_a + _sigmoid(rb_ref[...]) * p_b).astype(BF16)
    out = _dot(merged, wo_ref[...])
    nrm = out * lax.rsqrt(jnp.mean(out * out, axis=-1, keepdims=True) + NORM_EPS) * gpost_ref[...]
    y_ref[...] = x_ref[...] + gate_ref[...] * nrm


def _mix_out(oa, yn, rest, x2d, gate3, g_post, w_o_a, w_o_b, w_o, tm, rows_per_mod):
    t, d = x2d.shape
    r = gate3.shape[1]
    row = lambda w, off: pl.BlockSpec((tm, w), lambda i: (i, off // w))
    const = lambda shape: pl.BlockSpec(shape, lambda i: (0, 0), pipeline_mode=pl.Buffered(1))
    return pl.pallas_call(
        _mix_out_kernel,
        grid=(t // tm,),
        in_specs=[row(A_WIDTH, 0), row(d, 0), row(d, RA_OFF), row(d, RB_OFF), row(d, 0),
                  pl.BlockSpec((None, r, d), lambda i: ((i * tm) // rows_per_mod, 0, 0)),
                  pl.BlockSpec((1, d), lambda i: (0, 0)),
                  const((A_WIDTH, d)), const((d, d)), const((d, d))],
        out_specs=row(d, 0),
        out_shape=jax.ShapeDtypeStruct((t, d), F32),
        compiler_params=_cparams("arbitrary"),
        name="mix_out",
    )(oa, yn, rest, rest, x2d, gate3, g_post.reshape(1, d), w_o_a, w_o_b, w_o)


def _window_kernel(k_ref, v_ref, o_ref, *, rows):
    per_row = 2 * A_HEADS
    for kind, src in enumerate((k_ref, v_ref)):
        for h in range(A_HEADS):
            o_ref[pl.ds(kind * A_HEADS + h, rows, stride=per_row), :] = src[h]


def _window_rows(qkv, g, b, s, w):
    nh = A_GROUPS * A_HEADS
    rows = min(w, 256)
    per_row = 2 * A_HEADS
    assert w % rows == 0 and (s - w) % rows == 0
    src = lambda kind: pl.BlockSpec(
        (A_HEADS, rows, A_HEAD_DIM),
        lambda bi, i: (kind * A_GROUPS + g, (bi * s + s - w) // rows + i, 0))
    out = pl.pallas_call(
        functools.partial(_window_kernel, rows=rows),
        grid=(b, w // rows),
        in_specs=[src(1), src(2)],
        out_specs=pl.BlockSpec((None, rows * per_row, A_HEAD_DIM), lambda bi, i: (bi, i, 0)),
        out_shape=jax.ShapeDtypeStruct((b, w * per_row, A_HEAD_DIM), F32),
        compiler_params=_cparams("arbitrary", "arbitrary"),
        name=f"window{g}",
    )(qkv, qkv)
    return out.reshape(b, w, 2, A_HEADS, A_HEAD_DIM)


def _prep_w_kernel(tbl_ref, a_ref, b_ref, dt_ref, w_ref, wdt_ref):
    j = pl.program_id(0)
    d = a_ref.shape[1]
    late = tbl_ref[1, j]
    t = LANES

    def put(r, tile_rows):
        for c in range(d // t):
            w_ref[c * t:(c + 1) * t, r * t:(r + 1) * t] = tile_rows[:, c * t:(c + 1) * t].T.astype(BF16)

    @pl.when(late == 0)
    def _():
        for r in range(COL // t):
            put(r, a_ref[r * t:(r + 1) * t, :])

    @pl.when(late != 0)
    def _():
        for r in range(COL // t - 1):
            put(r, a_ref[r * t + SSM_HEADS:(r + 1) * t + SSM_HEADS, :])
        put(COL // t - 1, jnp.concatenate([a_ref[COL - t + SSM_HEADS:COL, :], b_ref[...]], axis=0))

    @pl.when(j == 0)
    def _():
        lane = lax.broadcasted_iota(jnp.int32, (t, t), 1)
        for c in range(d // t):
            blk = dt_ref[:, c * t:(c + 1) * t].T
            wdt_ref[c * t:(c + 1) * t, :] = jnp.where(lane < SSM_HEADS, blk, 0.0).astype(BF16)


def _prep_w(w_t):
    d = w_t.shape[1]
    assert W_DT % COL == 0 and W_RB - W_RA == D_MODEL and DT_PAD == LANES and SSM_HEADS % SUBLANES == 0
    blk = lambda off, n: [off // COL + k for k in range(n // COL)]
    plain = blk(0, N_QKV_COLS) + blk(W_XBC, SSM_CONV_DIM) + blk(W_ZB, SSM_D_INNER)
    late = blk(W_DT, 2 * D_MODEL)
    za = blk(W_ZA, A_WIDTH)
    src = plain + late + za
    is_late = [0] * len(plain) + [1] * len(late) + [0] * len(za)
    nxt = [0] * len(plain) + [(k + 1) * COL // SSM_HEADS for k in late] + [0] * len(za)
    tbl = jnp.asarray(np.array([src, is_late, nxt], np.int32))
    n = len(src)
    grid_spec = pltpu.PrefetchScalarGridSpec(
        num_scalar_prefetch=1,
        grid=(n,),
        in_specs=[pl.BlockSpec((COL, d), lambda j, t: (t[0, j], 0)),
                  pl.BlockSpec((SSM_HEADS, d), lambda j, t: (t[2, j], 0)),
                  pl.BlockSpec((DT_PAD, d), lambda j, t: (W_DT // DT_PAD, 0))],
        out_specs=[pl.BlockSpec((d, COL), lambda j, t: (0, j)),
                   pl.BlockSpec((d, DT_PAD), lambda j, t: (0, 0))])
    return pl.pallas_call(
        _prep_w_kernel,
        grid_spec=grid_spec,
        out_shape=[jax.ShapeDtypeStruct((d, n * COL), BF16), jax.ShapeDtypeStruct((d, DT_PAD), BF16)],
        compiler_params=_cparams("arbitrary"),
        name="prep_w",
    )(tbl, w_t, w_t, w_t)


def _layer(depth_i, x_p, x_s, caches, conv_s, ssm_s, c_p, c_s, w_ada, b_ada, g_pre, g_post, w_in, conv_w,
           conv_b, dt_bias, a_log, d_skip, g_ssm, w_o_a, w_o_b, w_o):
    bp, sp, d = x_p.shape
    bs, ts, _ = x_s.shape
    assert sp % SSM_CHUNK == 0

    w_main, w_dt = _prep_w(jnp.swapaxes(w_in, 0, 1))
    woa_b, wob_b, wo_b = w_o_a.astype(BF16), w_o_b.astype(BF16), w_o.astype(BF16)

    nmod = bp + bs
    mpad = -(-nmod // SUBLANES) * SUBLANES
    c_all = jnp.pad(jnp.concatenate([c_p, c_s], axis=0), ((0, mpad - nmod), (0, 0)))
    mod = _ada(c_all, w_ada, b_ada)
    shift, scale, gate = mod[:, :d], mod[:, d:2 * d], mod[:, 2 * d:]
    per_seq = lambda v: v[:bp].reshape(bp, 1, d)
    per_row = lambda v: jnp.repeat(v[bp:nmod], ts, axis=0).reshape(1, bs * ts, d)

    tp = bp * sp
    xp2 = x_p.reshape(tp, d)
    qkv_p, rest_p, dt_p, *kv_shifted = _in_proj(xp2, per_seq(scale), per_seq(shift), g_pre, w_main, w_dt, 1024, sp,
                                                caches, depth_i, ts)
    oa_p = _attn_prompt(qkv_p, rest_p, bp, sp)
    conv0 = jnp.zeros((bp, SSM_CONV_W - 1, SSM_CONV_DIM), F32)
    ssm0 = jnp.zeros((bp, SSM_HEADS, SSM_HEAD_DIM, SSM_D_STATE), F32)
    yn_p, conv_p, ssm_p = _ssd(rest_p.reshape(bp, sp, N_REST), dt_p.reshape(bp, sp, DT_PAD), conv0, ssm0,
                               conv_w, conv_b, dt_bias, a_log, d_skip, g_ssm, SSM_CHUNK)
    y_p = _mix_out(oa_p, yn_p, rest_p, xp2, per_seq(gate), g_post, woa_b, wob_b, wo_b, 256, sp).reshape(bp, sp, d)
    kv_p = [_window_rows(qkv_p, g, bp, sp, min(A_WINDOWS[g], sp)) for g in range(A_GROUPS)]

    tsn = bs * ts
    xs2 = x_s.reshape(tsn, d)
    qkv_s, rest_s, dt_s = _in_proj(xs2, per_row(scale), per_row(shift), g_pre, w_main, w_dt, tsn, tsn)[:3]
    qkvn = jnp.transpose(qkv_s.reshape(3, A_GROUPS, A_HEADS, bs, ts, A_HEAD_DIM), (0, 1, 3, 4, 2, 5))
    za5 = rest_s[:, ZA_OFF:ZA_OFF + A_WIDTH].reshape(bs, ts, A_HEADS, A_HEAD_DIM)
    oa5, kv_s = _attn_decode(qkvn, za5, caches, kv_shifted, depth_i)
    oa_s = oa5.reshape(tsn, A_WIDTH).astype(BF16)
    yn_s, conv_sn, ssm_sn = _ssd(rest_s.reshape(bs, ts, N_REST), dt_s.reshape(bs, ts, DT_PAD), conv_s, ssm_s,
                                 conv_w, conv_b, dt_bias, a_log, d_skip, g_ssm, ts)
    y_s = _mix_out(oa_s, yn_s, rest_s, xs2, per_row(gate), g_post, woa_b, wob_b, wo_b, tsn, tsn).reshape(bs, ts, d)
    return y_p, y_s, kv_p, conv_p, ssm_p, list(kv_s), conv_sn, ssm_sn


def kernel(x_prompt, x_sample, cache_a_w128, cache_a_w512, cache_a_w2048, state_conv, state_ssm, c_prompt, c_sample, w_ada, b_ada, g_pre, g_post, w_in, conv_w, conv_b, dt_bias, a_log, d_skip, g_ssm, w_o_a, w_o_b, w_o):
    depth = w_in.shape[0]
    caches = (cache_a_w128, cache_a_w512, cache_a_w2048)
    y_p, y_s = x_prompt, x_sample
    acc = [[] for _ in range(10)]
    for i in range(depth):
        y_p, y_s, kv_p, conv_p, ssm_p, kv_s, conv_s, ssm_s = _layer(
            i, y_p, y_s, caches, state_conv[i], state_ssm[i],
            c_prompt, c_sample, w_ada[i], b_ada[i], g_pre[i], g_post[i], w_in[i], conv_w[i], conv_b[i],
            dt_bias[i], a_log[i], d_skip[i], g_ssm[i], w_o_a[i], w_o_b[i], w_o[i])
        for lst, v in zip(acc, (*kv_p, conv_p, ssm_p, *kv_s, conv_s, ssm_s)):
            lst.append(v)
    return (y_p, y_s, *[jnp.stack(v) for v in acc])
```

```python
import functools

import numpy as np
import jax
import jax.numpy as jnp
from jax import lax
from jax.experimental import pallas as pl
from jax.experimental.pallas import tpu as pltpu

F32 = jnp.float32
BF16 = jnp.bfloat16

D_MODEL = 2048
A_WINDOWS = (128, 512, 2048)
A_DILATIONS = (1, 4, 16)
A_GROUPS = 3
A_HEADS = 8
A_HEAD_DIM = 128
A_WIDTH = A_HEADS * A_HEAD_DIM
A_NK = 128
ATT_SCALE = A_HEAD_DIM ** -0.5
SSM_D_INNER = D_MODEL
SSM_HEAD_DIM = 64
SSM_HEADS = SSM_D_INNER // SSM_HEAD_DIM
SSM_GROUPS = 8
SSM_D_STATE = 128
SSM_CONV_W = 4
SSM_CHUNK = 128
SSM_CONV_DIM = SSM_D_INNER + 2 * SSM_GROUPS * SSM_D_STATE
QKV_WIDTH = A_GROUPS * A_WIDTH
NORM_EPS = 1e-6

LANES = 128
SUBLANES = 8
VMEM_LIMIT = 60 * 1024 * 1024
NEG = -1e30

ADA_COLS = 512
IN_PROJ_ROWS = 1024
PRENORM_ROWS = 256
ATTN_MERGE_ROWS = 256
DECODE_CHUNK_ROWS = 32
MIX_OUT_ROWS = 256
WINDOW_ROWS = 512

COL = 1024
N_QKV_COLS = 3 * QKV_WIDTH
N_SLABS = N_QKV_COLS // LANES
XBC_OFF = 0
ZB_OFF = XBC_OFF + SSM_CONV_DIM
RA_OFF = ZB_OFF + SSM_D_INNER
RB_OFF = RA_OFF + D_MODEL
ZA_OFF = RB_OFF + D_MODEL
N_REST = ZA_OFF + A_WIDTH
DT_PAD = LANES
W_ZA = N_QKV_COLS
W_ZB = W_ZA + A_WIDTH
W_XBC = W_ZB + SSM_D_INNER
W_DT = W_XBC + SSM_CONV_DIM
W_RA = W_DT + SSM_HEADS
W_RB = W_RA + D_MODEL


def _cparams(*sem):
    return pltpu.CompilerParams(dimension_semantics=sem, vmem_limit_bytes=VMEM_LIMIT)


def _dot(a, b):
    return jnp.dot(a, b, preferred_element_type=F32)


def _dot_nt(a, b):
    return lax.dot_general(a, b, (((1,), (1,)), ((), ())), preferred_element_type=F32)


def _split3(x):
    x1 = x.astype(BF16)
    r1 = x - x1.astype(F32)
    x2 = r1.astype(BF16)
    x3 = (r1 - x2.astype(F32)).astype(BF16)
    return x1, x2, x3


def _silu(x):
    h = 0.5 * x
    return h + h * jnp.tanh(h)


def _ada_kernel(c_ref, w_ref, b_ref, o_ref):
    s = _silu(c_ref[...]).astype(BF16)
    o_ref[...] = _dot(s, w_ref[...].astype(BF16)) + b_ref[...]


def _ada(c, w_ada, b_ada):
    m, d = c.shape
    n = w_ada.shape[1]
    tn = ADA_COLS
    return pl.pallas_call(
        _ada_kernel,
        grid=(n // tn,),
        in_specs=[pl.BlockSpec((m, d), lambda j: (0, 0)),
                  pl.BlockSpec((d, tn), lambda j: (0, j)),
                  pl.BlockSpec((1, tn), lambda j: (0, j))],
        out_specs=pl.BlockSpec((m, tn), lambda j: (0, j)),
        out_shape=jax.ShapeDtypeStruct((m, n), F32),
        compiler_params=_cparams("arbitrary"),
        name="ada",
    )(c, w_ada, b_ada.reshape(1, n))


SHIFT_CHUNK_ROWS = 384


def _shift_plan(caches, t):
    plan, first = [], 0
    for c in caches:
        nb, n = c.shape[1], c.shape[2] - t
        cps = next(k for k in range(-(-n // SHIFT_CHUNK_ROWS), n + 1) if n % k == 0)
        plan.append((n // cps, cps, nb, first))
        first += nb * cps
    return plan, first


def _shift_step(step, c_refs, kv_refs, buf, in_sem, out_sem, tail_sem, plan, depth_i, t):
    def where(g, c):
        rows, cps, _, first = plan[g]
        return c // cps, c % cps, (c + first % 2) % 2, rows, cps

    def read(g, c):
        b, ci, slot, rows, _ = where(g, c)
        return pltpu.make_async_copy(c_refs[g].at[depth_i, b, pl.ds(t + ci * rows, rows)],
                                     buf.at[slot, pl.ds(0, rows)], in_sem.at[slot])

    def write(g, c):
        b, ci, slot, rows, _ = where(g, c)
        return pltpu.make_async_copy(buf.at[slot, pl.ds(0, rows)], kv_refs[g].at[b, pl.ds(ci * rows, rows)],
                                     out_sem.at[slot])

    def write_tail(g, c):
        b, _, slot, rows, cps = where(g, c)
        return pltpu.make_async_copy(buf.at[slot, pl.ds(rows - t, t)], kv_refs[g].at[b, pl.ds(cps * rows, t)],
                                     tail_sem.at[slot])

    def ends_sequence(g, c):
        return c % plan[g][1] == plan[g][1] - 1

    def in_group(g, k):
        first, n = plan[g][3], plan[g][2] * plan[g][1]
        return jnp.logical_and(k >= first, k < first + n)

    for g in range(len(plan)):
        @pl.when(in_group(g, step - 2))
        def _(g=g):
            c = step - 2 - plan[g][3]
            write(g, c).wait()
            pl.when(ends_sequence(g, c))(lambda: write_tail(g, c).wait())

    for g in range(len(plan)):
        @pl.when(in_group(g, step))
        def _(g=g):
            read(g, step - plan[g][3]).start()

    for g in range(len(plan)):
        @pl.when(in_group(g, step - 1))
        def _(g=g):
            c = step - 1 - plan[g][3]
            read(g, c).wait()
            write(g, c).start()
            pl.when(ends_sequence(g, c))(lambda: write_tail(g, c).start())


def _in_proj_kernel(*refs, tm, rc, n_qkv_steps, plan, depth_i, t_new):
    n_copy = len(plan)
    x_ref, sc_ref, sh_ref, g_ref, w_ref, wdt_ref = refs[:6]
    c_refs = refs[6:6 + n_copy]
    qkv_ref, rest_ref, dt_ref = refs[6 + n_copy:9 + n_copy]
    kv_refs = refs[9 + n_copy:9 + 2 * n_copy]
    h_scr = refs[9 + 2 * n_copy]
    j = pl.program_id(1)

    if n_copy:
        buf, in_sem, out_sem, tail_sem = refs[10 + 2 * n_copy:]
        step = pl.program_id(0) * pl.num_programs(1) + j
        _shift_step(step, c_refs, kv_refs, buf, in_sem, out_sem, tail_sem, plan, depth_i, t_new)

    @pl.when(j == 0)
    def _():
        per_row = sc_ref.shape[0] != 1

        def body(r, carry):
            rows = pl.ds(pl.multiple_of(r * rc, rc), rc)
            x = x_ref[rows, :]
            y = x * lax.rsqrt(jnp.mean(x * x, axis=-1, keepdims=True) + NORM_EPS) * g_ref[...]
            sc = sc_ref[rows, :] if per_row else sc_ref[...]
            sh = sh_ref[rows, :] if per_row else sh_ref[...]
            h_scr[rows, :] = (y * (1.0 + sc) + sh).astype(BF16)
            return carry

        lax.fori_loop(0, tm // rc, body, 0)
        dt_ref[...] = _dot(h_scr[...], wdt_ref[...])

    res = _dot(h_scr[...], w_ref[...])

    @pl.when(j < n_qkv_steps)
    def _():
        for c in range(COL // LANES):
            qkv_ref[c] = res[:, c * LANES:(c + 1) * LANES]

    @pl.when(j >= n_qkv_steps)
    def _():
        rest_ref[...] = res


def _in_proj(x2d, sc3, sh3, g_pre, w_main, w_dt, tm, rows_per_mod, caches=(), depth_i=0, t_new=0):
    t, d = x2d.shape
    r = sc3.shape[1]
    tn = COL
    rc = min(tm, PRENORM_ROWS)
    nq = N_QKV_COLS // tn
    spb = tn // LANES
    grid = (t // tm, (N_QKV_COLS + N_REST) // tn)
    n_copy = len(caches)
    plan, n_chunks = _shift_plan(caches, t_new)
    assert n_chunks + 2 <= grid[0] * grid[1]
    shift_scratch = []
    if n_copy:
        stage_rows = max(p[0] for p in plan)
        shift_scratch = [pltpu.VMEM((2, stage_rows) + caches[0].shape[3:], caches[0].dtype)] + [
            pltpu.SemaphoreType.DMA((2,))] * 3
    anyspec = pl.BlockSpec(memory_space=pl.ANY)
    mod_map = lambda i, j: ((i * tm) // rows_per_mod, 0, 0)
    return pl.pallas_call(
        functools.partial(_in_proj_kernel, tm=tm, rc=rc, n_qkv_steps=nq, plan=tuple(plan), depth_i=depth_i,
                          t_new=t_new),
        grid=grid,
        in_specs=[pl.BlockSpec((tm, d), lambda i, j: (i, 0)),
                  pl.BlockSpec((None, r, d), mod_map),
                  pl.BlockSpec((None, r, d), mod_map),
                  pl.BlockSpec((1, d), lambda i, j: (0, 0)),
                  pl.BlockSpec((d, tn), lambda i, j: (0, j)),
                  pl.BlockSpec((d, DT_PAD), lambda i, j: (0, 0))] + [anyspec] * n_copy,
        out_specs=[pl.BlockSpec((spb, tm, LANES), lambda i, j: (jnp.minimum(j, nq - 1), i, 0)),
                   pl.BlockSpec((tm, tn), lambda i, j: (i, jnp.maximum(j - nq, 0))),
                   pl.BlockSpec((tm, DT_PAD), lambda i, j: (i, 0))] + [anyspec] * n_copy,
        out_shape=[jax.ShapeDtypeStruct((N_SLABS, t, LANES), F32),
                   jax.ShapeDtypeStruct((t, N_REST), F32),
                   jax.ShapeDtypeStruct((t, DT_PAD), F32)]
        + [jax.ShapeDtypeStruct(c.shape[1:], c.dtype) for c in caches],
        scratch_shapes=[pltpu.VMEM((tm, d), BF16)] + shift_scratch,
        compiler_params=_cparams("arbitrary", "arbitrary"),
        name="in_proj",
    )(x2d, sc3, sh3, g_pre.reshape(1, d), w_main, w_dt, *caches)


def _attn_prompt_kernel(*refs, tb):
    ng = A_GROUPS
    q_refs, kc_refs, kp_refs = refs[0:ng], refs[ng:2 * ng], refs[2 * ng:3 * ng]
    vc_refs, vp_refs = refs[3 * ng:4 * ng], refs[4 * ng:5 * ng]
    za_ref, oa_ref, o_scr, lse_scr = refs[5 * ng:]
    i = pl.program_id(2)
    nk = A_NK
    row = lax.broadcasted_iota(jnp.int32, (nk, 2 * nk), 0)
    col = lax.broadcasted_iota(jnp.int32, (nk, 2 * nk), 1)
    ok = jnp.logical_and(col >= row, col <= row + nk)
    ok_first = jnp.logical_and(ok, col >= jnp.where(i > 0, 0, nk))

    def attend(qh, k2, v2, mask):
        s = jnp.where(mask, _dot_nt(qh, k2), NEG)
        m = jnp.max(s, axis=-1, keepdims=True)
        e = jnp.exp2((s - m) * (ATT_SCALE * 1.4426950408889634))
        z = jnp.sum(e, axis=-1, keepdims=True)
        return _dot(e.astype(BF16), v2) / z, m * ATT_SCALE + jnp.log(z)

    for g in range(ng):
        d = A_DILATIONS[g]
        span = nk * d
        rows_at = lambda start, d=d: pl.ds(start, nk, stride=d) if d > 1 else pl.ds(start, nk)
        tiles = {}

        def tile(ref, start, tiles=tiles, rows_at=rows_at):
            if (id(ref), start) not in tiles:
                tiles[id(ref), start] = ref[rows_at(start), :].astype(BF16)
            return tiles[id(ref), start]

        starts = [sb * span + r for sb in range(tb // span) for r in range(d)]
        loaded = []
        for start in starts:
            if start < span:
                k_prev, v_prev, mask = tile(kp_refs[g], start), tile(vp_refs[g], start), ok_first
            else:
                k_prev, v_prev, mask = tile(kc_refs[g], start - span), tile(vc_refs[g], start - span), ok
            loaded.append((tile(q_refs[g], start),
                           jnp.concatenate([k_prev, tile(kc_refs[g], start)], axis=0),
                           jnp.concatenate([v_prev, tile(vc_refs[g], start)], axis=0), mask))
        results = [attend(*x) for x in loaded]
        for start, (o, lse) in zip(starts, results):
            o_scr[g, rows_at(start), :] = o
            lse_scr[g, rows_at(start), :] = jnp.broadcast_to(lse, (nk, LANES))

    mc = ATTN_MERGE_ROWS

    def merge(c, carry):
        rows = pl.ds(pl.multiple_of(c * mc, mc), mc)
        l0, l1, l2 = lse_scr[0, rows, :], lse_scr[1, rows, :], lse_scr[2, rows, :]
        m = jnp.maximum(jnp.maximum(l0, l1), l2)
        e0, e1, e2 = jnp.exp(l0 - m), jnp.exp(l1 - m), jnp.exp(l2 - m)
        inv = 1.0 / (e0 + e1 + e2)
        oa = (e0 * inv) * o_scr[0, rows, :] + (e1 * inv) * o_scr[1, rows, :] + (e2 * inv) * o_scr[2, rows, :]
        oa_ref[rows, :] = (oa * _silu(za_ref[rows, :])).astype(BF16)
        return carry

    lax.fori_loop(0, tb // mc, merge, 0)


def _attn_prompt(qkv, rest, b, s):
    t = b * s
    nk = A_NK
    tb = nk * A_DILATIONS[-1]
    assert s % tb == 0
    nb = s // tb
    nh = A_GROUPS * A_HEADS

    def cur(kind, g):
        return pl.BlockSpec((None, tb, LANES), lambda bi, h, i: (kind * nh + g * A_HEADS + h, bi * nb + i, 0))

    def prev(kind, g):
        pb = nk * A_DILATIONS[g]
        return pl.BlockSpec((None, pb, LANES),
                            lambda bi, h, i: (kind * nh + g * A_HEADS + h,
                                              jnp.maximum((bi * s + i * tb) // pb - 1, 0), 0))

    groups = range(A_GROUPS)
    in_specs = ([cur(0, g) for g in groups] + [cur(1, g) for g in groups] + [prev(1, g) for g in groups]
                + [cur(2, g) for g in groups] + [prev(2, g) for g in groups]
                + [pl.BlockSpec((tb, LANES), lambda bi, h, i: (bi * nb + i, ZA_OFF // LANES + h))])
    return pl.pallas_call(
        functools.partial(_attn_prompt_kernel, tb=tb),
        grid=(b, A_HEADS, nb),
        in_specs=in_specs,
        out_specs=pl.BlockSpec((tb, LANES), lambda bi, h, i: (bi * nb + i, h)),
        out_shape=jax.ShapeDtypeStruct((t, A_WIDTH), BF16),
        scratch_shapes=[pltpu.VMEM((A_GROUPS, tb, LANES), F32), pltpu.VMEM((A_GROUPS, tb, LANES), F32)],
        compiler_params=_cparams("arbitrary", "arbitrary", "arbitrary"),
        name="attn_p",
    )(*([qkv] * (5 * A_GROUPS)), rest)


def _attn_decode_kernel(qn_ref, kn_ref, vn_ref, za_ref, c0_ref, c1_ref, c2_ref, kv0_any, kv1_any, kv2_any,
                        oa_ref, kv0_ref, kv1_ref, kv2_ref, *, t, chunk):
    del kv0_any, kv1_any, kv2_any
    tile = (A_HEADS, A_HEAD_DIM)
    log2e, ln2 = 1.4426950408889634, 0.6931471805599453

    for g, kv_ref in enumerate((kv0_ref, kv1_ref, kv2_ref)):
        kv_ref[:, 0] = kn_ref[g]
        kv_ref[:, 1] = vn_ref[g]

    def rows_of(g, tq, i0, n):
        d = A_DILATIONS[g]
        if g == A_GROUPS - 1:
            return c2_ref[i0:i0 + n, tq % d, 0], c2_ref[i0:i0 + n, tq % d, 1]
        c_ref = (c0_ref, c1_ref)[g]
        rows = pl.ds(tq % d + i0 * d, n, stride=d) if d > 1 else pl.ds(i0, n)
        return c_ref[rows, 0], c_ref[rows, 1]

    def partial(s, v):
        m = jnp.max(s, axis=0)
        p = jnp.exp2(s - m[None])
        return m, jnp.sum(p, axis=0), jnp.sum(p * v, axis=0)

    def combine(a, b):
        m = jnp.maximum(a[0], b[0])
        fa, fb = jnp.exp2(a[0] - m), jnp.exp2(b[0] - m)
        return m, a[1] * fa + b[1] * fb, a[2] * fa + b[2] * fb

    def attend(g, tq):
        d = A_DILATIONS[g]
        q = qn_ref[g, tq] * (ATT_SCALE * log2e)
        parts = []
        for i0 in range(0, A_NK, chunk):
            k, v = rows_of(g, tq, i0, chunk)
            s = jnp.broadcast_to(jnp.sum(k * q[None], axis=-1, keepdims=True), (chunk,) + tile)
            first_row = tq % d + i0 * d
            if first_row < tq:
                ri = first_row + d * lax.broadcasted_iota(jnp.int32, (chunk,) + tile, 0)
                s = jnp.where(ri >= tq, s, NEG)
            parts.append(partial(s, v))
        js = [tk for tk in range(tq + 1) if (tq - tk) % d == 0]
        s_new = jnp.stack([jnp.broadcast_to(jnp.sum(kn_ref[g, tk] * q, axis=-1, keepdims=True), tile)
                           for tk in js])
        parts.append(partial(s_new, jnp.stack([vn_ref[g, tk] for tk in js])))
        while len(parts) > 1:
            parts = [combine(parts[i], parts[i + 1]) if i + 1 < len(parts) else parts[i]
                     for i in range(0, len(parts), 2)]
        m, l, acc = parts[0]
        return acc / l, m * ln2 + jnp.log(l)

    for tq in range(t):
        outs = [attend(g, tq) for g in range(A_GROUPS)]
        lses = [o[1] for o in outs]
        mm = functools.reduce(jnp.maximum, lses)
        es = [jnp.exp(l - mm) for l in lses]
        inv = 1.0 / sum(es)
        oa = sum((e * inv) * o[0] for e, o in zip(es, outs))
        oa_ref[tq] = oa * _silu(za_ref[tq])


def _attn_decode(qkvn, za5, caches, kv_shifted, depth_i):
    _, ng, b, t, nh, hd = qkvn.shape
    wbs = [c.shape[2] for c in caches]
    d2 = A_DILATIONS[-1]
    assert t == SUBLANES and tuple(wbs) == A_WINDOWS and t <= d2
    assert all(w == A_NK * d for w, d in zip(wbs, A_DILATIONS))
    c2 = caches[-1].reshape(caches[-1].shape[:2] + (wbs[-1] // d2, d2, 2, nh, hd))
    new = lambda kind: pl.BlockSpec((None, ng, None, t, nh, hd), lambda bi: (kind, 0, bi, 0, 0, 0))
    whole = lambda n: pl.BlockSpec((None, None, n, 2, nh, hd), lambda bi: (depth_i, bi, 0, 0, 0, 0))
    tail = lambda n: pl.BlockSpec((None, t, 2, nh, hd), lambda bi: (bi, n // t - 1, 0, 0, 0))
    anyspec = pl.BlockSpec(memory_space=pl.ANY)
    res = pl.pallas_call(
        functools.partial(_attn_decode_kernel, t=t, chunk=DECODE_CHUNK_ROWS),
        grid=(b,),
        in_specs=[new(0), new(1), new(2),
                  pl.BlockSpec((None, t, nh, hd), lambda bi: (bi, 0, 0, 0)),
                  whole(wbs[0]), whole(wbs[1]),
                  pl.BlockSpec((None, None, wbs[-1] // d2, t, 2, nh, hd), lambda bi: (depth_i, bi, 0, 0, 0, 0, 0)),
                  anyspec, anyspec, anyspec],
        out_specs=[pl.BlockSpec((None, t, nh, hd), lambda bi: (bi, 0, 0, 0))] + [tail(w) for w in wbs],
        out_shape=[jax.ShapeDtypeStruct((b, t, nh, hd), F32)]
        + [jax.ShapeDtypeStruct((b, w, 2, nh, hd), F32) for w in wbs],
        input_output_aliases={7: 1, 8: 2, 9: 3},
        compiler_params=_cparams("arbitrary"),
        name="attn_d",
    )(qkvn, qkvn, qkvn, za5, caches[0], caches[1], c2, *kv_shifted)
    return res[0], res[1:]


def _ssd_kernel(xbc_ref, dt_ref, zb_ref, conv0_ref, h0_ref, convw_ref, convb_ref, dtb_ref, alog_ref,
                dskip_ref, gssm_ref, e_ref, yn_ref, convo_ref, ho_ref, ext_scr, st_scr, *, nv):
    q = SSM_CHUNK
    c = pl.program_id(1)
    di = SSM_D_INNER
    gw = di // SSM_GROUPS
    ns = SSM_D_STATE
    hp = SUBLANES
    kw = SSM_CONV_W

    nslab = SSM_CONV_DIM // LANES
    lanes_of = lambda j: slice(j * LANES, (j + 1) * LANES)

    @pl.when(c == 0)
    def _():
        for j in range(nslab):
            ext_scr[j, 0:hp, :] = conv0_ref[:, lanes_of(j)]
        for j in range(di // LANES):
            st_scr[:, lanes_of(j)] = h0_ref[j].T

    rows_c = -(-nv // SUBLANES) * SUBLANES
    for j in range(nslab):
        ext_scr[j, hp:hp + nv, :] = xbc_ref[:, lanes_of(j)]
        if nv != rows_c:
            ext_scr[j, hp + nv:hp + rows_c, :] = jnp.zeros((rows_c - nv, LANES), F32)

    def conv_act(c0, c1):
        pieces = []
        for j in range(c0 // LANES, c1 // LANES):
            acc = convb_ref[:, lanes_of(j)]
            for k in range(kw):
                acc = acc + convw_ref[k:k + 1, lanes_of(j)] * ext_scr[j, pl.ds(hp - (kw - 1) + k, rows_c), :]
            act = _silu(acc)
            if rows_c != q:
                act = jnp.concatenate([act, jnp.zeros((q - rows_c, LANES), F32)], axis=0)
            pieces.append(act)
        return pieces[0] if len(pieces) == 1 else jnp.concatenate(pieces, axis=1)

    row = lax.broadcasted_iota(jnp.int32, (q, q), 0)
    col = lax.broadcasted_iota(jnp.int32, (q, q), 1)
    causal = row >= col
    if nv == q:
        dt_raw = dt_ref[...]
    else:
        dt_raw = jnp.concatenate([dt_ref[...], jnp.zeros((q - nv, DT_PAD), F32)], axis=0)
    dt = jax.nn.softplus(dt_raw + dtb_ref[...])
    if nv != q:
        dt = jnp.where(row < nv, dt, 0.0)
    da = dt * (-jnp.exp(alog_ref[...]))
    tri = jnp.where(causal, 1.0, 0.0).astype(BF16)
    cs = _dot(jnp.concatenate([tri] * 3, axis=1), jnp.concatenate(_split3(da), axis=0)) * 1.4426950408889634
    cs_t = cs.T
    dt_cat = jnp.concatenate(_split3(dt), axis=1)
    cs_cat = jnp.concatenate(_split3(cs), axis=1)
    half = lax.broadcasted_iota(jnp.int32, (q, LANES), 1) < SSM_HEAD_DIM

    for g in range(SSM_GROUPS):
        gs = slice(g * gw, (g + 1) * gw)
        e_g = e_ref[:, gs]
        dt_x = _dot(dt_cat, e_g)
        cs_x = _dot(cs_cat, e_g)
        cs_last = cs_x[q - 1:q, :]
        xs = conv_act(g * gw, (g + 1) * gw)
        xdt = xs * dt_x
        xdt_b = xdt.astype(BF16)
        w_b = (xdt * jnp.exp2(cs_last - cs_x)).astype(BF16)
        bg_t = conv_act(di + g * ns, di + (g + 1) * ns).T.astype(BF16)
        cg = conv_act(di + SSM_GROUPS * ns + g * ns, di + SSM_GROUPS * ns + (g + 1) * ns).astype(BF16)
        cb = _dot(cg, bg_t)
        st = st_scr[:, gs]
        y_off = _dot(cg, st.astype(BF16)) * jnp.exp2(cs_x)
        st_scr[:, gs] = st * jnp.exp2(cs_last) + _dot(bg_t, w_b)
        y_pairs = []
        for pair in range(gw // LANES):
            h0 = (g * gw + pair * LANES) // SSM_HEAD_DIM
            ps = slice(pair * LANES, (pair + 1) * LANES)
            ys = []
            for h in (h0, h0 + 1):
                diff = cs[:, h:h + 1] - cs_t[h:h + 1, :]
                m_h = (cb * jnp.exp2(jnp.where(causal, diff, NEG))).astype(BF16)
                ys.append(_dot(m_h, xdt_b[:, ps]))
            y_pairs.append(jnp.where(half, ys[0], ys[1]))
        y = jnp.concatenate(y_pairs, axis=1) + y_off + xs * dskip_ref[:, gs]
        v = y[0:nv, :] * _silu(zb_ref[:, gs])
        vn = v * lax.rsqrt(jnp.mean(v * v, axis=-1, keepdims=True) + NORM_EPS)
        yn_ref[:, gs] = (vn * gssm_ref[:, gs]).astype(BF16)

    @pl.when(c == pl.num_programs(1) - 1)
    def _():
        for j in range(di // LANES):
            ho_ref[j] = st_scr[:, j * LANES:(j + 1) * LANES].T

    for j in range(nslab):
        tail = ext_scr[j, hp + nv - (kw - 1):hp + nv, :]
        convo_ref[:, lanes_of(j)] = tail
        ext_scr[j, hp - (kw - 1):hp, :] = tail


def _ssd(rest3, dt3, conv0, h0, conv_w, conv_b, dt_bias, a_log, d_skip, g_ssm, nv):
    b, l, _ = rest3.shape
    q = SSM_CHUNK
    nc = l // nv
    assert nv == q or nc == 1
    di, cd, nh = SSM_D_INNER, SSM_CONV_DIM, SSM_HEADS
    hp = SUBLANES
    conv0p = jnp.pad(conv0, ((0, 0), (hp - (SSM_CONV_W - 1), 0), (0, 0)))
    h0v = h0.reshape(b, di // LANES, LANES, SSM_D_STATE)
    pad1 = lambda v: jnp.pad(v.reshape(1, nh), ((0, 0), (0, DT_PAD - nh)))
    expand = np.zeros((DT_PAD, di), np.float32)
    expand[np.arange(di) // SSM_HEAD_DIM, np.arange(di)] = 1.0
    expand = np.tile(expand, (3, 1))
    const = lambda shape: pl.BlockSpec(shape, lambda bi, c: (0,) * len(shape))
    yn, conv_o, h_o = pl.pallas_call(
        functools.partial(_ssd_kernel, nv=nv),
        grid=(b, nc),
        in_specs=[pl.BlockSpec((None, nv, cd), lambda bi, c: (bi, c, XBC_OFF // cd)),
                  pl.BlockSpec((None, nv, DT_PAD), lambda bi, c: (bi, c, 0)),
                  pl.BlockSpec((None, nv, di), lambda bi, c: (bi, c, ZB_OFF // di)),
                  pl.BlockSpec((None, hp, cd), lambda bi, c: (bi, 0, 0)),
                  pl.BlockSpec((None, di // LANES, LANES, SSM_D_STATE), lambda bi, c: (bi, 0, 0, 0)),
                  const((SSM_CONV_W, cd)), const((1, cd)), const((1, DT_PAD)), const((1, DT_PAD)),
                  const((1, di)), const((1, di)), const((3 * DT_PAD, di))],
        out_specs=[pl.BlockSpec((None, nv, di), lambda bi, c: (bi, c, 0)),
                   pl.BlockSpec((None, SSM_CONV_W - 1, cd), lambda bi, c: (bi, 0, 0)),
                   pl.BlockSpec((None, di // LANES, LANES, SSM_D_STATE), lambda bi, c: (bi, 0, 0, 0))],
        out_shape=[jax.ShapeDtypeStruct((b, l, di), BF16),
                   jax.ShapeDtypeStruct((b, SSM_CONV_W - 1, cd), F32),
                   jax.ShapeDtypeStruct((b, di // LANES, LANES, SSM_D_STATE), F32)],
        scratch_shapes=[pltpu.VMEM((cd // LANES, hp + q, LANES), F32),
                        pltpu.VMEM((SSM_D_STATE, di), F32)],
        compiler_params=_cparams("arbitrary", "arbitrary"),
        name=f"ssd{nv}",
    )(rest3, dt3, rest3, conv0p, h0v, conv_w, conv_b.reshape(1, cd), pad1(dt_bias), pad1(a_log),
      jnp.repeat(d_skip, SSM_HEAD_DIM).reshape(1, di), g_ssm.reshape(1, di), jnp.asarray(expand, BF16))
    return yn.reshape(b * l, di), conv_o, h_o.reshape(b, nh, SSM_HEAD_DIM, SSM_D_STATE)


def _sigmoid(x):
    return 0.5 + 0.5 * jnp.tanh(0.5 * x)


def _mix_out_kernel(oa_ref, yn_ref, ra_ref, rb_ref, x_ref, gate_ref, gpost_ref, woa_ref, wob_ref, wo_ref, y_ref):
    p_a = _dot(oa_ref[...], woa_ref[...])
    p_b = _dot(yn_ref[...], wob_ref[...])
    merged = (_sigmoid(ra_ref[...]) * p_a + _sigmoid(rb_ref[...]) * p_b).astype(BF16)
    out = _dot(merged, wo_ref[...])
    nrm = out * lax.rsqrt(jnp.mean(out * out, axis=-1, keepdims=True) + NORM_EPS) * gpost_ref[...]
    y_ref[...] = x_ref[...] + gate_ref[...] * nrm


def _mix_out(oa, yn, rest, x2d, gate3, g_post, w_o_a, w_o_b, w_o, tm, rows_per_mod):
    t, d = x2d.shape
    r = gate3.shape[1]
    row = lambda w, off: pl.BlockSpec((tm, w), lambda i: (i, off // w))
    const = lambda shape: pl.BlockSpec(shape, lambda i: (0, 0), pipeline_mode=pl.Buffered(1))
    return pl.pallas_call(
        _mix_out_kernel,
        grid=(t // tm,),
        in_specs=[row(A_WIDTH, 0), row(d, 0), row(d, RA_OFF), row(d, RB_OFF), row(d, 0),
                  pl.BlockSpec((None, r, d), lambda i: ((i * tm) // rows_per_mod, 0, 0)),
                  pl.BlockSpec((1, d), lambda i: (0, 0)),
                  const((A_WIDTH, d)), const((d, d)), const((d, d))],
        out_specs=row(d, 0),
        out_shape=jax.ShapeDtypeStruct((t, d), F32),
        compiler_params=_cparams("arbitrary"),
        name="mix_out",
    )(oa, yn, rest, rest, x2d, gate3, g_post.reshape(1, d), w_o_a, w_o_b, w_o)


def _window_kernel(k_ref, v_ref, o_ref, *, rows):
    per_row = 2 * A_HEADS
    for kind, src in enumerate((k_ref, v_ref)):
        for h in range(A_HEADS):
            o_ref[pl.ds(kind * A_HEADS + h, rows, stride=per_row), :] = src[h]


def _window_rows(qkv, g, b, s, w):
    nh = A_GROUPS * A_HEADS
    rows = min(w, WINDOW_ROWS)
    per_row = 2 * A_HEADS
    assert w % rows == 0 and (s - w) % rows == 0
    src = lambda kind: pl.BlockSpec(
        (A_HEADS, rows, A_HEAD_DIM),
        lambda bi, i: (kind * A_GROUPS + g, (bi * s + s - w) // rows + i, 0))
    out = pl.pallas_call(
        functools.partial(_window_kernel, rows=rows),
        grid=(b, w // rows),
        in_specs=[src(1), src(2)],
        out_specs=pl.BlockSpec((None, rows * per_row, A_HEAD_DIM), lambda bi, i: (bi, i, 0)),
        out_shape=jax.ShapeDtypeStruct((b, w * per_row, A_HEAD_DIM), F32),
        compiler_params=_cparams("arbitrary", "arbitrary"),
        name=f"window{g}",
    )(qkv, qkv)
    return out.reshape(b, w, 2, A_HEADS, A_HEAD_DIM)


def _prep_w_kernel(tbl_ref, a_ref, b_ref, dt_ref, w_ref, wdt_ref):
    j = pl.program_id(0)
    d = a_ref.shape[1]
    late = tbl_ref[1, j]
    t = LANES

    def put(r, tile_rows):
        for c in range(d // t):
            w_ref[c * t:(c + 1) * t, r * t:(r + 1) * t] = tile_rows[:, c * t:(c + 1) * t].T.astype(BF16)

    @pl.when(late == 0)
    def _():
        for r in range(COL // t):
            put(r, a_ref[r * t:(r + 1) * t, :])

    @pl.when(late != 0)
    def _():
        for r in range(COL // t - 1):
            put(r, a_ref[r * t + SSM_HEADS:(r + 1) * t + SSM_HEADS, :])
        put(COL // t - 1, jnp.concatenate([a_ref[COL - t + SSM_HEADS:COL, :], b_ref[...]], axis=0))

    @pl.when(j == 0)
    def _():
        lane = lax.broadcasted_iota(jnp.int32, (t, t), 1)
        for c in range(d // t):
            blk = dt_ref[:, c * t:(c + 1) * t].T
            wdt_ref[c * t:(c + 1) * t, :] = jnp.where(lane < SSM_HEADS, blk, 0.0).astype(BF16)


def _prep_w(w_t):
    d = w_t.shape[1]
    assert W_DT % COL == 0 and W_RB - W_RA == D_MODEL and DT_PAD == LANES and SSM_HEADS % SUBLANES == 0
    blk = lambda off, n: [off // COL + k for k in range(n // COL)]
    plain = blk(0, N_QKV_COLS) + blk(W_XBC, SSM_CONV_DIM) + blk(W_ZB, SSM_D_INNER)
    late = blk(W_DT, 2 * D_MODEL)
    za = blk(W_ZA, A_WIDTH)
    src = plain + late + za
    is_late = [0] * len(plain) + [1] * len(late) + [0] * len(za)
    nxt = [0] * len(plain) + [(k + 1) * COL // SSM_HEADS for k in late] + [0] * len(za)
    tbl = jnp.asarray(np.array([src, is_late, nxt], np.int32))
    n = len(src)
    grid_spec = pltpu.PrefetchScalarGridSpec(
        num_scalar_prefetch=1,
        grid=(n,),
        in_specs=[pl.BlockSpec((COL, d), lambda j, t: (t[0, j], 0)),
                  pl.BlockSpec((SSM_HEADS, d), lambda j, t: (t[2, j], 0)),
                  pl.BlockSpec((DT_PAD, d), lambda j, t: (W_DT // DT_PAD, 0))],
        out_specs=[pl.BlockSpec((d, COL), lambda j, t: (0, j)),
                   pl.BlockSpec((d, DT_PAD), lambda j, t: (0, 0))])
    return pl.pallas_call(
        _prep_w_kernel,
        grid_spec=grid_spec,
        out_shape=[jax.ShapeDtypeStruct((d, n * COL), BF16), jax.ShapeDtypeStruct((d, DT_PAD), BF16)],
        compiler_params=_cparams("arbitrary"),
        name="prep_w",
    )(tbl, w_t, w_t, w_t)


def _layer(depth_i, x_p, x_s, caches, conv_s, ssm_s, c_p, c_s, w_ada, b_ada, g_pre, g_post, w_in, conv_w,
           conv_b, dt_bias, a_log, d_skip, g_ssm, w_o_a, w_o_b, w_o):
    bp, sp, d = x_p.shape
    bs, ts, _ = x_s.shape
    assert sp % SSM_CHUNK == 0

    w_main, w_dt = _prep_w(jnp.swapaxes(w_in, 0, 1))
    woa_b, wob_b, wo_b = w_o_a.astype(BF16), w_o_b.astype(BF16), w_o.astype(BF16)

    nmod = bp + bs
    mpad = -(-nmod // SUBLANES) * SUBLANES
    c_all = jnp.pad(jnp.concatenate([c_p, c_s], axis=0), ((0, mpad - nmod), (0, 0)))
    mod = _ada(c_all, w_ada, b_ada)
    shift, scale, gate = mod[:, :d], mod[:, d:2 * d], mod[:, 2 * d:]
    per_seq = lambda v: v[:bp].reshape(bp, 1, d)
    per_row = lambda v: jnp.repeat(v[bp:nmod], ts, axis=0).reshape(1, bs * ts, d)

    tp = bp * sp
    xp2 = x_p.reshape(tp, d)
    qkv_p, rest_p, dt_p, *kv_shifted = _in_proj(xp2, per_seq(scale), per_seq(shift), g_pre, w_main, w_dt, IN_PROJ_ROWS, sp,
                                                caches, depth_i, ts)
    oa_p = _attn_prompt(qkv_p, rest_p, bp, sp)
    conv0 = jnp.zeros((bp, SSM_CONV_W - 1, SSM_CONV_DIM), F32)
    ssm0 = jnp.zeros((bp, SSM_HEADS, SSM_HEAD_DIM, SSM_D_STATE), F32)
    yn_p, conv_p, ssm_p = _ssd(rest_p.reshape(bp, sp, N_REST), dt_p.reshape(bp, sp, DT_PAD), conv0, ssm0,
                               conv_w, conv_b, dt_bias, a_log, d_skip, g_ssm, SSM_CHUNK)
    y_p = _mix_out(oa_p, yn_p, rest_p, xp2, per_seq(gate), g_post, woa_b, wob_b, wo_b, MIX_OUT_ROWS, sp).reshape(bp, sp, d)
    kv_p = [_window_rows(qkv_p, g, bp, sp, min(A_WINDOWS[g], sp)) for g in range(A_GROUPS)]

    tsn = bs * ts
    xs2 = x_s.reshape(tsn, d)
    qkv_s, rest_s, dt_s = _in_proj(xs2, per_row(scale), per_row(shift), g_pre, w_main, w_dt, tsn, tsn)[:3]
    qkvn = jnp.transpose(qkv_s.reshape(3, A_GROUPS, A_HEADS, bs, ts, A_HEAD_DIM), (0, 1, 3, 4, 2, 5))
    za5 = rest_s[:, ZA_OFF:ZA_OFF + A_WIDTH].reshape(bs, ts, A_HEADS, A_HEAD_DIM)
    oa5, kv_s = _attn_decode(qkvn, za5, caches, kv_shifted, depth_i)
    oa_s = oa5.reshape(tsn, A_WIDTH).astype(BF16)
    yn_s, conv_sn, ssm_sn = _ssd(rest_s.reshape(bs, ts, N_REST), dt_s.reshape(bs, ts, DT_PAD), conv_s, ssm_s,
                                 conv_w, conv_b, dt_bias, a_log, d_skip, g_ssm, ts)
    y_s = _mix_out(oa_s, yn_s, rest_s, xs2, per_row(gate), g_post, woa_b, wob_b, wo_b, tsn, tsn).reshape(bs, ts, d)
    return y_p, y_s, kv_p, conv_p, ssm_p, list(kv_s), conv_sn, ssm_sn


def kernel(x_prompt, x_sample, cache_a_w128, cache_a_w512, cache_a_w2048, state_conv, state_ssm, c_prompt, c_sample, w_ada, b_ada, g_pre, g_post, w_in, conv_w, conv_b, dt_bias, a_log, d_skip, g_ssm, w_o_a, w_o_b, w_o):
    depth = w_in.shape[0]
    caches = (cache_a_w128, cache_a_w512, cache_a_w2048)
    y_p, y_s = x_prompt, x_sample
    acc = [[] for _ in range(10)]
    for i in range(depth):
        y_p, y_s, kv_p, conv_p, ssm_p, kv_s, conv_s, ssm_s = _layer(
            i, y_p, y_s, caches, state_conv[i], state_ssm[i],
            c_prompt, c_sample, w_ada[i], b_ada[i], g_pre[i], g_post[i], w_in[i], conv_w[i], conv_b[i],
            dt_bias[i], a_log[i], d_skip[i], g_ssm[i], w_o_a[i], w_o_b[i], w_o[i])
        for lst, v in zip(acc, (*kv_p, conv_p, ssm_p, *kv_s, conv_s, ssm_s)):
            lst.append(v)
    return (y_p, y_s, *[jnp.stack(v) for v in acc])
```

```python
import functools

import numpy as np
import jax
import jax.numpy as jnp
from jax import lax
from jax.experimental import pallas as pl
from jax.experimental.pallas import tpu as pltpu

F32 = jnp.float32
BF16 = jnp.bfloat16

D_MODEL = 2048
A_WINDOWS = (128, 512, 2048)
A_DILATIONS = (1, 4, 16)
A_GROUPS = 3
A_HEADS = 8
A_HEAD_DIM = 128
A_WIDTH = A_HEADS * A_HEAD_DIM
A_NK = 128
ATT_SCALE = A_HEAD_DIM ** -0.5
SSM_D_INNER = D_MODEL
SSM_HEAD_DIM = 64
SSM_HEADS = SSM_D_INNER // SSM_HEAD_DIM
SSM_GROUPS = 8
SSM_D_STATE = 128
SSM_CONV_W = 4
SSM_CHUNK = 128
SSM_CONV_DIM = SSM_D_INNER + 2 * SSM_GROUPS * SSM_D_STATE
QKV_WIDTH = A_GROUPS * A_WIDTH
NORM_EPS = 1e-6

LANES = 128
SUBLANES = 8
VMEM_LIMIT = 60 * 1024 * 1024
NEG = -1e30

ADA_COLS = 512
IN_PROJ_ROWS = 1024
PRENORM_ROWS = 256
ATTN_MERGE_ROWS = 256
DECODE_CHUNK_ROWS = 32
MIX_OUT_ROWS = 256
WINDOW_ROWS = 512

COL = 1024
N_QKV_COLS = 3 * QKV_WIDTH
N_SLABS = N_QKV_COLS // LANES
XBC_OFF = 0
ZB_OFF = XBC_OFF + SSM_CONV_DIM
RA_OFF = ZB_OFF + SSM_D_INNER
RB_OFF = RA_OFF + D_MODEL
ZA_OFF = RB_OFF + D_MODEL
N_REST = ZA_OFF + A_WIDTH
DT_PAD = LANES
W_ZA = N_QKV_COLS
W_ZB = W_ZA + A_WIDTH
W_XBC = W_ZB + SSM_D_INNER
W_DT = W_XBC + SSM_CONV_DIM
W_RA = W_DT + SSM_HEADS
W_RB = W_RA + D_MODEL


def _cparams(*sem):
    return pltpu.CompilerParams(dimension_semantics=sem, vmem_limit_bytes=VMEM_LIMIT)


def _dot(a, b):
    return jnp.dot(a, b, preferred_element_type=F32)


def _dot_nt(a, b):
    return lax.dot_general(a, b, (((1,), (1,)), ((), ())), preferred_element_type=F32)


def _split3(x):
    x1 = x.astype(BF16)
    r1 = x - x1.astype(F32)
    x2 = r1.astype(BF16)
    x3 = (r1 - x2.astype(F32)).astype(BF16)
    return x1, x2, x3


def _silu(x):
    h = 0.5 * x
    return h + h * jnp.tanh(h)


def _ada_kernel(c_ref, w_ref, b_ref, o_ref):
    s = _silu(c_ref[...]).astype(BF16)
    o_ref[...] = _dot(s, w_ref[...].astype(BF16)) + b_ref[...]


def _ada(c, w_ada, b_ada):
    m, d = c.shape
    n = w_ada.shape[1]
    tn = ADA_COLS
    return pl.pallas_call(
        _ada_kernel,
        grid=(n // tn,),
        in_specs=[pl.BlockSpec((m, d), lambda j: (0, 0)),
                  pl.BlockSpec((d, tn), lambda j: (0, j)),
                  pl.BlockSpec((1, tn), lambda j: (0, j))],
        out_specs=pl.BlockSpec((m, tn), lambda j: (0, j)),
        out_shape=jax.ShapeDtypeStruct((m, n), F32),
        compiler_params=_cparams("arbitrary"),
        name="ada",
    )(c, w_ada, b_ada.reshape(1, n))


SHIFT_CHUNK_ROWS = 384


def _shift_plan(caches, t):
    plan, first = [], 0
    for c in caches:
        nb, n = c.shape[1], c.shape[2] - t
        cps = next(k for k in range(-(-n // SHIFT_CHUNK_ROWS), n + 1) if n % k == 0)
        plan.append((n // cps, cps, nb, first))
        first += nb * cps
    return plan, first


def _shift_step(step, c_refs, kv_refs, buf, in_sem, out_sem, tail_sem, plan, depth_i, t):
    def where(g, c):
        rows, cps, _, first = plan[g]
        return c // cps, c % cps, (c + first % 2) % 2, rows, cps

    def read(g, c):
        b, ci, slot, rows, _ = where(g, c)
        return pltpu.make_async_copy(c_refs[g].at[depth_i, b, pl.ds(t + ci * rows, rows)],
                                     buf.at[slot, pl.ds(0, rows)], in_sem.at[slot])

    def write(g, c):
        b, ci, slot, rows, _ = where(g, c)
        return pltpu.make_async_copy(buf.at[slot, pl.ds(0, rows)], kv_refs[g].at[b, pl.ds(ci * rows, rows)],
                                     out_sem.at[slot])

    def write_tail(g, c):
        b, _, slot, rows, cps = where(g, c)
        return pltpu.make_async_copy(buf.at[slot, pl.ds(rows - t, t)], kv_refs[g].at[b, pl.ds(cps * rows, t)],
                                     tail_sem.at[slot])

    def ends_sequence(g, c):
        return c % plan[g][1] == plan[g][1] - 1

    def in_group(g, k):
        first, n = plan[g][3], plan[g][2] * plan[g][1]
        return jnp.logical_and(k >= first, k < first + n)

    for g in range(len(plan)):
        @pl.when(in_group(g, step - 2))
        def _(g=g):
            c = step - 2 - plan[g][3]
            write(g, c).wait()
            pl.when(ends_sequence(g, c))(lambda: write_tail(g, c).wait())

    for g in range(len(plan)):
        @pl.when(in_group(g, step))
        def _(g=g):
            read(g, step - plan[g][3]).start()

    for g in range(len(plan)):
        @pl.when(in_group(g, step - 1))
        def _(g=g):
            c = step - 1 - plan[g][3]
            read(g, c).wait()
            write(g, c).start()
            pl.when(ends_sequence(g, c))(lambda: write_tail(g, c).start())


def _in_proj_kernel(*refs, tm, rc, n_qkv_steps, plan, depth_i, t_new):
    n_copy = len(plan)
    x_ref, sc_ref, sh_ref, g_ref, w_ref, wdt_ref = refs[:6]
    c_refs = refs[6:6 + n_copy]
    qkv_ref, rest_ref, dt_ref = refs[6 + n_copy:9 + n_copy]
    kv_refs = refs[9 + n_copy:9 + 2 * n_copy]
    h_scr = refs[9 + 2 * n_copy]
    j = pl.program_id(1)

    if n_copy:
        buf, in_sem, out_sem, tail_sem = refs[10 + 2 * n_copy:]
        step = pl.program_id(0) * pl.num_programs(1) + j
        _shift_step(step, c_refs, kv_refs, buf, in_sem, out_sem, tail_sem, plan, depth_i, t_new)

    @pl.when(j == 0)
    def _():
        per_row = sc_ref.shape[0] != 1

        def body(r, carry):
            rows = pl.ds(pl.multiple_of(r * rc, rc), rc)
            x = x_ref[rows, :]
            y = x * lax.rsqrt(jnp.mean(x * x, axis=-1, keepdims=True) + NORM_EPS) * g_ref[...]
            sc = sc_ref[rows, :] if per_row else sc_ref[...]
            sh = sh_ref[rows, :] if per_row else sh_ref[...]
            h_scr[rows, :] = (y * (1.0 + sc) + sh).astype(BF16)
            return carry

        lax.fori_loop(0, tm // rc, body, 0)
        dt_ref[...] = _dot(h_scr[...], wdt_ref[...])

    res = _dot(h_scr[...], w_ref[...])

    @pl.when(j < n_qkv_steps)
    def _():
        for c in range(COL // LANES):
            qkv_ref[c] = res[:, c * LANES:(c + 1) * LANES]

    @pl.when(j >= n_qkv_steps)
    def _():
        rest_ref[...] = res


def _in_proj(x2d, sc3, sh3, g_pre, w_main, w_dt, tm, rows_per_mod, caches=(), depth_i=0, t_new=0):
    t, d = x2d.shape
    r = sc3.shape[1]
    tn = COL
    rc = min(tm, PRENORM_ROWS)
    nq = N_QKV_COLS // tn
    spb = tn // LANES
    grid = (t // tm, (N_QKV_COLS + N_REST) // tn)
    n_copy = len(caches)
    plan, n_chunks = _shift_plan(caches, t_new)
    assert n_chunks + 2 <= grid[0] * grid[1]
    shift_scratch = []
    if n_copy:
        stage_rows = max(p[0] for p in plan)
        shift_scratch = [pltpu.VMEM((2, stage_rows) + caches[0].shape[3:], caches[0].dtype)] + [
            pltpu.SemaphoreType.DMA((2,))] * 3
    anyspec = pl.BlockSpec(memory_space=pl.ANY)
    mod_map = lambda i, j: ((i * tm) // rows_per_mod, 0, 0)
    return pl.pallas_call(
        functools.partial(_in_proj_kernel, tm=tm, rc=rc, n_qkv_steps=nq, plan=tuple(plan), depth_i=depth_i,
                          t_new=t_new),
        grid=grid,
        in_specs=[pl.BlockSpec((tm, d), lambda i, j: (i, 0)),
                  pl.BlockSpec((None, r, d), mod_map),
                  pl.BlockSpec((None, r, d), mod_map),
                  pl.BlockSpec((1, d), lambda i, j: (0, 0)),
                  pl.BlockSpec((d, tn), lambda i, j: (0, j)),
                  pl.BlockSpec((d, DT_PAD), lambda i, j: (0, 0))] + [anyspec] * n_copy,
        out_specs=[pl.BlockSpec((spb, tm, LANES), lambda i, j: (jnp.minimum(j, nq - 1), i, 0)),
                   pl.BlockSpec((tm, tn), lambda i, j: (i, jnp.maximum(j - nq, 0))),
                   pl.BlockSpec((tm, DT_PAD), lambda i, j: (i, 0))] + [anyspec] * n_copy,
        out_shape=[jax.ShapeDtypeStruct((N_SLABS, t, LANES), F32),
                   jax.ShapeDtypeStruct((t, N_REST), F32),
                   jax.ShapeDtypeStruct((t, DT_PAD), F32)]
        + [jax.ShapeDtypeStruct(c.shape[1:], c.dtype) for c in caches],
        scratch_shapes=[pltpu.VMEM((tm, d), BF16)] + shift_scratch,
        compiler_params=_cparams("arbitrary", "arbitrary"),
        name="in_proj",
    )(x2d, sc3, sh3, g_pre.reshape(1, d), w_main, w_dt, *caches)


def _attn_prompt_kernel(*refs, tb):
    ng = A_GROUPS
    q_refs, kc_refs, kp_refs = refs[0:ng], refs[ng:2 * ng], refs[2 * ng:3 * ng]
    vc_refs, vp_refs = refs[3 * ng:4 * ng], refs[4 * ng:5 * ng]
    za_ref, oa_ref, o_scr, lse_scr = refs[5 * ng:]
    i = pl.program_id(2)
    nk = A_NK
    row = lax.broadcasted_iota(jnp.int32, (nk, 2 * nk), 0)
    col = lax.broadcasted_iota(jnp.int32, (nk, 2 * nk), 1)
    ok = jnp.logical_and(col >= row, col <= row + nk)
    ok_first = jnp.logical_and(ok, col >= jnp.where(i > 0, 0, nk))

    def attend(qh, k2, v2, mask):
        s = jnp.where(mask, _dot_nt(qh, k2), NEG)
        m = jnp.max(s, axis=-1, keepdims=True)
        e = jnp.exp2((s - m) * (ATT_SCALE * 1.4426950408889634))
        z = jnp.sum(e, axis=-1, keepdims=True)
        return _dot(e.astype(BF16), v2) / z, m * ATT_SCALE + jnp.log(z)

    for g in range(ng):
        d = A_DILATIONS[g]
        span = nk * d
        rows_at = lambda start, d=d: pl.ds(start, nk, stride=d) if d > 1 else pl.ds(start, nk)
        tiles = {}

        def tile(ref, start, tiles=tiles, rows_at=rows_at):
            if (id(ref), start) not in tiles:
                tiles[id(ref), start] = ref[rows_at(start), :].astype(BF16)
            return tiles[id(ref), start]

        starts = [sb * span + r for sb in range(tb // span) for r in range(d)]
        loaded = []
        for start in starts:
            if start < span:
                k_prev, v_prev, mask = tile(kp_refs[g], start), tile(vp_refs[g], start), ok_first
            else:
                k_prev, v_prev, mask = tile(kc_refs[g], start - span), tile(vc_refs[g], start - span), ok
            loaded.append((tile(q_refs[g], start),
                           jnp.concatenate([k_prev, tile(kc_refs[g], start)], axis=0),
                           jnp.concatenate([v_prev, tile(vc_refs[g], start)], axis=0), mask))
        results = [attend(*x) for x in loaded]
        for start, (o, lse) in zip(starts, results):
            o_scr[g, rows_at(start), :] = o
            lse_scr[g, rows_at(start), :] = jnp.broadcast_to(lse, (nk, LANES))

    mc = ATTN_MERGE_ROWS

    def merge(c, carry):
        rows = pl.ds(pl.multiple_of(c * mc, mc), mc)
        l0, l1, l2 = lse_scr[0, rows, :], lse_scr[1, rows, :], lse_scr[2, rows, :]
        m = jnp.maximum(jnp.maximum(l0, l1), l2)
        e0, e1, e2 = jnp.exp(l0 - m), jnp.exp(l1 - m), jnp.exp(l2 - m)
        inv = 1.0 / (e0 + e1 + e2)
        oa = (e0 * inv) * o_scr[0, rows, :] + (e1 * inv) * o_scr[1, rows, :] + (e2 * inv) * o_scr[2, rows, :]
        oa_ref[rows, :] = (oa * _silu(za_ref[rows, :])).astype(BF16)
        return carry

    lax.fori_loop(0, tb // mc, merge, 0)


def _attn_prompt(qkv, rest, b, s):
    t = b * s
    nk = A_NK
    tb = nk * A_DILATIONS[-1]
    assert s % tb == 0
    nb = s // tb
    nh = A_GROUPS * A_HEADS

    def cur(kind, g):
        return pl.BlockSpec((None, tb, LANES), lambda bi, h, i: (kind * nh + g * A_HEADS + h, bi * nb + i, 0))

    def prev(kind, g):
        pb = nk * A_DILATIONS[g]
        return pl.BlockSpec((None, pb, LANES),
                            lambda bi, h, i: (kind * nh + g * A_HEADS + h,
                                              jnp.maximum((bi * s + i * tb) // pb - 1, 0), 0))

    groups = range(A_GROUPS)
    in_specs = ([cur(0, g) for g in groups] + [cur(1, g) for g in groups] + [prev(1, g) for g in groups]
                + [cur(2, g) for g in groups] + [prev(2, g) for g in groups]
                + [pl.BlockSpec((tb, LANES), lambda bi, h, i: (bi * nb + i, ZA_OFF // LANES + h))])
    return pl.pallas_call(
        functools.partial(_attn_prompt_kernel, tb=tb),
        grid=(b, A_HEADS, nb),
        in_specs=in_specs,
        out_specs=pl.BlockSpec((tb, LANES), lambda bi, h, i: (bi * nb + i, h)),
        out_shape=jax.ShapeDtypeStruct((t, A_WIDTH), BF16),
        scratch_shapes=[pltpu.VMEM((A_GROUPS, tb, LANES), F32), pltpu.VMEM((A_GROUPS, tb, LANES), F32)],
        compiler_params=_cparams("arbitrary", "arbitrary", "arbitrary"),
        name="attn_p",
    )(*([qkv] * (5 * A_GROUPS)), rest)


def _attn_decode_kernel(qn_ref, kn_ref, vn_ref, za_ref, c0_ref, c1_ref, c2_ref, kv0_any, kv1_any, kv2_any,
                        oa_ref, kv0_ref, kv1_ref, kv2_ref, *, t, chunk):
    del kv0_any, kv1_any, kv2_any
    tile = (A_HEADS, A_HEAD_DIM)
    log2e, ln2 = 1.4426950408889634, 0.6931471805599453

    for g, kv_ref in enumerate((kv0_ref, kv1_ref, kv2_ref)):
        kv_ref[:, 0] = kn_ref[g]
        kv_ref[:, 1] = vn_ref[g]

    def rows_of(g, tq, i0, n):
        d = A_DILATIONS[g]
        if g == A_GROUPS - 1:
            return c2_ref[i0:i0 + n, tq % d, 0], c2_ref[i0:i0 + n, tq % d, 1]
        c_ref = (c0_ref, c1_ref)[g]
        rows = pl.ds(tq % d + i0 * d, n, stride=d) if d > 1 else pl.ds(i0, n)
        return c_ref[rows, 0], c_ref[rows, 1]

    def partial(s, v):
        m = jnp.max(s, axis=0)
        p = jnp.exp2(s - m[None])
        return m, jnp.sum(p, axis=0), jnp.sum(p * v, axis=0)

    def combine(a, b):
        m = jnp.maximum(a[0], b[0])
        fa, fb = jnp.exp2(a[0] - m), jnp.exp2(b[0] - m)
        return m, a[1] * fa + b[1] * fb, a[2] * fa + b[2] * fb

    def attend(g, tq):
        d = A_DILATIONS[g]
        q = qn_ref[g, tq] * (ATT_SCALE * log2e)
        parts = []
        for i0 in range(0, A_NK, chunk):
            k, v = rows_of(g, tq, i0, chunk)
            s = jnp.broadcast_to(jnp.sum(k * q[None], axis=-1, keepdims=True), (chunk,) + tile)
            first_row = tq % d + i0 * d
            if first_row < tq:
                ri = first_row + d * lax.broadcasted_iota(jnp.int32, (chunk,) + tile, 0)
                s = jnp.where(ri >= tq, s, NEG)
            parts.append(partial(s, v))
        js = [tk for tk in range(tq + 1) if (tq - tk) % d == 0]
        s_new = jnp.stack([jnp.broadcast_to(jnp.sum(kn_ref[g, tk] * q, axis=-1, keepdims=True), tile)
                           for tk in js])
        parts.append(partial(s_new, jnp.stack([vn_ref[g, tk] for tk in js])))
        while len(parts) > 1:
            parts = [combine(parts[i], parts[i + 1]) if i + 1 < len(parts) else parts[i]
                     for i in range(0, len(parts), 2)]
        m, l, acc = parts[0]
        return acc / l, m * ln2 + jnp.log(l)

    for tq in range(t):
        outs = [attend(g, tq) for g in range(A_GROUPS)]
        lses = [o[1] for o in outs]
        mm = functools.reduce(jnp.maximum, lses)
        es = [jnp.exp(l - mm) for l in lses]
        inv = 1.0 / sum(es)
        oa = sum((e * inv) * o[0] for e, o in zip(es, outs))
        oa_ref[tq] = oa * _silu(za_ref[tq])


def _attn_decode(qkvn, za5, caches, kv_shifted, depth_i):
    _, ng, b, t, nh, hd = qkvn.shape
    wbs = [c.shape[2] for c in caches]
    d2 = A_DILATIONS[-1]
    assert t == SUBLANES and tuple(wbs) == A_WINDOWS and t <= d2
    assert all(w == A_NK * d for w, d in zip(wbs, A_DILATIONS))
    c2 = caches[-1].reshape(caches[-1].shape[:2] + (wbs[-1] // d2, d2, 2, nh, hd))
    new = lambda kind: pl.BlockSpec((None, ng, None, t, nh, hd), lambda bi: (kind, 0, bi, 0, 0, 0))
    whole = lambda n: pl.BlockSpec((None, None, n, 2, nh, hd), lambda bi: (depth_i, bi, 0, 0, 0, 0))
    tail = lambda n: pl.BlockSpec((None, t, 2, nh, hd), lambda bi: (bi, n // t - 1, 0, 0, 0))
    anyspec = pl.BlockSpec(memory_space=pl.ANY)
    res = pl.pallas_call(
        functools.partial(_attn_decode_kernel, t=t, chunk=DECODE_CHUNK_ROWS),
        grid=(b,),
        in_specs=[new(0), new(1), new(2),
                  pl.BlockSpec((None, t, nh, hd), lambda bi: (bi, 0, 0, 0)),
                  whole(wbs[0]), whole(wbs[1]),
                  pl.BlockSpec((None, None, wbs[-1] // d2, t, 2, nh, hd), lambda bi: (depth_i, bi, 0, 0, 0, 0, 0)),
                  anyspec, anyspec, anyspec],
        out_specs=[pl.BlockSpec((None, t, nh, hd), lambda bi: (bi, 0, 0, 0))] + [tail(w) for w in wbs],
        out_shape=[jax.ShapeDtypeStruct((b, t, nh, hd), F32)]
        + [jax.ShapeDtypeStruct((b, w, 2, nh, hd), F32) for w in wbs],
        input_output_aliases={7: 1, 8: 2, 9: 3},
        compiler_params=_cparams("arbitrary"),
        name="attn_d",
    )(qkvn, qkvn, qkvn, za5, caches[0], caches[1], c2, *kv_shifted)
    return res[0], res[1:]


def _ssd_kernel(*refs, nv, plan, depth_i, t_new):
    n_copy = len(plan)
    (xbc_ref, dt_ref, zb_ref, conv0_ref, h0_ref, convw_ref, convb_ref, dtb_ref, alog_ref, dskip_ref, gssm_ref,
     e_ref) = refs[:12]
    c_refs = refs[12:12 + n_copy]
    yn_ref, convo_ref, ho_ref = refs[12 + n_copy:15 + n_copy]
    kv_refs = refs[15 + n_copy:15 + 2 * n_copy]
    ext_scr, st_scr = refs[15 + 2 * n_copy:17 + 2 * n_copy]
    q = SSM_CHUNK
    c = pl.program_id(1)
    if n_copy:
        buf, in_sem, out_sem, tail_sem = refs[17 + 2 * n_copy:]
        step = pl.program_id(0) * pl.num_programs(1) + c
        _shift_step(step, c_refs, kv_refs, buf, in_sem, out_sem, tail_sem, plan, depth_i, t_new)
    di = SSM_D_INNER
    gw = di // SSM_GROUPS
    ns = SSM_D_STATE
    hp = SUBLANES
    kw = SSM_CONV_W

    nslab = SSM_CONV_DIM // LANES
    lanes_of = lambda j: slice(j * LANES, (j + 1) * LANES)

    @pl.when(c == 0)
    def _():
        for j in range(nslab):
            ext_scr[j, 0:hp, :] = conv0_ref[:, lanes_of(j)]
        for j in range(di // LANES):
            st_scr[:, lanes_of(j)] = h0_ref[j].T

    rows_c = -(-nv // SUBLANES) * SUBLANES
    for j in range(nslab):
        ext_scr[j, hp:hp + nv, :] = xbc_ref[:, lanes_of(j)]
        if nv != rows_c:
            ext_scr[j, hp + nv:hp + rows_c, :] = jnp.zeros((rows_c - nv, LANES), F32)

    def conv_act(c0, c1):
        pieces = []
        for j in range(c0 // LANES, c1 // LANES):
            acc = convb_ref[:, lanes_of(j)]
            for k in range(kw):
                acc = acc + convw_ref[k:k + 1, lanes_of(j)] * ext_scr[j, pl.ds(hp - (kw - 1) + k, rows_c), :]
            act = _silu(acc)
            if rows_c != q:
                act = jnp.concatenate([act, jnp.zeros((q - rows_c, LANES), F32)], axis=0)
            pieces.append(act)
        return pieces[0] if len(pieces) == 1 else jnp.concatenate(pieces, axis=1)

    row = lax.broadcasted_iota(jnp.int32, (q, q), 0)
    col = lax.broadcasted_iota(jnp.int32, (q, q), 1)
    causal = row >= col
    if nv == q:
        dt_raw = dt_ref[...]
    else:
        dt_raw = jnp.concatenate([dt_ref[...], jnp.zeros((q - nv, DT_PAD), F32)], axis=0)
    dt = jax.nn.softplus(dt_raw + dtb_ref[...])
    if nv != q:
        dt = jnp.where(row < nv, dt, 0.0)
    da = dt * (-jnp.exp(alog_ref[...]))
    tri = jnp.where(causal, 1.0, 0.0).astype(BF16)
    cs = _dot(jnp.concatenate([tri] * 3, axis=1), jnp.concatenate(_split3(da), axis=0)) * 1.4426950408889634
    cs_t = cs.T
    dt_cat = jnp.concatenate(_split3(dt), axis=1)
    cs_cat = jnp.concatenate(_split3(cs), axis=1)
    half = lax.broadcasted_iota(jnp.int32, (q, LANES), 1) < SSM_HEAD_DIM

    for g in range(SSM_GROUPS):
        gs = slice(g * gw, (g + 1) * gw)
        e_g = e_ref[:, gs]
        dt_x = _dot(dt_cat, e_g)
        cs_x = _dot(cs_cat, e_g)
        cs_last = cs_x[q - 1:q, :]
        xs = conv_act(g * gw, (g + 1) * gw)
        xdt = xs * dt_x
        xdt_b = xdt.astype(BF16)
        w_b = (xdt * jnp.exp2(cs_last - cs_x)).astype(BF16)
        bg_t = conv_act(di + g * ns, di + (g + 1) * ns).T.astype(BF16)
        cg = conv_act(di + SSM_GROUPS * ns + g * ns, di + SSM_GROUPS * ns + (g + 1) * ns).astype(BF16)
        cb = _dot(cg, bg_t)
        st = st_scr[:, gs]
        y_off = _dot(cg, st.astype(BF16)) * jnp.exp2(cs_x)
        st_scr[:, gs] = st * jnp.exp2(cs_last) + _dot(bg_t, w_b)
        y_pairs = []
        for pair in range(gw // LANES):
            h0 = (g * gw + pair * LANES) // SSM_HEAD_DIM
            ps = slice(pair * LANES, (pair + 1) * LANES)
            ys = []
            for h in (h0, h0 + 1):
                diff = cs[:, h:h + 1] - cs_t[h:h + 1, :]
                m_h = (cb * jnp.exp2(jnp.where(causal, diff, NEG))).astype(BF16)
                ys.append(_dot(m_h, xdt_b[:, ps]))
            y_pairs.append(jnp.where(half, ys[0], ys[1]))
        y = jnp.concatenate(y_pairs, axis=1) + y_off + xs * dskip_ref[:, gs]
        v = y[0:nv, :] * _silu(zb_ref[:, gs])
        vn = v * lax.rsqrt(jnp.mean(v * v, axis=-1, keepdims=True) + NORM_EPS)
        yn_ref[:, gs] = (vn * gssm_ref[:, gs]).astype(BF16)

    @pl.when(c == pl.num_programs(1) - 1)
    def _():
        for j in range(di // LANES):
            ho_ref[j] = st_scr[:, j * LANES:(j + 1) * LANES].T

    for j in range(nslab):
        tail = ext_scr[j, hp + nv - (kw - 1):hp + nv, :]
        convo_ref[:, lanes_of(j)] = tail
        ext_scr[j, hp - (kw - 1):hp, :] = tail


def _ssd(rest3, dt3, conv0, h0, conv_w, conv_b, dt_bias, a_log, d_skip, g_ssm, nv, caches=(), depth_i=0, t_new=0):
    b, l, _ = rest3.shape
    q = SSM_CHUNK
    nc = l // nv
    assert nv == q or nc == 1
    plan, n_chunks = _shift_plan(caches, t_new)
    assert n_chunks + 2 <= b * nc
    n_copy = len(caches)
    shift_scratch = []
    if n_copy:
        stage_rows = max(p[0] for p in plan)
        shift_scratch = [pltpu.VMEM((2, stage_rows) + caches[0].shape[3:], caches[0].dtype)] + [
            pltpu.SemaphoreType.DMA((2,))] * 3
    anyspec = pl.BlockSpec(memory_space=pl.ANY)
    di, cd, nh = SSM_D_INNER, SSM_CONV_DIM, SSM_HEADS
    hp = SUBLANES
    conv0p = jnp.pad(conv0, ((0, 0), (hp - (SSM_CONV_W - 1), 0), (0, 0)))
    h0v = h0.reshape(b, di // LANES, LANES, SSM_D_STATE)
    pad1 = lambda v: jnp.pad(v.reshape(1, nh), ((0, 0), (0, DT_PAD - nh)))
    expand = np.zeros((DT_PAD, di), np.float32)
    expand[np.arange(di) // SSM_HEAD_DIM, np.arange(di)] = 1.0
    expand = np.tile(expand, (3, 1))
    const = lambda shape: pl.BlockSpec(shape, lambda bi, c: (0,) * len(shape))
    yn, conv_o, h_o, *shifted = pl.pallas_call(
        functools.partial(_ssd_kernel, nv=nv, plan=tuple(plan), depth_i=depth_i, t_new=t_new),
        grid=(b, nc),
        in_specs=[pl.BlockSpec((None, nv, cd), lambda bi, c: (bi, c, XBC_OFF // cd)),
                  pl.BlockSpec((None, nv, DT_PAD), lambda bi, c: (bi, c, 0)),
                  pl.BlockSpec((None, nv, di), lambda bi, c: (bi, c, ZB_OFF // di)),
                  pl.BlockSpec((None, hp, cd), lambda bi, c: (bi, 0, 0)),
                  pl.BlockSpec((None, di // LANES, LANES, SSM_D_STATE), lambda bi, c: (bi, 0, 0, 0)),
                  const((SSM_CONV_W, cd)), const((1, cd)), const((1, DT_PAD)), const((1, DT_PAD)),
                  const((1, di)), const((1, di)), const((3 * DT_PAD, di))] + [anyspec] * n_copy,
        out_specs=[pl.BlockSpec((None, nv, di), lambda bi, c: (bi, c, 0)),
                   pl.BlockSpec((None, SSM_CONV_W - 1, cd), lambda bi, c: (bi, 0, 0)),
                   pl.BlockSpec((None, di // LANES, LANES, SSM_D_STATE), lambda bi, c: (bi, 0, 0, 0))]
        + [anyspec] * n_copy,
        out_shape=[jax.ShapeDtypeStruct((b, l, di), BF16),
                   jax.ShapeDtypeStruct((b, SSM_CONV_W - 1, cd), F32),
                   jax.ShapeDtypeStruct((b, di // LANES, LANES, SSM_D_STATE), F32)]
        + [jax.ShapeDtypeStruct(c.shape[1:], c.dtype) for c in caches],
        scratch_shapes=[pltpu.VMEM((cd // LANES, hp + q, LANES), F32),
                        pltpu.VMEM((SSM_D_STATE, di), F32)] + shift_scratch,
        compiler_params=_cparams("arbitrary", "arbitrary"),
        name=f"ssd{nv}",
    )(rest3, dt3, rest3, conv0p, h0v, conv_w, conv_b.reshape(1, cd), pad1(dt_bias), pad1(a_log),
      jnp.repeat(d_skip, SSM_HEAD_DIM).reshape(1, di), g_ssm.reshape(1, di), jnp.asarray(expand, BF16), *caches)
    return (yn.reshape(b * l, di), conv_o, h_o.reshape(b, nh, SSM_HEAD_DIM, SSM_D_STATE), *shifted)


def _sigmoid(x):
    return 0.5 + 0.5 * jnp.tanh(0.5 * x)


def _mix_out_kernel(oa_ref, yn_ref, ra_ref, rb_ref, x_ref, gate_ref, gpost_ref, woa_ref, wob_ref, wo_ref, y_ref):
    p_a = _dot(oa_ref[...], woa_ref[...])
    p_b = _dot(yn_ref[...], wob_ref[...])
    merged = (_sigmoid(ra_ref[...]) * p_a + _sigmoid(rb_ref[...]) * p_b).astype(BF16)
    out = _dot(merged, wo_ref[...])
    nrm = out * lax.rsqrt(jnp.mean(out * out, axis=-1, keepdims=True) + NORM_EPS) * gpost_ref[...]
    y_ref[...] = x_ref[...] + gate_ref[...] * nrm


def _mix_out(oa, yn, rest, x2d, gate3, g_post, w_o_a, w_o_b, w_o, tm, rows_per_mod):
    t, d = x2d.shape
    r = gate3.shape[1]
    row = lambda w, off: pl.BlockSpec((tm, w), lambda i: (i, off // w))
    const = lambda shape: pl.BlockSpec(shape, lambda i: (0, 0), pipeline_mode=pl.Buffered(1))
    return pl.pallas_call(
        _mix_out_kernel,
        grid=(t // tm,),
        in_specs=[row(A_WIDTH, 0), row(d, 0), row(d, RA_OFF), row(d, RB_OFF), row(d, 0),
                  pl.BlockSpec((None, r, d), lambda i: ((i * tm) // rows_per_mod, 0, 0)),
                  pl.BlockSpec((1, d), lambda i: (0, 0)),
                  const((A_WIDTH, d)), const((d, d)), const((d, d))],
        out_specs=row(d, 0),
        out_shape=jax.ShapeDtypeStruct((t, d), F32),
        compiler_params=_cparams("arbitrary"),
        name="mix_out",
    )(oa, yn, rest, rest, x2d, gate3, g_post.reshape(1, d), w_o_a, w_o_b, w_o)


def _window_kernel(k_ref, v_ref, o_ref, *, rows):
    per_row = 2 * A_HEADS
    for kind, src in enumerate((k_ref, v_ref)):
        for h in range(A_HEADS):
            o_ref[pl.ds(kind * A_HEADS + h, rows, stride=per_row), :] = src[h]


def _window_rows(qkv, g, b, s, w):
    nh = A_GROUPS * A_HEADS
    rows = min(w, WINDOW_ROWS)
    per_row = 2 * A_HEADS
    assert w % rows == 0 and (s - w) % rows == 0
    src = lambda kind: pl.BlockSpec(
        (A_HEADS, rows, A_HEAD_DIM),
        lambda bi, i: (kind * A_GROUPS + g, (bi * s + s - w) // rows + i, 0))
    out = pl.pallas_call(
        functools.partial(_window_kernel, rows=rows),
        grid=(b, w // rows),
        in_specs=[src(1), src(2)],
        out_specs=pl.BlockSpec((None, rows * per_row, A_HEAD_DIM), lambda bi, i: (bi, i, 0)),
        out_shape=jax.ShapeDtypeStruct((b, w * per_row, A_HEAD_DIM), F32),
        compiler_params=_cparams("arbitrary", "arbitrary"),
        name=f"window{g}",
    )(qkv, qkv)
    return out.reshape(b, w, 2, A_HEADS, A_HEAD_DIM)


def _prep_w_kernel(tbl_ref, a_ref, b_ref, dt_ref, w_ref, wdt_ref):
    j = pl.program_id(0)
    d = a_ref.shape[1]
    late = tbl_ref[1, j]
    t = LANES

    def put(r, tile_rows):
        for c in range(d // t):
            w_ref[c * t:(c + 1) * t, r * t:(r + 1) * t] = tile_rows[:, c * t:(c + 1) * t].T.astype(BF16)

    @pl.when(late == 0)
    def _():
        for r in range(COL // t):
            put(r, a_ref[r * t:(r + 1) * t, :])

    @pl.when(late != 0)
    def _():
        for r in range(COL // t - 1):
            put(r, a_ref[r * t + SSM_HEADS:(r + 1) * t + SSM_HEADS, :])
        put(COL // t - 1, jnp.concatenate([a_ref[COL - t + SSM_HEADS:COL, :], b_ref[...]], axis=0))

    @pl.when(j == 0)
    def _():
        lane = lax.broadcasted_iota(jnp.int32, (t, t), 1)
        for c in range(d // t):
            blk = dt_ref[:, c * t:(c + 1) * t].T
            wdt_ref[c * t:(c + 1) * t, :] = jnp.where(lane < SSM_HEADS, blk, 0.0).astype(BF16)


def _prep_w(w_t):
    d = w_t.shape[1]
    assert W_DT % COL == 0 and W_RB - W_RA == D_MODEL and DT_PAD == LANES and SSM_HEADS % SUBLANES == 0
    blk = lambda off, n: [off // COL + k for k in range(n // COL)]
    plain = blk(0, N_QKV_COLS) + blk(W_XBC, SSM_CONV_DIM) + blk(W_ZB, SSM_D_INNER)
    late = blk(W_DT, 2 * D_MODEL)
    za = blk(W_ZA, A_WIDTH)
    src = plain + late + za
    is_late = [0] * len(plain) + [1] * len(late) + [0] * len(za)
    nxt = [0] * len(plain) + [(k + 1) * COL // SSM_HEADS for k in late] + [0] * len(za)
    tbl = jnp.asarray(np.array([src, is_late, nxt], np.int32))
    n = len(src)
    grid_spec = pltpu.PrefetchScalarGridSpec(
        num_scalar_prefetch=1,
        grid=(n,),
        in_specs=[pl.BlockSpec((COL, d), lambda j, t: (t[0, j], 0)),
                  pl.BlockSpec((SSM_HEADS, d), lambda j, t: (t[2, j], 0)),
                  pl.BlockSpec((DT_PAD, d), lambda j, t: (W_DT // DT_PAD, 0))],
        out_specs=[pl.BlockSpec((d, COL), lambda j, t: (0, j)),
                   pl.BlockSpec((d, DT_PAD), lambda j, t: (0, 0))])
    return pl.pallas_call(
        _prep_w_kernel,
        grid_spec=grid_spec,
        out_shape=[jax.ShapeDtypeStruct((d, n * COL), BF16), jax.ShapeDtypeStruct((d, DT_PAD), BF16)],
        compiler_params=_cparams("arbitrary"),
        name="prep_w",
    )(tbl, w_t, w_t, w_t)


def _layer(depth_i, x_p, x_s, caches, conv_s, ssm_s, c_p, c_s, w_ada, b_ada, g_pre, g_post, w_in, conv_w,
           conv_b, dt_bias, a_log, d_skip, g_ssm, w_o_a, w_o_b, w_o):
    bp, sp, d = x_p.shape
    bs, ts, _ = x_s.shape
    assert sp % SSM_CHUNK == 0

    w_main, w_dt = _prep_w(jnp.swapaxes(w_in, 0, 1))
    woa_b, wob_b, wo_b = w_o_a.astype(BF16), w_o_b.astype(BF16), w_o.astype(BF16)

    nmod = bp + bs
    mpad = -(-nmod // SUBLANES) * SUBLANES
    c_all = jnp.pad(jnp.concatenate([c_p, c_s], axis=0), ((0, mpad - nmod), (0, 0)))
    mod = _ada(c_all, w_ada, b_ada)
    shift, scale, gate = mod[:, :d], mod[:, d:2 * d], mod[:, 2 * d:]
    per_seq = lambda v: v[:bp].reshape(bp, 1, d)
    per_row = lambda v: jnp.repeat(v[bp:nmod], ts, axis=0).reshape(1, bs * ts, d)

    tp = bp * sp
    xp2 = x_p.reshape(tp, d)
    qkv_p, rest_p, dt_p, *kv_wide = _in_proj(xp2, per_seq(scale), per_seq(shift), g_pre, w_main, w_dt, IN_PROJ_ROWS, sp,
                                             caches[-1:], depth_i, ts)
    oa_p = _attn_prompt(qkv_p, rest_p, bp, sp)
    conv0 = jnp.zeros((bp, SSM_CONV_W - 1, SSM_CONV_DIM), F32)
    ssm0 = jnp.zeros((bp, SSM_HEADS, SSM_HEAD_DIM, SSM_D_STATE), F32)
    yn_p, conv_p, ssm_p, *kv_small = _ssd(rest_p.reshape(bp, sp, N_REST), dt_p.reshape(bp, sp, DT_PAD), conv0, ssm0,
                                          conv_w, conv_b, dt_bias, a_log, d_skip, g_ssm, SSM_CHUNK,
                                          caches[:-1], depth_i, ts)
    kv_shifted = kv_small + kv_wide
    y_p = _mix_out(oa_p, yn_p, rest_p, xp2, per_seq(gate), g_post, woa_b, wob_b, wo_b, MIX_OUT_ROWS, sp).reshape(bp, sp, d)
    kv_p = [_window_rows(qkv_p, g, bp, sp, min(A_WINDOWS[g], sp)) for g in range(A_GROUPS)]

    tsn = bs * ts
    xs2 = x_s.reshape(tsn, d)
    qkv_s, rest_s, dt_s = _in_proj(xs2, per_row(scale), per_row(shift), g_pre, w_main, w_dt, tsn, tsn)[:3]
    qkvn = jnp.transpose(qkv_s.reshape(3, A_GROUPS, A_HEADS, bs, ts, A_HEAD_DIM), (0, 1, 3, 4, 2, 5))
    za5 = rest_s[:, ZA_OFF:ZA_OFF + A_WIDTH].reshape(bs, ts, A_HEADS, A_HEAD_DIM)
    oa5, kv_s = _attn_decode(qkvn, za5, caches, kv_shifted, depth_i)
    oa_s = oa5.reshape(tsn, A_WIDTH).astype(BF16)
    yn_s, conv_sn, ssm_sn = _ssd(rest_s.reshape(bs, ts, N_REST), dt_s.reshape(bs, ts, DT_PAD), conv_s, ssm_s,
                                 conv_w, conv_b, dt_bias, a_log, d_skip, g_ssm, ts)
    y_s = _mix_out(oa_s, yn_s, rest_s, xs2, per_row(gate), g_post, woa_b, wob_b, wo_b, tsn, tsn).reshape(bs, ts, d)
    return y_p, y_s, kv_p, conv_p, ssm_p, list(kv_s), conv_sn, ssm_sn


def kernel(x_prompt, x_sample, cache_a_w128, cache_a_w512, cache_a_w2048, state_conv, state_ssm, c_prompt, c_sample, w_ada, b_ada, g_pre, g_post, w_in, conv_w, conv_b, dt_bias, a_log, d_skip, g_ssm, w_o_a, w_o_b, w_o):
    depth = w_in.shape[0]
    caches = (cache_a_w128, cache_a_w512, cache_a_w2048)
    y_p, y_s = x_prompt, x_sample
    acc = [[] for _ in range(10)]
    for i in range(depth):
        y_p, y_s, kv_p, conv_p, ssm_p, kv_s, conv_s, ssm_s = _layer(
            i, y_p, y_s, caches, state_conv[i], state_ssm[i],
            c_prompt, c_sample, w_ada[i], b_ada[i], g_pre[i], g_post[i], w_in[i], conv_w[i], conv_b[i],
            dt_bias[i], a_log[i], d_skip[i], g_ssm[i], w_o_a[i], w_o_b[i], w_o[i])
        for lst, v in zip(acc, (*kv_p, conv_p, ssm_p, *kv_s, conv_s, ssm_s)):
            lst.append(v)
    return (y_p, y_s, *[jnp.stack(v) for v in acc])
```

```python
import functools

import numpy as np
import jax
import jax.numpy as jnp
from jax import lax
from jax.experimental import pallas as pl
from jax.experimental.pallas import tpu as pltpu

F32 = jnp.float32
BF16 = jnp.bfloat16

D_MODEL = 2048
A_WINDOWS = (128, 512, 2048)
A_DILATIONS = (1, 4, 16)
A_GROUPS = 3
A_HEADS = 8
A_HEAD_DIM = 128
A_WIDTH = A_HEADS * A_HEAD_DIM
A_NK = 128
ATT_SCALE = A_HEAD_DIM ** -0.5
SSM_D_INNER = D_MODEL
SSM_HEAD_DIM = 64
SSM_HEADS = SSM_D_INNER // SSM_HEAD_DIM
SSM_GROUPS = 8
SSM_D_STATE = 128
SSM_CONV_W = 4
SSM_CHUNK = 128
SSM_CONV_DIM = SSM_D_INNER + 2 * SSM_GROUPS * SSM_D_STATE
QKV_WIDTH = A_GROUPS * A_WIDTH
NORM_EPS = 1e-6

LANES = 128
SUBLANES = 8
VMEM_LIMIT = 62 * 1024 * 1024
NEG = -1e30

ADA_COLS = 512
IN_PROJ_ROWS = 1024
PRENORM_ROWS = 256
ATTN_MERGE_ROWS = 256
DECODE_CHUNK_ROWS = 32
MIX_OUT_ROWS = 256
WINDOW_ROWS = 512

COL = 1024
N_QKV_COLS = 3 * QKV_WIDTH
N_SLABS = N_QKV_COLS // LANES
XBC_OFF = 0
ZB_OFF = XBC_OFF + SSM_CONV_DIM
RA_OFF = ZB_OFF + SSM_D_INNER
RB_OFF = RA_OFF + D_MODEL
ZA_OFF = RB_OFF + D_MODEL
N_REST = ZA_OFF + A_WIDTH
DT_PAD = LANES
W_ZA = N_QKV_COLS
W_ZB = W_ZA + A_WIDTH
W_XBC = W_ZB + SSM_D_INNER
W_DT = W_XBC + SSM_CONV_DIM
W_RA = W_DT + SSM_HEADS
W_RB = W_RA + D_MODEL


def _cparams(*sem):
    return pltpu.CompilerParams(dimension_semantics=sem, vmem_limit_bytes=VMEM_LIMIT)


def _dot(a, b):
    return jnp.dot(a, b, preferred_element_type=F32)


def _dot_nt(a, b):
    return lax.dot_general(a, b, (((1,), (1,)), ((), ())), preferred_element_type=F32)


def _split3(x):
    x1 = x.astype(BF16)
    r1 = x - x1.astype(F32)
    x2 = r1.astype(BF16)
    x3 = (r1 - x2.astype(F32)).astype(BF16)
    return x1, x2, x3


def _silu(x):
    h = 0.5 * x
    return h + h * jnp.tanh(h)


def _ada_kernel(c_ref, w_ref, b_ref, o_ref):
    s = _silu(c_ref[...]).astype(BF16)
    o_ref[...] = _dot(s, w_ref[...].astype(BF16)) + b_ref[...]


def _ada(c, w_ada, b_ada):
    m, d = c.shape
    n = w_ada.shape[1]
    tn = ADA_COLS
    return pl.pallas_call(
        _ada_kernel,
        grid=(n // tn,),
        in_specs=[pl.BlockSpec((m, d), lambda j: (0, 0)),
                  pl.BlockSpec((d, tn), lambda j: (0, j)),
                  pl.BlockSpec((1, tn), lambda j: (0, j))],
        out_specs=pl.BlockSpec((m, tn), lambda j: (0, j)),
        out_shape=jax.ShapeDtypeStruct((m, n), F32),
        compiler_params=_cparams("arbitrary"),
        name="ada",
    )(c, w_ada, b_ada.reshape(1, n))


SHIFT_CHUNK_ROWS = 256


def _shift_plan(caches, t):
    plan, first = [], 0
    for c in caches:
        nb, n = c.shape[1], c.shape[2] - t
        cps = next(k for k in range(-(-n // SHIFT_CHUNK_ROWS), n + 1) if n % k == 0)
        plan.append((n // cps, cps, nb, first))
        first += nb * cps
    return plan, first


def _shift_step(step, c_refs, kv_refs, buf, in_sem, out_sem, tail_sem, plan, depth_i, t):
    def where(g, c):
        rows, cps, _, first = plan[g]
        return c // cps, c % cps, (c + first % 2) % 2, rows, cps

    def read(g, c):
        b, ci, slot, rows, _ = where(g, c)
        return pltpu.make_async_copy(c_refs[g].at[depth_i, b, pl.ds(t + ci * rows, rows)],
                                     buf.at[slot, pl.ds(0, rows)], in_sem.at[slot])

    def write(g, c):
        b, ci, slot, rows, _ = where(g, c)
        return pltpu.make_async_copy(buf.at[slot, pl.ds(0, rows)], kv_refs[g].at[b, pl.ds(ci * rows, rows)],
                                     out_sem.at[slot])

    def write_tail(g, c):
        b, _, slot, rows, cps = where(g, c)
        return pltpu.make_async_copy(buf.at[slot, pl.ds(rows - t, t)], kv_refs[g].at[b, pl.ds(cps * rows, t)],
                                     tail_sem.at[slot])

    def ends_sequence(g, c):
        return c % plan[g][1] == plan[g][1] - 1

    def in_group(g, k):
        first, n = plan[g][3], plan[g][2] * plan[g][1]
        return jnp.logical_and(k >= first, k < first + n)

    for g in range(len(plan)):
        @pl.when(in_group(g, step - 2))
        def _(g=g):
            c = step - 2 - plan[g][3]
            write(g, c).wait()
            pl.when(ends_sequence(g, c))(lambda: write_tail(g, c).wait())

    for g in range(len(plan)):
        @pl.when(in_group(g, step))
        def _(g=g):
            read(g, step - plan[g][3]).start()

    for g in range(len(plan)):
        @pl.when(in_group(g, step - 1))
        def _(g=g):
            c = step - 1 - plan[g][3]
            read(g, c).wait()
            write(g, c).start()
            pl.when(ends_sequence(g, c))(lambda: write_tail(g, c).start())


def _in_proj_kernel(*refs, tm, rc, n_qkv_steps, plan, depth_i, t_new):
    n_copy = len(plan)
    x_ref, sc_ref, sh_ref, g_ref, w_ref, wdt_ref = refs[:6]
    c_refs = refs[6:6 + n_copy]
    qkv_ref, rest_ref, dt_ref = refs[6 + n_copy:9 + n_copy]
    kv_refs = refs[9 + n_copy:9 + 2 * n_copy]
    h_scr, wbuf, wsem = refs[9 + 2 * n_copy:12 + 2 * n_copy]
    j = pl.program_id(1)
    nj = pl.num_programs(1)
    step = pl.program_id(0) * nj + j
    tn = wbuf.shape[2]

    def wcopy(jj, slot):
        return pltpu.make_async_copy(w_ref.at[:, pl.ds(pl.multiple_of(jj * tn, tn), tn)], wbuf.at[slot],
                                     wsem.at[slot])

    @pl.when(step == 0)
    def _():
        wcopy(0, 0).start()
        wcopy(1, 1).start()

    @pl.when(step + 2 < pl.num_programs(0) * nj)
    def _():
        wcopy((j + 2) % nj, (step + 2) % 3).start()

    if n_copy:
        buf, in_sem, out_sem, tail_sem = refs[12 + 2 * n_copy:]
        _shift_step(step, c_refs, kv_refs, buf, in_sem, out_sem, tail_sem, plan, depth_i, t_new)

    @pl.when(j == 0)
    def _():
        per_row = sc_ref.shape[0] != 1

        def body(r, carry):
            rows = pl.ds(pl.multiple_of(r * rc, rc), rc)
            x = x_ref[rows, :]
            y = x * lax.rsqrt(jnp.mean(x * x, axis=-1, keepdims=True) + NORM_EPS) * g_ref[...]
            sc = sc_ref[rows, :] if per_row else sc_ref[...]
            sh = sh_ref[rows, :] if per_row else sh_ref[...]
            h_scr[rows, :] = (y * (1.0 + sc) + sh).astype(BF16)
            return carry

        lax.fori_loop(0, tm // rc, body, 0)
        dt_ref[...] = _dot(h_scr[...], wdt_ref[...])

    wcopy(j, step % 3).wait()
    res = _dot(h_scr[...], wbuf[step % 3])

    @pl.when(j < n_qkv_steps)
    def _():
        for c in range(COL // LANES):
            qkv_ref[c] = res[:, c * LANES:(c + 1) * LANES]

    @pl.when(j >= n_qkv_steps)
    def _():
        rest_ref[...] = res


def _in_proj(x2d, sc3, sh3, g_pre, w_main, w_dt, tm, rows_per_mod, caches=(), depth_i=0, t_new=0):
    t, d = x2d.shape
    r = sc3.shape[1]
    tn = COL
    rc = min(tm, PRENORM_ROWS)
    nq = N_QKV_COLS // tn
    spb = tn // LANES
    grid = (t // tm, (N_QKV_COLS + N_REST) // tn)
    n_copy = len(caches)
    plan, n_chunks = _shift_plan(caches, t_new)
    assert n_chunks + 2 <= grid[0] * grid[1]
    shift_scratch = []
    if n_copy:
        stage_rows = max(p[0] for p in plan)
        shift_scratch = [pltpu.VMEM((2, stage_rows) + caches[0].shape[3:], caches[0].dtype)] + [
            pltpu.SemaphoreType.DMA((2,))] * 3
    anyspec = pl.BlockSpec(memory_space=pl.ANY)
    mod_map = lambda i, j: ((i * tm) // rows_per_mod, 0, 0)
    return pl.pallas_call(
        functools.partial(_in_proj_kernel, tm=tm, rc=rc, n_qkv_steps=nq, plan=tuple(plan), depth_i=depth_i,
                          t_new=t_new),
        grid=grid,
        in_specs=[pl.BlockSpec((tm, d), lambda i, j: (i, 0)),
                  pl.BlockSpec((None, r, d), mod_map),
                  pl.BlockSpec((None, r, d), mod_map),
                  pl.BlockSpec((1, d), lambda i, j: (0, 0)),
                  anyspec,
                  pl.BlockSpec((d, DT_PAD), lambda i, j: (0, 0))] + [anyspec] * n_copy,
        out_specs=[pl.BlockSpec((spb, tm, LANES), lambda i, j: (jnp.minimum(j, nq - 1), i, 0)),
                   pl.BlockSpec((tm, tn), lambda i, j: (i, jnp.maximum(j - nq, 0))),
                   pl.BlockSpec((tm, DT_PAD), lambda i, j: (i, 0))] + [anyspec] * n_copy,
        out_shape=[jax.ShapeDtypeStruct((N_SLABS, t, LANES), F32),
                   jax.ShapeDtypeStruct((t, N_REST), F32),
                   jax.ShapeDtypeStruct((t, DT_PAD), F32)]
        + [jax.ShapeDtypeStruct(c.shape[1:], c.dtype) for c in caches],
        scratch_shapes=[pltpu.VMEM((tm, d), BF16), pltpu.VMEM((3, d, tn), BF16),
                        pltpu.SemaphoreType.DMA((3,))] + shift_scratch,
        compiler_params=_cparams("arbitrary", "arbitrary"),
        name="in_proj",
    )(x2d, sc3, sh3, g_pre.reshape(1, d), w_main, w_dt, *caches)


def _attn_prompt_kernel(*refs, tb):
    ng = A_GROUPS
    q_refs, kc_refs, kp_refs = refs[0:ng], refs[ng:2 * ng], refs[2 * ng:3 * ng]
    vc_refs, vp_refs = refs[3 * ng:4 * ng], refs[4 * ng:5 * ng]
    za_ref, oa_ref, o_scr, lse_scr = refs[5 * ng:]
    i = pl.program_id(2)
    nk = A_NK
    row = lax.broadcasted_iota(jnp.int32, (nk, 2 * nk), 0)
    col = lax.broadcasted_iota(jnp.int32, (nk, 2 * nk), 1)
    ok = jnp.logical_and(col >= row, col <= row + nk)
    ok_first = jnp.logical_and(ok, col >= jnp.where(i > 0, 0, nk))

    def attend(qh, k2, v2, mask):
        s = jnp.where(mask, _dot_nt(qh, k2), NEG)
        m = jnp.max(s, axis=-1, keepdims=True)
        e = jnp.exp2((s - m) * (ATT_SCALE * 1.4426950408889634))
        z = jnp.sum(e, axis=-1, keepdims=True)
        return _dot(e.astype(BF16), v2) / z, m * ATT_SCALE + jnp.log(z)

    for g in range(ng):
        d = A_DILATIONS[g]
        span = nk * d
        rows_at = lambda start, d=d: pl.ds(start, nk, stride=d) if d > 1 else pl.ds(start, nk)
        tiles = {}

        def tile(ref, start, tiles=tiles, rows_at=rows_at):
            if (id(ref), start) not in tiles:
                tiles[id(ref), start] = ref[rows_at(start), :].astype(BF16)
            return tiles[id(ref), start]

        starts = [sb * span + r for sb in range(tb // span) for r in range(d)]
        loaded = []
        for start in starts:
            if start < span:
                k_prev, v_prev, mask = tile(kp_refs[g], start), tile(vp_refs[g], start), ok_first
            else:
                k_prev, v_prev, mask = tile(kc_refs[g], start - span), tile(vc_refs[g], start - span), ok
            loaded.append((tile(q_refs[g], start),
                           jnp.concatenate([k_prev, tile(kc_refs[g], start)], axis=0),
                           jnp.concatenate([v_prev, tile(vc_refs[g], start)], axis=0), mask))
        results = [attend(*x) for x in loaded]
        for start, (o, lse) in zip(starts, results):
            o_scr[g, rows_at(start), :] = o
            lse_scr[g, rows_at(start), :] = jnp.broadcast_to(lse, (nk, LANES))

    mc = ATTN_MERGE_ROWS

    def merge(c, carry):
        rows = pl.ds(pl.multiple_of(c * mc, mc), mc)
        l0, l1, l2 = lse_scr[0, rows, :], lse_scr[1, rows, :], lse_scr[2, rows, :]
        m = jnp.maximum(jnp.maximum(l0, l1), l2)
        e0, e1, e2 = jnp.exp(l0 - m), jnp.exp(l1 - m), jnp.exp(l2 - m)
        inv = 1.0 / (e0 + e1 + e2)
        oa = (e0 * inv) * o_scr[0, rows, :] + (e1 * inv) * o_scr[1, rows, :] + (e2 * inv) * o_scr[2, rows, :]
        oa_ref[rows, :] = (oa * _silu(za_ref[rows, :])).astype(BF16)
        return carry

    lax.fori_loop(0, tb // mc, merge, 0)


def _attn_prompt(qkv, rest, b, s):
    t = b * s
    nk = A_NK
    tb = nk * A_DILATIONS[-1]
    assert s % tb == 0
    nb = s // tb
    nh = A_GROUPS * A_HEADS

    def cur(kind, g):
        return pl.BlockSpec((None, tb, LANES), lambda bi, h, i: (kind * nh + g * A_HEADS + h, bi * nb + i, 0))

    def prev(kind, g):
        pb = nk * A_DILATIONS[g]
        return pl.BlockSpec((None, pb, LANES),
                            lambda bi, h, i: (kind * nh + g * A_HEADS + h,
                                              jnp.maximum((bi * s + i * tb) // pb - 1, 0), 0))

    groups = range(A_GROUPS)
    in_specs = ([cur(0, g) for g in groups] + [cur(1, g) for g in groups] + [prev(1, g) for g in groups]
                + [cur(2, g) for g in groups] + [prev(2, g) for g in groups]
                + [pl.BlockSpec((tb, LANES), lambda bi, h, i: (bi * nb + i, ZA_OFF // LANES + h))])
    return pl.pallas_call(
        functools.partial(_attn_prompt_kernel, tb=tb),
        grid=(b, A_HEADS, nb),
        in_specs=in_specs,
        out_specs=pl.BlockSpec((tb, LANES), lambda bi, h, i: (bi * nb + i, h)),
        out_shape=jax.ShapeDtypeStruct((t, A_WIDTH), BF16),
        scratch_shapes=[pltpu.VMEM((A_GROUPS, tb, LANES), F32), pltpu.VMEM((A_GROUPS, tb, LANES), F32)],
        compiler_params=_cparams("arbitrary", "arbitrary", "arbitrary"),
        name="attn_p",
    )(*([qkv] * (5 * A_GROUPS)), rest)


def _attn_decode_kernel(qn_ref, kn_ref, vn_ref, za_ref, c0_ref, c1_ref, c2_ref, kv0_any, kv1_any, kv2_any,
                        oa_ref, kv0_ref, kv1_ref, kv2_ref, *, t, chunk):
    del kv0_any, kv1_any, kv2_any
    tile = (A_HEADS, A_HEAD_DIM)
    log2e, ln2 = 1.4426950408889634, 0.6931471805599453

    for g, kv_ref in enumerate((kv0_ref, kv1_ref, kv2_ref)):
        kv_ref[:, 0] = kn_ref[g]
        kv_ref[:, 1] = vn_ref[g]

    def rows_of(g, tq, i0, n):
        d = A_DILATIONS[g]
        if g == A_GROUPS - 1:
            return c2_ref[i0:i0 + n, tq % d, 0], c2_ref[i0:i0 + n, tq % d, 1]
        c_ref = (c0_ref, c1_ref)[g]
        rows = pl.ds(tq % d + i0 * d, n, stride=d) if d > 1 else pl.ds(i0, n)
        return c_ref[rows, 0], c_ref[rows, 1]

    def partial(s, v):
        m = jnp.max(s, axis=0)
        p = jnp.exp2(s - m[None])
        return m, jnp.sum(p, axis=0), jnp.sum(p * v, axis=0)

    def combine(a, b):
        m = jnp.maximum(a[0], b[0])
        fa, fb = jnp.exp2(a[0] - m), jnp.exp2(b[0] - m)
        return m, a[1] * fa + b[1] * fb, a[2] * fa + b[2] * fb

    def attend(g, tq):
        d = A_DILATIONS[g]
        q = qn_ref[g, tq] * (ATT_SCALE * log2e)
        parts = []
        for i0 in range(0, A_NK, chunk):
            k, v = rows_of(g, tq, i0, chunk)
            s = jnp.broadcast_to(jnp.sum(k * q[None], axis=-1, keepdims=True), (chunk,) + tile)
            first_row = tq % d + i0 * d
            if first_row < tq:
                ri = first_row + d * lax.broadcasted_iota(jnp.int32, (chunk,) + tile, 0)
                s = jnp.where(ri >= tq, s, NEG)
            parts.append(partial(s, v))
        js = [tk for tk in range(tq + 1) if (tq - tk) % d == 0]
        s_new = jnp.stack([jnp.broadcast_to(jnp.sum(kn_ref[g, tk] * q, axis=-1, keepdims=True), tile)
                           for tk in js])
        parts.append(partial(s_new, jnp.stack([vn_ref[g, tk] for tk in js])))
        while len(parts) > 1:
            parts = [combine(parts[i], parts[i + 1]) if i + 1 < len(parts) else parts[i]
                     for i in range(0, len(parts), 2)]
        m, l, acc = parts[0]
        return acc / l, m * ln2 + jnp.log(l)

    for tq in range(t):
        outs = [attend(g, tq) for g in range(A_GROUPS)]
        lses = [o[1] for o in outs]
        mm = functools.reduce(jnp.maximum, lses)
        es = [jnp.exp(l - mm) for l in lses]
        inv = 1.0 / sum(es)
        oa = sum((e * inv) * o[0] for e, o in zip(es, outs))
        oa_ref[tq] = oa * _silu(za_ref[tq])


def _attn_decode(qkvn, za5, caches, kv_shifted, depth_i):
    _, ng, b, t, nh, hd = qkvn.shape
    wbs = [c.shape[2] for c in caches]
    d2 = A_DILATIONS[-1]
    assert t == SUBLANES and tuple(wbs) == A_WINDOWS and t <= d2
    assert all(w == A_NK * d for w, d in zip(wbs, A_DILATIONS))
    c2 = caches[-1].reshape(caches[-1].shape[:2] + (wbs[-1] // d2, d2, 2, nh, hd))
    new = lambda kind: pl.BlockSpec((None, ng, None, t, nh, hd), lambda bi: (kind, 0, bi, 0, 0, 0))
    whole = lambda n: pl.BlockSpec((None, None, n, 2, nh, hd), lambda bi: (depth_i, bi, 0, 0, 0, 0))
    tail = lambda n: pl.BlockSpec((None, t, 2, nh, hd), lambda bi: (bi, n // t - 1, 0, 0, 0))
    anyspec = pl.BlockSpec(memory_space=pl.ANY)
    res = pl.pallas_call(
        functools.partial(_attn_decode_kernel, t=t, chunk=DECODE_CHUNK_ROWS),
        grid=(b,),
        in_specs=[new(0), new(1), new(2),
                  pl.BlockSpec((None, t, nh, hd), lambda bi: (bi, 0, 0, 0)),
                  whole(wbs[0]), whole(wbs[1]),
                  pl.BlockSpec((None, None, wbs[-1] // d2, t, 2, nh, hd), lambda bi: (depth_i, bi, 0, 0, 0, 0, 0)),
                  anyspec, anyspec, anyspec],
        out_specs=[pl.BlockSpec((None, t, nh, hd), lambda bi: (bi, 0, 0, 0))] + [tail(w) for w in wbs],
        out_shape=[jax.ShapeDtypeStruct((b, t, nh, hd), F32)]
        + [jax.ShapeDtypeStruct((b, w, 2, nh, hd), F32) for w in wbs],
        input_output_aliases={7: 1, 8: 2, 9: 3},
        compiler_params=_cparams("arbitrary"),
        name="attn_d",
    )(qkvn, qkvn, qkvn, za5, caches[0], caches[1], c2, *kv_shifted)
    return res[0], res[1:]


def _ssd_kernel(*refs, nv, plan, depth_i, t_new):
    n_copy = len(plan)
    (xbc_ref, dt_ref, zb_ref, conv0_ref, h0_ref, convw_ref, convb_ref, dtb_ref, alog_ref, dskip_ref, gssm_ref,
     e_ref) = refs[:12]
    c_refs = refs[12:12 + n_copy]
    yn_ref, convo_ref, ho_ref = refs[12 + n_copy:15 + n_copy]
    kv_refs = refs[15 + n_copy:15 + 2 * n_copy]
    ext_scr, st_scr = refs[15 + 2 * n_copy:17 + 2 * n_copy]
    q = SSM_CHUNK
    c = pl.program_id(1)
    if n_copy:
        buf, in_sem, out_sem, tail_sem = refs[17 + 2 * n_copy:]
        step = pl.program_id(0) * pl.num_programs(1) + c
        _shift_step(step, c_refs, kv_refs, buf, in_sem, out_sem, tail_sem, plan, depth_i, t_new)
    di = SSM_D_INNER
    gw = di // SSM_GROUPS
    ns = SSM_D_STATE
    hp = SUBLANES
    kw = SSM_CONV_W

    nslab = SSM_CONV_DIM // LANES
    lanes_of = lambda j: slice(j * LANES, (j + 1) * LANES)

    @pl.when(c == 0)
    def _():
        for j in range(nslab):
            ext_scr[j, 0:hp, :] = conv0_ref[:, lanes_of(j)]
        for j in range(di // LANES):
            st_scr[:, lanes_of(j)] = h0_ref[j].T

    rows_c = -(-nv // SUBLANES) * SUBLANES
    for j in range(nslab):
        ext_scr[j, hp:hp + nv, :] = xbc_ref[:, lanes_of(j)]
        if nv != rows_c:
            ext_scr[j, hp + nv:hp + rows_c, :] = jnp.zeros((rows_c - nv, LANES), F32)

    def conv_act(c0, c1):
        pieces = []
        for j in range(c0 // LANES, c1 // LANES):
            acc = convb_ref[:, lanes_of(j)]
            for k in range(kw):
                acc = acc + convw_ref[k:k + 1, lanes_of(j)] * ext_scr[j, pl.ds(hp - (kw - 1) + k, rows_c), :]
            act = _silu(acc)
            if rows_c != q:
                act = jnp.concatenate([act, jnp.zeros((q - rows_c, LANES), F32)], axis=0)
            pieces.append(act)
        return pieces[0] if len(pieces) == 1 else jnp.concatenate(pieces, axis=1)

    row = lax.broadcasted_iota(jnp.int32, (q, q), 0)
    col = lax.broadcasted_iota(jnp.int32, (q, q), 1)
    causal = row >= col
    if nv == q:
        dt_raw = dt_ref[...]
    else:
        dt_raw = jnp.concatenate([dt_ref[...], jnp.zeros((q - nv, DT_PAD), F32)], axis=0)
    dt = jax.nn.softplus(dt_raw + dtb_ref[...])
    if nv != q:
        dt = jnp.where(row < nv, dt, 0.0)
    da = dt * (-jnp.exp(alog_ref[...]))
    tri = jnp.where(causal, 1.0, 0.0).astype(BF16)
    cs = _dot(jnp.concatenate([tri] * 3, axis=1), jnp.concatenate(_split3(da), axis=0)) * 1.4426950408889634
    cs_t = cs.T
    dt_cat = jnp.concatenate(_split3(dt), axis=1)
    cs_cat = jnp.concatenate(_split3(cs), axis=1)
    half = lax.broadcasted_iota(jnp.int32, (q, LANES), 1) < SSM_HEAD_DIM

    for g in range(SSM_GROUPS):
        gs = slice(g * gw, (g + 1) * gw)
        e_g = e_ref[:, gs]
        dt_x = _dot(dt_cat, e_g)
        cs_x = _dot(cs_cat, e_g)
        cs_last = cs_x[q - 1:q, :]
        xs = conv_act(g * gw, (g + 1) * gw)
        xdt = xs * dt_x
        xdt_b = xdt.astype(BF16)
        w_b = (xdt * jnp.exp2(cs_last - cs_x)).astype(BF16)
        bg_t = conv_act(di + g * ns, di + (g + 1) * ns).T.astype(BF16)
        cg = conv_act(di + SSM_GROUPS * ns + g * ns, di + SSM_GROUPS * ns + (g + 1) * ns).astype(BF16)
        cb = _dot(cg, bg_t)
        st = st_scr[:, gs]
        y_off = _dot(cg, st.astype(BF16)) * jnp.exp2(cs_x)
        st_scr[:, gs] = st * jnp.exp2(cs_last) + _dot(bg_t, w_b)
        y_pairs = []
        for pair in range(gw // LANES):
            h0 = (g * gw + pair * LANES) // SSM_HEAD_DIM
            ps = slice(pair * LANES, (pair + 1) * LANES)
            ys = []
            for h in (h0, h0 + 1):
                diff = cs[:, h:h + 1] - cs_t[h:h + 1, :]
                m_h = (cb * jnp.exp2(jnp.where(causal, diff, NEG))).astype(BF16)
                ys.append(_dot(m_h, xdt_b[:, ps]))
            y_pairs.append(jnp.where(half, ys[0], ys[1]))
        y = jnp.concatenate(y_pairs, axis=1) + y_off + xs * dskip_ref[:, gs]
        v = y[0:nv, :] * _silu(zb_ref[:, gs])
        vn = v * lax.rsqrt(jnp.mean(v * v, axis=-1, keepdims=True) + NORM_EPS)
        yn_ref[:, gs] = (vn * gssm_ref[:, gs]).astype(BF16)

    @pl.when(c == pl.num_programs(1) - 1)
    def _():
        for j in range(di // LANES):
            ho_ref[j] = st_scr[:, j * LANES:(j + 1) * LANES].T

    for j in range(nslab):
        tail = ext_scr[j, hp + nv - (kw - 1):hp + nv, :]
        convo_ref[:, lanes_of(j)] = tail
        ext_scr[j, hp - (kw - 1):hp, :] = tail


def _ssd(rest3, dt3, conv0, h0, conv_w, conv_b, dt_bias, a_log, d_skip, g_ssm, nv, caches=(), depth_i=0, t_new=0):
    b, l, _ = rest3.shape
    q = SSM_CHUNK
    nc = l // nv
    assert nv == q or nc == 1
    plan, n_chunks = _shift_plan(caches, t_new)
    assert n_chunks + 2 <= b * nc
    n_copy = len(caches)
    shift_scratch = []
    if n_copy:
        stage_rows = max(p[0] for p in plan)
        shift_scratch = [pltpu.VMEM((2, stage_rows) + caches[0].shape[3:], caches[0].dtype)] + [
            pltpu.SemaphoreType.DMA((2,))] * 3
    anyspec = pl.BlockSpec(memory_space=pl.ANY)
    di, cd, nh = SSM_D_INNER, SSM_CONV_DIM, SSM_HEADS
    hp = SUBLANES
    conv0p = jnp.pad(conv0, ((0, 0), (hp - (SSM_CONV_W - 1), 0), (0, 0)))
    h0v = h0.reshape(b, di // LANES, LANES, SSM_D_STATE)
    pad1 = lambda v: jnp.pad(v.reshape(1, nh), ((0, 0), (0, DT_PAD - nh)))
    expand = np.zeros((DT_PAD, di), np.float32)
    expand[np.arange(di) // SSM_HEAD_DIM, np.arange(di)] = 1.0
    expand = np.tile(expand, (3, 1))
    const = lambda shape: pl.BlockSpec(shape, lambda bi, c: (0,) * len(shape))
    yn, conv_o, h_o, *shifted = pl.pallas_call(
        functools.partial(_ssd_kernel, nv=nv, plan=tuple(plan), depth_i=depth_i, t_new=t_new),
        grid=(b, nc),
        in_specs=[pl.BlockSpec((None, nv, cd), lambda bi, c: (bi, c, XBC_OFF // cd)),
                  pl.BlockSpec((None, nv, DT_PAD), lambda bi, c: (bi, c, 0)),
                  pl.BlockSpec((None, nv, di), lambda bi, c: (bi, c, ZB_OFF // di)),
                  pl.BlockSpec((None, hp, cd), lambda bi, c: (bi, 0, 0)),
                  pl.BlockSpec((None, di // LANES, LANES, SSM_D_STATE), lambda bi, c: (bi, 0, 0, 0)),
                  const((SSM_CONV_W, cd)), const((1, cd)), const((1, DT_PAD)), const((1, DT_PAD)),
                  const((1, di)), const((1, di)), const((3 * DT_PAD, di))] + [anyspec] * n_copy,
        out_specs=[pl.BlockSpec((None, nv, di), lambda bi, c: (bi, c, 0)),
                   pl.BlockSpec((None, SSM_CONV_W - 1, cd), lambda bi, c: (bi, 0, 0)),
                   pl.BlockSpec((None, di // LANES, LANES, SSM_D_STATE), lambda bi, c: (bi, 0, 0, 0))]
        + [anyspec] * n_copy,
        out_shape=[jax.ShapeDtypeStruct((b, l, di), BF16),
                   jax.ShapeDtypeStruct((b, SSM_CONV_W - 1, cd), F32),
                   jax.ShapeDtypeStruct((b, di // LANES, LANES, SSM_D_STATE), F32)]
        + [jax.ShapeDtypeStruct(c.shape[1:], c.dtype) for c in caches],
        scratch_shapes=[pltpu.VMEM((cd // LANES, hp + q, LANES), F32),
                        pltpu.VMEM((SSM_D_STATE, di), F32)] + shift_scratch,
        compiler_params=_cparams("arbitrary", "arbitrary"),
        name=f"ssd{nv}",
    )(rest3, dt3, rest3, conv0p, h0v, conv_w, conv_b.reshape(1, cd), pad1(dt_bias), pad1(a_log),
      jnp.repeat(d_skip, SSM_HEAD_DIM).reshape(1, di), g_ssm.reshape(1, di), jnp.asarray(expand, BF16), *caches)
    return (yn.reshape(b * l, di), conv_o, h_o.reshape(b, nh, SSM_HEAD_DIM, SSM_D_STATE), *shifted)


def _sigmoid(x):
    return 0.5 + 0.5 * jnp.tanh(0.5 * x)


def _mix_out_kernel(oa_ref, yn_ref, ra_ref, rb_ref, x_ref, gate_ref, gpost_ref, woa_ref, wob_ref, wo_ref, y_ref):
    p_a = _dot(oa_ref[...], woa_ref[...])
    p_b = _dot(yn_ref[...], wob_ref[...])
    merged = (_sigmoid(ra_ref[...]) * p_a + _sigmoid(rb_ref[...]) * p_b).astype(BF16)
    out = _dot(merged, wo_ref[...])
    nrm = out * lax.rsqrt(jnp.mean(out * out, axis=-1, keepdims=True) + NORM_EPS) * gpost_ref[...]
    y_ref[...] = x_ref[...] + gate_ref[...] * nrm


def _mix_out(oa, yn, rest, x2d, gate3, g_post, w_o_a, w_o_b, w_o, tm, rows_per_mod):
    t, d = x2d.shape
    r = gate3.shape[1]
    row = lambda w, off: pl.BlockSpec((tm, w), lambda i: (i, off // w))
    const = lambda shape: pl.BlockSpec(shape, lambda i: (0, 0), pipeline_mode=pl.Buffered(1))
    return pl.pallas_call(
        _mix_out_kernel,
        grid=(t // tm,),
        in_specs=[row(A_WIDTH, 0), row(d, 0), row(d, RA_OFF), row(d, RB_OFF), row(d, 0),
                  pl.BlockSpec((None, r, d), lambda i: ((i * tm) // rows_per_mod, 0, 0)),
                  pl.BlockSpec((1, d), lambda i: (0, 0)),
                  const((A_WIDTH, d)), const((d, d)), const((d, d))],
        out_specs=row(d, 0),
        out_shape=jax.ShapeDtypeStruct((t, d), F32),
        compiler_params=_cparams("arbitrary"),
        name="mix_out",
    )(oa, yn, rest, rest, x2d, gate3, g_post.reshape(1, d), w_o_a, w_o_b, w_o)


def _window_kernel(k_ref, v_ref, o_ref, *, rows):
    per_row = 2 * A_HEADS
    for kind, src in enumerate((k_ref, v_ref)):
        for h in range(A_HEADS):
            o_ref[pl.ds(kind * A_HEADS + h, rows, stride=per_row), :] = src[h]


def _window_rows(qkv, g, b, s, w):
    nh = A_GROUPS * A_HEADS
    rows = min(w, WINDOW_ROWS)
    per_row = 2 * A_HEADS
    assert w % rows == 0 and (s - w) % rows == 0
    src = lambda kind: pl.BlockSpec(
        (A_HEADS, rows, A_HEAD_DIM),
        lambda bi, i: (kind * A_GROUPS + g, (bi * s + s - w) // rows + i, 0))
    out = pl.pallas_call(
        functools.partial(_window_kernel, rows=rows),
        grid=(b, w // rows),
        in_specs=[src(1), src(2)],
        out_specs=pl.BlockSpec((None, rows * per_row, A_HEAD_DIM), lambda bi, i: (bi, i, 0)),
        out_shape=jax.ShapeDtypeStruct((b, w * per_row, A_HEAD_DIM), F32),
        compiler_params=_cparams("arbitrary", "arbitrary"),
        name=f"window{g}",
    )(qkv, qkv)
    return out.reshape(b, w, 2, A_HEADS, A_HEAD_DIM)


def _prep_w_kernel(tbl_ref, a_ref, b_ref, dt_ref, w_ref, wdt_ref):
    j = pl.program_id(0)
    d = a_ref.shape[1]
    late = tbl_ref[1, j]
    t = LANES

    def put(r, tile_rows):
        for c in range(d // t):
            w_ref[c * t:(c + 1) * t, r * t:(r + 1) * t] = tile_rows[:, c * t:(c + 1) * t].T.astype(BF16)

    @pl.when(late == 0)
    def _():
        for r in range(COL // t):
            put(r, a_ref[r * t:(r + 1) * t, :])

    @pl.when(late != 0)
    def _():
        for r in range(COL // t - 1):
            put(r, a_ref[r * t + SSM_HEADS:(r + 1) * t + SSM_HEADS, :])
        put(COL // t - 1, jnp.concatenate([a_ref[COL - t + SSM_HEADS:COL, :], b_ref[...]], axis=0))

    @pl.when(j == 0)
    def _():
        lane = lax.broadcasted_iota(jnp.int32, (t, t), 1)
        for c in range(d // t):
            blk = dt_ref[:, c * t:(c + 1) * t].T
            wdt_ref[c * t:(c + 1) * t, :] = jnp.where(lane < SSM_HEADS, blk, 0.0).astype(BF16)


def _prep_w(w_t):
    d = w_t.shape[1]
    assert W_DT % COL == 0 and W_RB - W_RA == D_MODEL and DT_PAD == LANES and SSM_HEADS % SUBLANES == 0
    blk = lambda off, n: [off // COL + k for k in range(n // COL)]
    plain = blk(0, N_QKV_COLS) + blk(W_XBC, SSM_CONV_DIM) + blk(W_ZB, SSM_D_INNER)
    late = blk(W_DT, 2 * D_MODEL)
    za = blk(W_ZA, A_WIDTH)
    src = plain + late + za
    is_late = [0] * len(plain) + [1] * len(late) + [0] * len(za)
    nxt = [0] * len(plain) + [(k + 1) * COL // SSM_HEADS for k in late] + [0] * len(za)
    tbl = jnp.asarray(np.array([src, is_late, nxt], np.int32))
    n = len(src)
    grid_spec = pltpu.PrefetchScalarGridSpec(
        num_scalar_prefetch=1,
        grid=(n,),
        in_specs=[pl.BlockSpec((COL, d), lambda j, t: (t[0, j], 0)),
                  pl.BlockSpec((SSM_HEADS, d), lambda j, t: (t[2, j], 0)),
                  pl.BlockSpec((DT_PAD, d), lambda j, t: (W_DT // DT_PAD, 0))],
        out_specs=[pl.BlockSpec((d, COL), lambda j, t: (0, j)),
                   pl.BlockSpec((d, DT_PAD), lambda j, t: (0, 0))])
    return pl.pallas_call(
        _prep_w_kernel,
        grid_spec=grid_spec,
        out_shape=[jax.ShapeDtypeStruct((d, n * COL), BF16), jax.ShapeDtypeStruct((d, DT_PAD), BF16)],
        compiler_params=_cparams("arbitrary"),
        name="prep_w",
    )(tbl, w_t, w_t, w_t)


def _layer(depth_i, x_p, x_s, caches, conv_s, ssm_s, c_p, c_s, w_ada, b_ada, g_pre, g_post, w_in, conv_w,
           conv_b, dt_bias, a_log, d_skip, g_ssm, w_o_a, w_o_b, w_o):
    bp, sp, d = x_p.shape
    bs, ts, _ = x_s.shape
    assert sp % SSM_CHUNK == 0

    w_main, w_dt = _prep_w(jnp.swapaxes(w_in, 0, 1))
    woa_b, wob_b, wo_b = w_o_a.astype(BF16), w_o_b.astype(BF16), w_o.astype(BF16)

    nmod = bp + bs
    mpad = -(-nmod // SUBLANES) * SUBLANES
    c_all = jnp.pad(jnp.concatenate([c_p, c_s], axis=0), ((0, mpad - nmod), (0, 0)))
    mod = _ada(c_all, w_ada, b_ada)
    shift, scale, gate = mod[:, :d], mod[:, d:2 * d], mod[:, 2 * d:]
    per_seq = lambda v: v[:bp].reshape(bp, 1, d)
    per_row = lambda v: jnp.repeat(v[bp:nmod], ts, axis=0).reshape(1, bs * ts, d)

    tp = bp * sp
    xp2 = x_p.reshape(tp, d)
    qkv_p, rest_p, dt_p, *kv_wide = _in_proj(xp2, per_seq(scale), per_seq(shift), g_pre, w_main, w_dt, IN_PROJ_ROWS, sp,
                                             caches[-1:], depth_i, ts)
    oa_p = _attn_prompt(qkv_p, rest_p, bp, sp)
    conv0 = jnp.zeros((bp, SSM_CONV_W - 1, SSM_CONV_DIM), F32)
    ssm0 = jnp.zeros((bp, SSM_HEADS, SSM_HEAD_DIM, SSM_D_STATE), F32)
    yn_p, conv_p, ssm_p, *kv_small = _ssd(rest_p.reshape(bp, sp, N_REST), dt_p.reshape(bp, sp, DT_PAD), conv0, ssm0,
                                          conv_w, conv_b, dt_bias, a_log, d_skip, g_ssm, SSM_CHUNK,
                                          caches[:-1], depth_i, ts)
    kv_shifted = kv_small + kv_wide
    y_p = _mix_out(oa_p, yn_p, rest_p, xp2, per_seq(gate), g_post, woa_b, wob_b, wo_b, MIX_OUT_ROWS, sp).reshape(bp, sp, d)
    kv_p = [_window_rows(qkv_p, g, bp, sp, min(A_WINDOWS[g], sp)) for g in range(A_GROUPS)]

    tsn = bs * ts
    xs2 = x_s.reshape(tsn, d)
    qkv_s, rest_s, dt_s = _in_proj(xs2, per_row(scale), per_row(shift), g_pre, w_main, w_dt, tsn, tsn)[:3]
    qkvn = jnp.transpose(qkv_s.reshape(3, A_GROUPS, A_HEADS, bs, ts, A_HEAD_DIM), (0, 1, 3, 4, 2, 5))
    za5 = rest_s[:, ZA_OFF:ZA_OFF + A_WIDTH].reshape(bs, ts, A_HEADS, A_HEAD_DIM)
    oa5, kv_s = _attn_decode(qkvn, za5, caches, kv_shifted, depth_i)
    oa_s = oa5.reshape(tsn, A_WIDTH).astype(BF16)
    yn_s, conv_sn, ssm_sn = _ssd(rest_s.reshape(bs, ts, N_REST), dt_s.reshape(bs, ts, DT_PAD), conv_s, ssm_s,
                                 conv_w, conv_b, dt_bias, a_log, d_skip, g_ssm, ts)
    y_s = _mix_out(oa_s, yn_s, rest_s, xs2, per_row(gate), g_post, woa_b, wob_b, wo_b, tsn, tsn).reshape(bs, ts, d)
    return y_p, y_s, kv_p, conv_p, ssm_p, list(kv_s), conv_sn, ssm_sn


def kernel(x_prompt, x_sample, cache_a_w128, cache_a_w512, cache_a_w2048, state_conv, state_ssm, c_prompt, c_sample, w_ada, b_ada, g_pre, g_post, w_in, conv_w, conv_b, dt_bias, a_log, d_skip, g_ssm, w_o_a, w_o_b, w_o):
    depth = w_in.shape[0]
    caches = (cache_a_w128, cache_a_w512, cache_a_w2048)
    y_p, y_s = x_prompt, x_sample
    acc = [[] for _ in range(10)]
    for i in range(depth):
        y_p, y_s, kv_p, conv_p, ssm_p, kv_s, conv_s, ssm_s = _layer(
            i, y_p, y_s, caches, state_conv[i], state_ssm[i],
            c_prompt, c_sample, w_ada[i], b_ada[i], g_pre[i], g_post[i], w_in[i], conv_w[i], conv_b[i],
            dt_bias[i], a_log[i], d_skip[i], g_ssm[i], w_o_a[i], w_o_b[i], w_o[i])
        for lst, v in zip(acc, (*kv_p, conv_p, ssm_p, *kv_s, conv_s, ssm_s)):
            lst.append(v)
    return (y_p, y_s, *[jnp.stack(v) for v in acc])
```
